```python
import jax, jax.numpy as jnp
from jax import lax
import numpy as np

D_MODEL = 2048
BATCH = 8
SEQ = 4096
DEPTH = 2

MLA_HEADS = 8
MLA_NOPE = 128
MLA_ROPE = 64
MLA_QK = MLA_NOPE + MLA_ROPE
MLA_V = 128
MLA_Q_RANK = 512
MLA_KV_RANK = 512
MLA_WIDTH = MLA_HEADS * MLA_V
ROPE_THETA = 10000.0
Q_BLOCK = 128
SGU_GROUPS = 8
SGU_CHUNK = 128
SGU_WIDTH = 1024
SGU_GROUP_DIM = SGU_WIDTH // SGU_GROUPS
GLA_HEADS = 4
GLA_DK = 128
GLA_DV = 256
GLA_GATE_RANK = 16
GLA_TAU = 16.0
GLA_CHUNK = 128
GLA_WIDTH = GLA_HEADS * GLA_DV
N_BRANCH = 3
BRANCH_WIDTH = 1024
EPS = 1e-6

IN_SIZES = (MLA_Q_RANK, MLA_KV_RANK, MLA_ROPE, MLA_WIDTH,
            SGU_WIDTH, SGU_WIDTH, SGU_WIDTH,
            GLA_HEADS * GLA_DK, GLA_HEADS * GLA_DK, GLA_WIDTH, GLA_GATE_RANK, GLA_WIDTH,
            N_BRANCH * D_MODEL)
IN_COLS = sum(IN_SIZES)
IN_SPLITS = tuple(int(c) for c in np.cumsum(IN_SIZES)[:-1])

kernel_name = "hybrid_mla_sgu_gla_gated_merge"


def rms_norm(x, g):
    xf = x.astype(jnp.float32)
    y = xf * lax.rsqrt(jnp.mean(xf * xf, axis=-1, keepdims=True) + EPS)
    return (y * g.astype(jnp.float32)).astype(x.dtype)


def rope_tables(positions, dtype):
    inv_freq = 1.0 / (ROPE_THETA ** (jnp.arange(0, MLA_ROPE, 2, dtype=jnp.float32) / MLA_ROPE))
    ang = positions.astype(jnp.float32)[..., None] * inv_freq
    return jnp.cos(ang).astype(dtype)[:, :, None, :], jnp.sin(ang).astype(dtype)[:, :, None, :]


def apply_rope(x, cos, sin):
    half = x.shape[-1] // 2
    x1, x2 = x[..., :half], x[..., half:]
    return jnp.concatenate([x1 * cos - x2 * sin, x2 * cos + x1 * sin], axis=-1)


def causal_block_attention(q, k, v):
    S = q.shape[1]
    scale = MLA_QK ** -0.5
    neg = jnp.finfo(jnp.float32).min
    outs = []
    for i in range(S // Q_BLOCK):
        q0, q1 = i * Q_BLOCK, (i + 1) * Q_BLOCK
        qb, kb, vb = q[:, q0:q1], k[:, :q1], v[:, :q1]
        s = jnp.einsum('bqhd,bkhd->bhqk', qb, kb).astype(jnp.float32) * scale
        mask = (q0 + jnp.arange(Q_BLOCK))[:, None] >= jnp.arange(q1)[None, :]
        p = jax.nn.softmax(jnp.where(mask, s, neg), axis=-1).astype(v.dtype)
        outs.append(jnp.einsum('bhqk,bkhd->bqhd', p, vb))
    return jnp.concatenate(outs, axis=1)


def mla_branch(c_q, c_kv, k_rope, cq_g, ckv_g, w_uq, w_ukv, q_g, k_g, cos, sin):
    B, S, _ = c_q.shape
    q = (rms_norm(c_q, cq_g) @ w_uq).reshape(B, S, MLA_HEADS, MLA_QK)
    kv = (rms_norm(c_kv, ckv_g) @ w_ukv).reshape(B, S, MLA_HEADS, MLA_NOPE + MLA_V)
    k_nope, v = kv[..., :MLA_NOPE], kv[..., MLA_NOPE:]
    k = jnp.concatenate([k_nope, jnp.broadcast_to(k_rope[:, :, None, :], (B, S, MLA_HEADS, MLA_ROPE))], axis=-1)
    q = rms_norm(q, q_g)
    k = rms_norm(k, k_g)
    q = jnp.concatenate([q[..., :MLA_NOPE], apply_rope(q[..., MLA_NOPE:], cos, sin)], axis=-1)
    k = jnp.concatenate([k[..., :MLA_NOPE], apply_rope(k[..., MLA_NOPE:], cos, sin)], axis=-1)
    o = causal_block_attention(q, k, v)
    return o.reshape(B, S, MLA_WIDTH)


def sgu_branch(u, v, v_g, w_s, b_s):
    B, S, _ = u.shape
    nc = S // SGU_CHUNK
    v = rms_norm(v, v_g).reshape(B, nc, SGU_CHUNK, SGU_GROUPS, SGU_GROUP_DIM)
    mix = jnp.einsum('gts,bcsgd->bctgd', jnp.tril(w_s), v) + b_s.T[:, :, None]
    return u * mix.reshape(B, S, SGU_WIDTH)


def gla_branch(q, k, v, a_r, w_a2, b_a, o_g):
    B, S, _ = q.shape
    nc = S // GLA_CHUNK
    L = GLA_CHUNK
    log_a = jax.nn.log_sigmoid((a_r @ w_a2 + b_a).astype(jnp.float32)) / GLA_TAU

    def chunks(t, d):
        return t.reshape(B, nc, L, GLA_HEADS, d).transpose(1, 0, 3, 2, 4)

    qc = chunks(q.astype(jnp.float32) * GLA_DK ** -0.5, GLA_DK)
    kc = chunks(k.astype(jnp.float32), GLA_DK)
    vc = chunks(v.astype(jnp.float32), GLA_DV)
    bc = jnp.cumsum(chunks(log_a, GLA_DK), axis=3)
    mask = jnp.tril(jnp.ones((L, L), dtype=bool))[:, :, None]

    def step(state, inp):
        q_, k_, v_, b_ = inp
        o_inter = jnp.einsum('bhtk,bhkv->bhtv', q_ * jnp.exp(b_), state)
        diff = b_[:, :, :, None, :] - b_[:, :, None, :, :]
        decay = jnp.where(mask, jnp.exp(jnp.where(mask, diff, 0.0)), 0.0)
        attn = jnp.einsum('bhtk,bhtsk,bhsk->bhts', q_, decay, k_)
        o = o_inter + jnp.einsum('bhts,bhsv->bhtv', attn, v_)
        b_last = b_[:, :, -1:, :]
        new_state = jnp.exp(b_last[:, :, 0, :])[..., None] * state + \
            jnp.einsum('bhsk,bhsv->bhkv', k_ * jnp.exp(b_last - b_), v_)
        return new_state, o

    state0 = jnp.zeros((B, GLA_HEADS, GLA_DK, GLA_DV), jnp.float32)
    _, o = lax.scan(step, state0, (qc, kc, vc, bc))
    o = o.transpose(1, 0, 3, 2, 4).reshape(B, S, GLA_HEADS, GLA_DV)
    o = rms_norm(o, o_g).astype(v.dtype)
    return o.reshape(B, S, GLA_WIDTH)


def hybrid_layer(x, cos, sin, norm_g, w_in, cq_g, ckv_g, w_uq, w_ukv, q_g, k_g,
                 v_g, w_s, b_s, w_a2, b_a, o_g, w_branch, w_out):
    B, S, D = x.shape
    h = rms_norm(x, norm_g)
    proj = h @ w_in
    (c_q, c_kv, k_rope, z_a, u_b, v_b, z_b, q_c, k_c, v_c, a_r, z_c, gate_logits) = \
        jnp.split(proj, IN_SPLITS, axis=-1)
    y_a = mla_branch(c_q, c_kv, k_rope, cq_g, ckv_g, w_uq, w_ukv, q_g, k_g, cos, sin) * jax.nn.silu(z_a)
    y_b = sgu_branch(jax.nn.gelu(u_b), jax.nn.gelu(v_b), v_g, w_s, b_s) * jax.nn.silu(z_b)
    y_c = gla_branch(q_c, k_c, v_c, a_r, w_a2, b_a, o_g) * jax.nn.silu(z_c)
    g = jax.nn.sigmoid(gate_logits.astype(jnp.float32)).astype(x.dtype).reshape(B, S, N_BRANCH, D)
    merged = (g[:, :, 0] * (y_a @ w_branch[0])
              + g[:, :, 1] * (y_b @ w_branch[1])
              + g[:, :, 2] * (y_c @ w_branch[2]))
    return x + merged @ w_out


def setup_inputs(seed: int = 0) -> dict:
    key = jax.random.key(seed)
    ks = jax.random.split(key, 20)
    f32 = jnp.float32

    def nrm(k, shape, fan_in, mult=1.0):
        return jax.random.normal(k, shape, f32) * (fan_in ** -0.5) * mult

    def gain(k, shape):
        return 1.0 + 0.02 * jax.random.normal(k, shape, f32)

    x = jax.random.normal(ks[0], (BATCH, SEQ, D_MODEL), f32)
    offset = jax.random.randint(ks[1], (BATCH, 1), 0, SEQ, dtype=jnp.int32)
    positions = (offset + jnp.arange(SEQ, dtype=jnp.int32)[None, :]).astype(jnp.int32)
    return {
        "x": x,
        "positions": positions,
        "norm_g": gain(ks[2], (DEPTH, D_MODEL)),
        "w_in": nrm(ks[3], (DEPTH, D_MODEL, IN_COLS), D_MODEL),
        "mla_cq_norm": gain(ks[4], (DEPTH, MLA_Q_RANK)),
        "mla_ckv_norm": gain(ks[5], (DEPTH, MLA_KV_RANK)),
        "mla_w_uq": nrm(ks[6], (DEPTH, MLA_Q_RANK, MLA_HEADS * MLA_QK), MLA_Q_RANK),
        "mla_w_ukv": nrm(ks[7], (DEPTH, MLA_KV_RANK, MLA_HEADS * (MLA_NOPE + MLA_V)), MLA_KV_RANK),
        "mla_q_norm": gain(ks[8], (DEPTH, MLA_QK)),
        "mla_k_norm": gain(ks[9], (DEPTH, MLA_QK)),
        "sgu_v_norm": gain(ks[10], (DEPTH, SGU_WIDTH)),
        "sgu_w_s": nrm(ks[11], (DEPTH, SGU_GROUPS, SGU_CHUNK, SGU_CHUNK), SGU_CHUNK),
        "sgu_b_s": 1.0 + 0.02 * jax.random.normal(ks[12], (DEPTH, SGU_GROUPS, SGU_CHUNK), f32),
        "gla_w_a2": nrm(ks[13], (DEPTH, GLA_GATE_RANK, GLA_HEADS * GLA_DK), GLA_GATE_RANK),
        "gla_b_a": 0.1 * jax.random.normal(ks[14], (DEPTH, GLA_HEADS * GLA_DK), f32),
        "gla_o_norm": gain(ks[15], (DEPTH, GLA_DV)),
        "w_branch": nrm(ks[16], (DEPTH, N_BRANCH, BRANCH_WIDTH, D_MODEL), BRANCH_WIDTH),
        "w_out": nrm(ks[17], (DEPTH, D_MODEL, D_MODEL), D_MODEL, 0.5),
    }


def reference(x, positions, norm_g, w_in, mla_cq_norm, mla_ckv_norm, mla_w_uq, mla_w_ukv,
              mla_q_norm, mla_k_norm, sgu_v_norm, sgu_w_s, sgu_b_s, gla_w_a2, gla_b_a,
              gla_o_norm, w_branch, w_out):
    cos, sin = rope_tables(positions, x.dtype)
    for l in range(DEPTH):
        x = hybrid_layer(x, cos, sin, norm_g[l], w_in[l], mla_cq_norm[l], mla_ckv_norm[l],
                         mla_w_uq[l], mla_w_ukv[l], mla_q_norm[l], mla_k_norm[l],
                         sgu_v_norm[l], sgu_w_s[l], sgu_b_s[l], gla_w_a2[l], gla_b_a[l],
                         gla_o_norm[l], w_branch[l], w_out[l])
    return x
```

```python
import functools
import math

import jax
import jax.numpy as jnp
from jax import lax
from jax.experimental import pallas as pl
from jax.experimental.pallas import tpu as pltpu

F32 = jnp.float32
BF16 = jnp.bfloat16

MLA_HEADS = 8
MLA_NOPE = 128
MLA_ROPE = 64
MLA_QK = MLA_NOPE + MLA_ROPE
MLA_V = 128
MLA_Q_RANK = 512
MLA_KV_RANK = 512
MLA_WIDTH = MLA_HEADS * MLA_V
MLA_QK_PAD = 256
ROPE_THETA = 10000.0
SGU_GROUPS = 8
SGU_CHUNK = 128
SGU_WIDTH = 1024
GLA_HEADS = 4
GLA_DK = 128
GLA_DV = 256
GLA_GATE_RANK = 16
GLA_TAU = 16.0
GLA_CHUNK = 128
GLA_WIDTH = GLA_HEADS * GLA_DV
N_BRANCH = 3
BRANCH_WIDTH = 1024
EPS = 1e-6
LANE = 128
SMALL_W = LANE

PROJ_TN = 1024
ACT_NONE, ACT_SILU, ACT_GELU, ACT_SIGMOID = 0, 1, 2, 3
COL_CQKV, COL_QK_C, COL_V_C, COL_Z_A, COL_Z_B, COL_Z_C, COL_U_B, COL_V_B, COL_GATE = range(9)
PROJ_KINDS = (ACT_NONE,) * 3 + (ACT_SILU,) * 3 + (ACT_GELU,) * 2 + (ACT_SIGMOID,) * 6
PROJ_COLS = PROJ_TN * len(PROJ_KINDS)

VMEM_LIMIT = 56 * 1024 * 1024


def _params(*sem):
    return pltpu.CompilerParams(dimension_semantics=sem, vmem_limit_bytes=VMEM_LIMIT)


def _dot(a, b):
    return jnp.dot(a, b, preferred_element_type=F32)


def _dot_nt(a, b):
    return lax.dot_general(a, b, (((1,), (1,)), ((), ())), preferred_element_type=F32)


def _dot_tn(a, b):
    return lax.dot_general(a, b, (((0,), (0,)), ((), ())), preferred_element_type=F32)


def _sigmoid(x):
    return 1.0 / (1.0 + jnp.exp(-x))


def _gelu_tanh(x):
    c = math.sqrt(2.0 / math.pi)
    return 0.5 * x * (1.0 + jnp.tanh(c * (x + 0.044715 * (x * x * x))))


def _resident(shape):
    nd = len(shape)
    return pl.BlockSpec(shape, lambda *_: (0,) * nd, pipeline_mode=pl.Buffered(1))


def _norm_kernel(x_ref, g_ref, ws_ref, h_ref, small_ref):
    x = x_ref[...]
    rstd = lax.rsqrt(jnp.mean(x * x, axis=-1, keepdims=True) + EPS)
    h = (x * rstd * g_ref[...]).astype(BF16)
    h_ref[...] = h
    small_ref[...] = _dot(h, ws_ref[...])


def _norm_call(x2, g, w_small, tm):
    T, D = x2.shape
    return pl.pallas_call(
        _norm_kernel,
        grid=(T // tm,),
        in_specs=[pl.BlockSpec((tm, D), lambda i: (i, 0)),
                  _resident((1, D)),
                  _resident((D, SMALL_W))],
        out_specs=[pl.BlockSpec((tm, D), lambda i: (i, 0)),
                   pl.BlockSpec((tm, SMALL_W), lambda i: (i, 0))],
        out_shape=[jax.ShapeDtypeStruct((T, D), BF16),
                   jax.ShapeDtypeStruct((T, SMALL_W), F32)],
        compiler_params=_params("parallel"),
    )(x2, g, w_small)


def _inproj_kernel(h_ref, w_ref, o_ref):
    j = pl.program_id(1)
    acc = _dot(h_ref[...], w_ref[...])
    n_none = PROJ_KINDS.count(ACT_NONE)
    n_silu = n_none + PROJ_KINDS.count(ACT_SILU)
    n_gelu = n_silu + PROJ_KINDS.count(ACT_GELU)

    @pl.when(j < n_none)
    def _():
        o_ref[...] = acc.astype(o_ref.dtype)

    @pl.when((j >= n_none) & (j < n_silu))
    def _():
        o_ref[...] = (acc * _sigmoid(acc)).astype(o_ref.dtype)

    @pl.when((j >= n_silu) & (j < n_gelu))
    def _():
        o_ref[...] = _gelu_tanh(acc).astype(o_ref.dtype)

    @pl.when(j >= n_gelu)
    def _():
        o_ref[...] = _sigmoid(acc).astype(o_ref.dtype)


def _inproj_call(h, w_main, tm):
    T, D = h.shape
    return pl.pallas_call(
        _inproj_kernel,
        grid=(T // tm, PROJ_COLS // PROJ_TN),
        in_specs=[pl.BlockSpec((tm, D), lambda i, j: (i, 0)),
                  pl.BlockSpec((D, PROJ_TN), lambda i, j: (0, j))],
        out_specs=pl.BlockSpec((tm, PROJ_TN), lambda i, j: (i, j)),
        out_shape=jax.ShapeDtypeStruct((T, PROJ_COLS), BF16),
        compiler_params=_params("parallel", "arbitrary"),
    )(h, w_main)


def _rope128(r, cosf, sins):
    lane = lax.broadcasted_iota(jnp.int32, r.shape, 1)
    partner = jnp.where(lane < MLA_ROPE // 2,
                        pltpu.roll(r, LANE - MLA_ROPE // 2, axis=1),
                        pltpu.roll(r, MLA_ROPE // 2, axis=1))
    return r * cosf + partner * sins


def _mla_prep_kernel(cq_ref, ckv_ref, small_ref, cos_ref, sin_ref,
                     cqg_ref, ckvg_ref, wuq_ref, wuk_ref, wuv_ref, qg_ref, kg_ref,
                     q_ref, k_ref, v_ref):
    cosf = cos_ref[...]
    sins = sin_ref[...]
    scale = MLA_QK ** -0.5

    cq = cq_ref[...].astype(F32)
    nq = cq * lax.rsqrt(jnp.mean(cq * cq, axis=-1, keepdims=True) + EPS) * cqg_ref[...]
    q_all = _dot(nq.astype(BF16), wuq_ref[...])

    ckv = ckv_ref[...].astype(F32)
    nkv = ckv * lax.rsqrt(jnp.mean(ckv * ckv, axis=-1, keepdims=True) + EPS) * ckvg_ref[...]
    nkv = nkv.astype(BF16)
    k_all = _dot(nkv, wuk_ref[...])
    v_ref[...] = _dot(nkv, wuv_ref[...]).astype(v_ref.dtype)

    qg = qg_ref[...]
    kg = kg_ref[...]
    lane = lax.broadcasted_iota(jnp.int32, (1, LANE), 1)
    kr = jnp.where(lane < MLA_ROPE, small_ref[...], 0.0)
    kr_ss = jnp.sum(kr * kr, axis=-1, keepdims=True)
    kr_rot = _rope128(kr * kg[:, MLA_NOPE:], cosf, sins)

    for hd in range(MLA_HEADS):
        qb = q_all[:, hd * MLA_QK_PAD:(hd + 1) * MLA_QK_PAD]
        rq = lax.rsqrt(jnp.sum(qb * qb, axis=-1, keepdims=True) * (1.0 / MLA_QK) + EPS) * scale
        qn = qb * rq * qg
        q_ref[:, hd * MLA_QK_PAD:hd * MLA_QK_PAD + MLA_NOPE] = qn[:, :MLA_NOPE].astype(q_ref.dtype)
        q_ref[:, hd * MLA_QK_PAD + MLA_NOPE:(hd + 1) * MLA_QK_PAD] = _rope128(
            qn[:, MLA_NOPE:], cosf, sins).astype(q_ref.dtype)

        kb = k_all[:, hd * MLA_NOPE:(hd + 1) * MLA_NOPE]
        rk = lax.rsqrt((jnp.sum(kb * kb, axis=-1, keepdims=True) + kr_ss) * (1.0 / MLA_QK) + EPS)
        k_ref[:, hd * MLA_QK_PAD:hd * MLA_QK_PAD + MLA_NOPE] = (
            kb * rk * kg[:, :MLA_NOPE]).astype(k_ref.dtype)
        k_ref[:, hd * MLA_QK_PAD + MLA_NOPE:(hd + 1) * MLA_QK_PAD] = (kr_rot * rk).astype(k_ref.dtype)


def _mla_prep_call(proj, small, cosf, sins, cqg, ckvg, wuq, wuk, wuv, qg, kg, tm):
    T = proj.shape[0]
    HQ = MLA_HEADS * MLA_QK_PAD
    return pl.pallas_call(
        _mla_prep_kernel,
        grid=(T // tm,),
        in_specs=[pl.BlockSpec((tm, MLA_Q_RANK), lambda i: (i, 0)),
                  pl.BlockSpec((tm, MLA_KV_RANK), lambda i: (i, 1)),
                  pl.BlockSpec((tm, SMALL_W), lambda i: (i, 0)),
                  pl.BlockSpec((tm, LANE), lambda i: (i, 0)),
                  pl.BlockSpec((tm, LANE), lambda i: (i, 0)),
                  _resident((1, MLA_Q_RANK)),
                  _resident((1, MLA_KV_RANK)),
                  _resident((MLA_Q_RANK, HQ)),
                  _resident((MLA_KV_RANK, MLA_HEADS * MLA_NOPE)),
                  _resident((MLA_KV_RANK, MLA_WIDTH)),
                  _resident((1, MLA_QK_PAD)),
                  _resident((1, MLA_QK_PAD))],
        out_specs=[pl.BlockSpec((tm, HQ), lambda i: (i, 0)),
                   pl.BlockSpec((tm, HQ), lambda i: (i, 0)),
                   pl.BlockSpec((tm, MLA_WIDTH), lambda i: (i, 0))],
        out_shape=[jax.ShapeDtypeStruct((T, HQ), BF16),
                   jax.ShapeDtypeStruct((T, HQ), BF16),
                   jax.ShapeDtypeStruct((T, MLA_WIDTH), BF16)],
        compiler_params=_params("parallel"),
    )(proj, proj, small, cosf, sins, cqg, ckvg, wuq, wuk, wuv, qg, kg)


def _attn_kernel(q_ref, k_ref, v_ref, sz_ref, o_ref, m_ref, l_ref, acc_ref, *, tq):
    qi = pl.program_id(2)
    q = q_ref[...]
    m_ref[...] = jnp.full(m_ref.shape, -jnp.inf, F32)
    l_ref[...] = jnp.zeros(l_ref.shape, F32)
    acc_ref[...] = jnp.zeros(acc_ref.shape, F32)

    def step(kv, masked):
        k = k_ref[pl.ds(pl.multiple_of(kv * tq, tq), tq), :]
        v = v_ref[pl.ds(pl.multiple_of(kv * tq, tq), tq), :]
        s = _dot_nt(q, k)
        if masked:
            row = lax.broadcasted_iota(jnp.int32, s.shape, 0)
            col = lax.broadcasted_iota(jnp.int32, s.shape, 1)
            s = jnp.where(row >= col, s, -jnp.inf)
        m_old = m_ref[...]
        m_new = jnp.maximum(m_old, jnp.max(s, axis=-1, keepdims=True))
        alpha = jnp.exp(m_old - m_new)
        p = jnp.exp(s - m_new)
        l_ref[...] = alpha * l_ref[...] + jnp.sum(p, axis=-1, keepdims=True)
        acc_ref[...] = alpha * acc_ref[...] + _dot(p.astype(v.dtype), v)
        m_ref[...] = m_new

    def body(kv, c):
        step(kv, False)
        return c

    lax.fori_loop(0, qi, body, 0)
    step(qi, True)
    o_ref[...] = (acc_ref[...] / l_ref[...] * sz_ref[...].astype(F32)).astype(o_ref.dtype)


def _attn_call(q, k, v, proj, B, S, tq):
    T = q.shape[0]
    nq = S // tq
    sz_col0 = COL_Z_A * PROJ_TN // MLA_V
    return pl.pallas_call(
        functools.partial(_attn_kernel, tq=tq),
        grid=(B, MLA_HEADS, nq),
        in_specs=[pl.BlockSpec((tq, MLA_QK_PAD), lambda b, h, i: (b * nq + i, h)),
                  pl.BlockSpec((S, MLA_QK_PAD), lambda b, h, i: (b, h)),
                  pl.BlockSpec((S, MLA_V), lambda b, h, i: (b, h)),
                  pl.BlockSpec((tq, MLA_V), lambda b, h, i: (b * nq + i, sz_col0 + h))],
        out_specs=pl.BlockSpec((tq, MLA_V), lambda b, h, i: (b * nq + i, h)),
        out_shape=jax.ShapeDtypeStruct((T, MLA_WIDTH), BF16),
        scratch_shapes=[pltpu.VMEM((tq, 1), F32), pltpu.VMEM((tq, 1), F32),
                        pltpu.VMEM((tq, MLA_V), F32)],
        compiler_params=_params("parallel", "parallel", "arbitrary"),
    )(q, k, v, proj)


def _sgu_kernel(u_ref, v_ref, sz_ref, vg_ref, ws_ref, bs_ref, o_ref, *, tm):
    v = v_ref[...].astype(F32)
    vn = (v * lax.rsqrt(jnp.mean(v * v, axis=-1, keepdims=True) + EPS) * vg_ref[...]).astype(BF16)
    row = lax.broadcasted_iota(jnp.int32, (SGU_CHUNK, SGU_CHUNK), 0)
    col = lax.broadcasted_iota(jnp.int32, (SGU_CHUNK, SGU_CHUNK), 1)
    gd = SGU_WIDTH // SGU_GROUPS
    for g in range(SGU_GROUPS):
        w = jnp.where(row >= col, ws_ref[g], 0.0).astype(BF16)
        bias = bs_ref[:, g:g + 1]
        for c in range(tm // SGU_CHUNK):
            rs = slice(c * SGU_CHUNK, (c + 1) * SGU_CHUNK)
            cs = slice(g * gd, (g + 1) * gd)
            mix = _dot(w, vn[rs, cs]) + bias
            o_ref[rs, cs] = (u_ref[rs, cs].astype(F32) * mix
                             * sz_ref[rs, cs].astype(F32)).astype(o_ref.dtype)


def _sgu_call(proj, vg, ws, bs_t, tm):
    T = proj.shape[0]
    return pl.pallas_call(
        functools.partial(_sgu_kernel, tm=tm),
        grid=(T // tm,),
        in_specs=[pl.BlockSpec((tm, SGU_WIDTH), lambda i: (i, COL_U_B)),
                  pl.BlockSpec((tm, SGU_WIDTH), lambda i: (i, COL_V_B)),
                  pl.BlockSpec((tm, SGU_WIDTH), lambda i: (i, COL_Z_B)),
                  _resident((1, SGU_WIDTH)),
                  _resident((SGU_GROUPS, SGU_CHUNK, SGU_CHUNK)),
                  _resident((SGU_CHUNK, SGU_GROUPS))],
        out_specs=pl.BlockSpec((tm, SGU_WIDTH), lambda i: (i, 0)),
        out_shape=jax.ShapeDtypeStruct((T, SGU_WIDTH), BF16),
        compiler_params=_params("parallel"),
    )(proj, proj, proj, vg, ws, bs_t)


def _gla_kernel(qk_ref, v_ref, sz_ref, small_ref, wa_ref, ba_ref, og_ref, o_ref, st_ref):
    L = GLA_CHUNK

    @pl.when(pl.program_id(1) == 0)
    def _():
        st_ref[...] = jnp.zeros(st_ref.shape, F32)

    xg = jnp.dot(small_ref[...], wa_ref[...], preferred_element_type=F32,
                 precision=lax.Precision.HIGHEST) + ba_ref[...]
    log_a = (jnp.minimum(xg, 0.0) - jnp.log(1.0 + jnp.exp(-jnp.abs(xg)))) * (1.0 / GLA_TAU)
    row = lax.broadcasted_iota(jnp.int32, (L, L), 0)
    col = lax.broadcasted_iota(jnp.int32, (L, L), 1)
    causal = row >= col
    bcum = jnp.dot(causal.astype(F32), log_a, preferred_element_type=F32,
                   precision=lax.Precision.HIGHEST)
    og = og_ref[...]

    for hd in range(GLA_HEADS):
        ks = slice(hd * GLA_DK, (hd + 1) * GLA_DK)
        vs = slice(hd * GLA_DV, (hd + 1) * GLA_DV)
        b = bcum[:, ks]
        b_mid = b[L // 2:L // 2 + 1, :]
        b_last = b[L - 1:L, :]
        q = qk_ref[:, ks].astype(F32) * (GLA_DK ** -0.5)
        k = qk_ref[:, GLA_HEADS * GLA_DK + hd * GLA_DK:GLA_HEADS * GLA_DK + (hd + 1) * GLA_DK].astype(F32)
        v = v_ref[:, vs]
        q_t = (q * jnp.exp(b - b_mid)).astype(BF16)
        k_t = (k * jnp.exp(b_mid - b)).astype(BF16)
        attn = jnp.where(causal, _dot_nt(q_t, k_t), 0.0).astype(BF16)
        st = st_ref[hd]
        o = _dot_nt((q * jnp.exp(b)).astype(BF16), st.astype(BF16)) + _dot(attn, v)
        k_s = (k * jnp.exp(b_last - b)).astype(BF16)
        st_ref[hd] = jnp.exp(b_last) * st + _dot_tn(v, k_s)
        on = o * lax.rsqrt(jnp.mean(o * o, axis=-1, keepdims=True) + EPS) * og
        o_ref[:, vs] = (on * sz_ref[:, vs].astype(F32)).astype(o_ref.dtype)


def _gla_call(proj, small, wa_pad, ba, og, B, S):
    T = proj.shape[0]
    nc = S // GLA_CHUNK
    L = GLA_CHUNK
    return pl.pallas_call(
        _gla_kernel,
        grid=(B, nc),
        in_specs=[pl.BlockSpec((L, PROJ_TN), lambda b, c: (b * nc + c, COL_QK_C)),
                  pl.BlockSpec((L, GLA_WIDTH), lambda b, c: (b * nc + c, COL_V_C)),
                  pl.BlockSpec((L, GLA_WIDTH), lambda b, c: (b * nc + c, COL_Z_C)),
                  pl.BlockSpec((L, SMALL_W), lambda b, c: (b * nc + c, 0)),
                  _resident((SMALL_W, GLA_HEADS * GLA_DK)),
                  _resident((1, GLA_HEADS * GLA_DK)),
                  _resident((1, GLA_DV))],
        out_specs=pl.BlockSpec((L, GLA_WIDTH), lambda b, c: (b * nc + c, 0)),
        out_shape=jax.ShapeDtypeStruct((T, GLA_WIDTH), BF16),
        scratch_shapes=[pltpu.VMEM((GLA_HEADS, GLA_DV, GLA_DK), F32)],
        compiler_params=_params("parallel", "arbitrary"),
    )(proj, proj, proj, small, wa_pad, ba, og)


def _merge_kernel(x_ref, ya_ref, yb_ref, yc_ref, ga_ref, gb_ref, gc_ref, wb_ref, wo_ref, o_ref):
    merged = ga_ref[...].astype(F32) * _dot(ya_ref[...], wb_ref[0])
    merged += gb_ref[...].astype(F32) * _dot(yb_ref[...], wb_ref[1])
    merged += gc_ref[...].astype(F32) * _dot(yc_ref[...], wb_ref[2])
    o_ref[...] = x_ref[...] + _dot(merged.astype(BF16), wo_ref[...])


def _merge_call(x2, ya, yb, yc, proj, wb, wo, tm):
    T, D = x2.shape
    gate_blk = COL_GATE * PROJ_TN // D
    yspec = pl.BlockSpec((tm, BRANCH_WIDTH), lambda i: (i, 0))
    return pl.pallas_call(
        _merge_kernel,
        grid=(T // tm,),
        in_specs=[pl.BlockSpec((tm, D), lambda i: (i, 0)),
                  yspec, yspec, yspec,
                  pl.BlockSpec((tm, D), lambda i: (i, gate_blk)),
                  pl.BlockSpec((tm, D), lambda i: (i, gate_blk + 1)),
                  pl.BlockSpec((tm, D), lambda i: (i, gate_blk + 2)),
                  _resident((N_BRANCH, BRANCH_WIDTH, D)),
                  _resident((D, D))],
        out_specs=pl.BlockSpec((tm, D), lambda i: (i, 0)),
        out_shape=jax.ShapeDtypeStruct((T, D), F32),
        compiler_params=_params("parallel"),
    )(x2, ya, yb, yc, proj, proj, proj, wb, wo)


def _split_w_in(w_in):
    sizes = (MLA_Q_RANK, MLA_KV_RANK, MLA_ROPE, MLA_WIDTH, SGU_WIDTH, SGU_WIDTH, SGU_WIDTH,
             GLA_HEADS * GLA_DK, GLA_HEADS * GLA_DK, GLA_WIDTH, GLA_GATE_RANK, GLA_WIDTH)
    parts, off = [], 0
    for s in sizes:
        parts.append(w_in[:, off:off + s])
        off += s
    parts.append(w_in[:, off:])
    (c_q, c_kv, k_rope, z_a, u_b, v_b, z_b, q_c, k_c, v_c, a_r, z_c, gates) = parts
    w_main = jnp.concatenate([c_q, c_kv, q_c, k_c, v_c, z_a, z_b, z_c, u_b, v_b, gates],
                             axis=1).astype(BF16)
    pad = jnp.zeros((w_in.shape[0], SMALL_W - MLA_ROPE - GLA_GATE_RANK), w_in.dtype)
    w_small = jnp.concatenate([k_rope, a_r, pad], axis=1).astype(BF16)
    return w_main, w_small


def _pad_heads(w, per_head, lo, hi, width):
    r = w.shape[0]
    w = w.reshape(r, MLA_HEADS, per_head)[:, :, lo:hi]
    w = jnp.pad(w, ((0, 0), (0, 0), (0, width - (hi - lo))))
    return w.reshape(r, MLA_HEADS * width)


def _rope_tables(positions):
    half = MLA_ROPE // 2
    inv_freq = 1.0 / (ROPE_THETA ** (jnp.arange(0, MLA_ROPE, 2, dtype=F32) / MLA_ROPE))
    ang = positions.astype(F32).reshape(-1, 1) * inv_freq
    cos, sin = jnp.cos(ang), jnp.sin(ang)
    z = jnp.zeros((ang.shape[0], LANE - 2 * half), F32)
    return (jnp.concatenate([cos, cos, z], axis=1), jnp.concatenate([-sin, sin, z], axis=1))


def _pick_tile(n, want):
    t = min(n, want)
    while n % t:
        t //= 2
    return t


def kernel(x, positions, norm_g, w_in, mla_cq_norm, mla_ckv_norm, mla_w_uq, mla_w_ukv,
           mla_q_norm, mla_k_norm, sgu_v_norm, sgu_w_s, sgu_b_s, gla_w_a2, gla_b_a,
           gla_o_norm, w_branch, w_out):
    B, S, D = x.shape
    T = B * S
    depth = w_in.shape[0]
    tm_proj = _pick_tile(T, 1024)
    tm_row = _pick_tile(T, 512)
    tm_merge = _pick_tile(T, 256)
    tq = _pick_tile(S, 512)

    cosf, sins = _rope_tables(positions)
    x2 = x.reshape(T, D)
    for l in range(depth):
        w_main, w_small = _split_w_in(w_in[l])
        wuq = _pad_heads(mla_w_uq[l], MLA_QK, 0, MLA_QK, MLA_QK_PAD).astype(BF16)
        wuk = _pad_heads(mla_w_ukv[l], MLA_NOPE + MLA_V, 0, MLA_NOPE, MLA_NOPE).astype(BF16)
        wuv = _pad_heads(mla_w_ukv[l], MLA_NOPE + MLA_V, MLA_NOPE, MLA_NOPE + MLA_V, MLA_V).astype(BF16)
        qg = jnp.pad(mla_q_norm[l], (0, MLA_QK_PAD - MLA_QK)).reshape(1, MLA_QK_PAD)
        kg = jnp.pad(mla_k_norm[l], (0, MLA_QK_PAD - MLA_QK)).reshape(1, MLA_QK_PAD)
        wa_pad = jnp.zeros((SMALL_W, GLA_HEADS * GLA_DK), F32).at[
            MLA_ROPE:MLA_ROPE + GLA_GATE_RANK].set(gla_w_a2[l])

        h, small = _norm_call(x2, norm_g[l].reshape(1, D), w_small, tm_row)
        proj = _inproj_call(h, w_main, tm_proj)
        q, k, v = _mla_prep_call(proj, small, cosf, sins,
                                 mla_cq_norm[l].reshape(1, -1), mla_ckv_norm[l].reshape(1, -1),
                                 wuq, wuk, wuv, qg, kg, tm_row)
        ya = _attn_call(q, k, v, proj, B, S, tq)
        yb = _sgu_call(proj, sgu_v_norm[l].reshape(1, -1), sgu_w_s[l], sgu_b_s[l].T, tm_row)
        yc = _gla_call(proj, small, wa_pad, gla_b_a[l].reshape(1, -1),
                       gla_o_norm[l].reshape(1, -1), B, S)
        x2 = _merge_call(x2, ya, yb, yc, proj, w_branch[l].astype(BF16), w_out[l].astype(BF16), tm_merge)
    return x2.reshape(B, S, D)
```

```python
import functools
import math

import jax
import jax.numpy as jnp
from jax import lax
from jax.experimental import pallas as pl
from jax.experimental.pallas import tpu as pltpu

F32 = jnp.float32
BF16 = jnp.bfloat16

MLA_HEADS = 8
MLA_NOPE = 128
MLA_ROPE = 64
MLA_QK = MLA_NOPE + MLA_ROPE
MLA_V = 128
MLA_Q_RANK = 512
MLA_KV_RANK = 512
MLA_WIDTH = MLA_HEADS * MLA_V
MLA_QK_PAD = 256
ROPE_THETA = 10000.0
SGU_GROUPS = 8
SGU_CHUNK = 128
SGU_WIDTH = 1024
GLA_HEADS = 4
GLA_DK = 128
GLA_DV = 256
GLA_GATE_RANK = 16
GLA_TAU = 16.0
GLA_CHUNK = 128
GLA_WIDTH = GLA_HEADS * GLA_DV
N_BRANCH = 3
BRANCH_WIDTH = 1024
EPS = 1e-6
LANE = 128
LOG2_E = math.log2(math.e)
MASK_VALUE = -1e30
SMALL_W = LANE

PROJ_TN = 1024
LIN_CQKV, LIN_QK_C, LIN_V_C = range(3)
SILU_Z_A, SILU_Z_B, SILU_Z_C = range(3)
GELU_U_B, GELU_V_B = range(2)

VMEM_LIMIT = 56 * 1024 * 1024


def _params(*sem):
    return pltpu.CompilerParams(dimension_semantics=sem, vmem_limit_bytes=VMEM_LIMIT)


def _dot(a, b):
    return jnp.dot(a, b, preferred_element_type=F32)


def _dot_nt(a, b):
    return lax.dot_general(a, b, (((1,), (1,)), ((), ())), preferred_element_type=F32)


def _dot_tn(a, b):
    return lax.dot_general(a, b, (((0,), (0,)), ((), ())), preferred_element_type=F32)


def _sigmoid(x):
    return 1.0 / (1.0 + jnp.exp(-x))


def _gelu_tanh(x):
    c = math.sqrt(2.0 / math.pi)
    return 0.5 * x * (1.0 + jnp.tanh(c * (x + 0.044715 * (x * x * x))))


def _silu(x):
    return x * _sigmoid(x)


def _identity(x):
    return x


def _resident(shape):
    nd = len(shape)
    return pl.BlockSpec(shape, lambda *_: (0,) * nd, pipeline_mode=pl.Buffered(1))


def _norm_kernel(x_ref, g_ref, ws_ref, h_ref, small_ref):
    x = x_ref[...]
    rstd = lax.rsqrt(jnp.mean(x * x, axis=-1, keepdims=True) + EPS)
    h = (x * rstd * g_ref[...]).astype(BF16)
    h_ref[...] = h
    small_ref[...] = _dot(h, ws_ref[...])


def _norm_call(x2, g, w_small, tm):
    T, D = x2.shape
    return pl.pallas_call(
        _norm_kernel,
        grid=(T // tm,),
        in_specs=[pl.BlockSpec((tm, D), lambda i: (i, 0)),
                  _resident((1, D)),
                  _resident((D, SMALL_W))],
        out_specs=[pl.BlockSpec((tm, D), lambda i: (i, 0)),
                   pl.BlockSpec((tm, SMALL_W), lambda i: (i, 0))],
        out_shape=[jax.ShapeDtypeStruct((T, D), BF16),
                   jax.ShapeDtypeStruct((T, SMALL_W), F32)],
        compiler_params=_params("parallel"),
    )(x2, g, w_small)


def _inproj_kernel(h_ref, w_ref, o_ref, *, act):
    o_ref[...] = act(_dot(h_ref[...], w_ref[...])).astype(o_ref.dtype)


def _inproj_call(h, w, act, tm):
    T, D = h.shape
    N = w.shape[1]
    return pl.pallas_call(
        functools.partial(_inproj_kernel, act=act),
        grid=(T // tm, N // PROJ_TN),
        in_specs=[pl.BlockSpec((tm, D), lambda i, j: (i, 0)),
                  pl.BlockSpec((D, PROJ_TN), lambda i, j: (0, j))],
        out_specs=pl.BlockSpec((tm, PROJ_TN), lambda i, j: (i, j)),
        out_shape=jax.ShapeDtypeStruct((T, N), BF16),
        compiler_params=_params("parallel", "arbitrary"),
    )(h, w)


def _rope128(r, cosf, sins):
    lane = lax.broadcasted_iota(jnp.int32, r.shape, 1)
    partner = jnp.where(lane < MLA_ROPE // 2,
                        pltpu.roll(r, LANE - MLA_ROPE // 2, axis=1),
                        pltpu.roll(r, MLA_ROPE // 2, axis=1))
    return r * cosf + partner * sins


def _mla_prep_kernel(cq_ref, ckv_ref, small_ref, cos_ref, sin_ref,
                     cqg_ref, ckvg_ref, wuq_ref, wuk_ref, wuv_ref, qg_ref, kg_ref,
                     q_ref, k_ref, v_ref):
    cosf = cos_ref[...]
    sins = sin_ref[...]
    scale = MLA_QK ** -0.5 * LOG2_E

    cq = cq_ref[...].astype(F32)
    nq = cq * lax.rsqrt(jnp.mean(cq * cq, axis=-1, keepdims=True) + EPS) * cqg_ref[...]
    q_all = _dot(nq.astype(BF16), wuq_ref[...])

    ckv = ckv_ref[...].astype(F32)
    nkv = ckv * lax.rsqrt(jnp.mean(ckv * ckv, axis=-1, keepdims=True) + EPS) * ckvg_ref[...]
    nkv = nkv.astype(BF16)
    k_all = _dot(nkv, wuk_ref[...])
    v_ref[...] = _dot(nkv, wuv_ref[...]).astype(v_ref.dtype)

    qg = qg_ref[...]
    kg = kg_ref[...]
    lane = lax.broadcasted_iota(jnp.int32, (1, LANE), 1)
    kr = jnp.where(lane < MLA_ROPE, small_ref[...], 0.0)
    kr_ss = jnp.sum(kr * kr, axis=-1, keepdims=True)
    kr_rot = _rope128(kr * kg[:, MLA_NOPE:], cosf, sins)

    for hd in range(MLA_HEADS):
        qb = q_all[:, hd * MLA_QK_PAD:(hd + 1) * MLA_QK_PAD]
        rq = lax.rsqrt(jnp.sum(qb * qb, axis=-1, keepdims=True) * (1.0 / MLA_QK) + EPS) * scale
        qn = qb * rq * qg
        q_ref[:, hd * MLA_QK_PAD:hd * MLA_QK_PAD + MLA_NOPE] = qn[:, :MLA_NOPE].astype(q_ref.dtype)
        q_ref[:, hd * MLA_QK_PAD + MLA_NOPE:(hd + 1) * MLA_QK_PAD] = _rope128(
            qn[:, MLA_NOPE:], cosf, sins).astype(q_ref.dtype)

        kb = k_all[:, hd * MLA_NOPE:(hd + 1) * MLA_NOPE]
        rk = lax.rsqrt((jnp.sum(kb * kb, axis=-1, keepdims=True) + kr_ss) * (1.0 / MLA_QK) + EPS)
        k_ref[:, hd * MLA_QK_PAD:hd * MLA_QK_PAD + MLA_NOPE] = (
            kb * rk * kg[:, :MLA_NOPE]).astype(k_ref.dtype)
        k_ref[:, hd * MLA_QK_PAD + MLA_NOPE:(hd + 1) * MLA_QK_PAD] = (kr_rot * rk).astype(k_ref.dtype)


def _mla_prep_call(proj, small, cosf, sins, cqg, ckvg, wuq, wuk, wuv, qg, kg, tm):
    T = proj.shape[0]
    HQ = MLA_HEADS * MLA_QK_PAD
    return pl.pallas_call(
        _mla_prep_kernel,
        grid=(T // tm,),
        in_specs=[pl.BlockSpec((tm, MLA_Q_RANK), lambda i: (i, 0)),
                  pl.BlockSpec((tm, MLA_KV_RANK), lambda i: (i, 1)),
                  pl.BlockSpec((tm, SMALL_W), lambda i: (i, 0)),
                  pl.BlockSpec((tm, LANE), lambda i: (i, 0)),
                  pl.BlockSpec((tm, LANE), lambda i: (i, 0)),
                  _resident((1, MLA_Q_RANK)),
                  _resident((1, MLA_KV_RANK)),
                  _resident((MLA_Q_RANK, HQ)),
                  _resident((MLA_KV_RANK, MLA_HEADS * MLA_NOPE)),
                  _resident((MLA_KV_RANK, MLA_WIDTH)),
                  _resident((1, MLA_QK_PAD)),
                  _resident((1, MLA_QK_PAD))],
        out_specs=[pl.BlockSpec((tm, HQ), lambda i: (i, 0)),
                   pl.BlockSpec((tm, HQ), lambda i: (i, 0)),
                   pl.BlockSpec((tm, MLA_WIDTH), lambda i: (i, 0))],
        out_shape=[jax.ShapeDtypeStruct((T, HQ), BF16),
                   jax.ShapeDtypeStruct((T, HQ), BF16),
                   jax.ShapeDtypeStruct((T, MLA_WIDTH), BF16)],
        compiler_params=_params("parallel"),
    )(proj, proj, small, cosf, sins, cqg, ckvg, wuq, wuk, wuv, qg, kg)


def _attn_kernel(q_ref, k_ref, v_ref, sz_ref, o_ref, s_ref, m_ref, l_ref, acc_ref, *, tq):
    qi = pl.program_id(2)
    q = q_ref[...]
    nl = tq // LANE

    def lane_fold(x, op):
        r = x[:, :LANE]
        for c in range(1, nl):
            r = op(r, x[:, c * LANE:(c + 1) * LANE])
        return r

    def scores(j):
        k = k_ref[pl.ds(pl.multiple_of(j * tq, tq), tq), :]
        return _dot_nt(q, k)

    row = lax.broadcasted_iota(jnp.int32, (tq, tq), 0)
    col = lax.broadcasted_iota(jnp.int32, (tq, tq), 1)
    s_diag = jnp.where(row >= col, scores(qi), MASK_VALUE)
    s_ref[qi] = s_diag
    m_ref[...] = lane_fold(s_diag, jnp.maximum)

    def max_body(j, c):
        s = scores(j)
        s_ref[j] = s
        m_ref[...] = jnp.maximum(m_ref[...], lane_fold(s, jnp.maximum))
        return c

    lax.fori_loop(0, qi, max_body, 0)
    m = jnp.max(m_ref[...], axis=-1, keepdims=True)

    l_ref[...] = jnp.zeros(l_ref.shape, F32)
    acc_ref[...] = jnp.zeros(acc_ref.shape, F32)

    def pv_body(j, c):
        p = jnp.exp2(s_ref[j] - m)
        l_ref[...] += lane_fold(p, jnp.add)
        v = v_ref[pl.ds(pl.multiple_of(j * tq, tq), tq), :]
        acc_ref[...] += _dot(p.astype(v.dtype), v)
        return c

    lax.fori_loop(0, qi + 1, pv_body, 0)
    l = jnp.sum(l_ref[...], axis=-1, keepdims=True)
    o_ref[...] = (acc_ref[...] / l * sz_ref[...].astype(F32)).astype(o_ref.dtype)


def _attn_call(q, k, v, p_silu, B, S, tq):
    T = q.shape[0]
    nq = S // tq
    sz_col0 = SILU_Z_A * PROJ_TN // MLA_V
    return pl.pallas_call(
        functools.partial(_attn_kernel, tq=tq),
        grid=(B, MLA_HEADS, nq),
        in_specs=[pl.BlockSpec((tq, MLA_QK_PAD), lambda b, h, i: (b * nq + i, h)),
                  pl.BlockSpec((S, MLA_QK_PAD), lambda b, h, i: (b, h)),
                  pl.BlockSpec((S, MLA_V), lambda b, h, i: (b, h)),
                  pl.BlockSpec((tq, MLA_V), lambda b, h, i: (b * nq + i, sz_col0 + h))],
        out_specs=pl.BlockSpec((tq, MLA_V), lambda b, h, i: (b * nq + i, h)),
        out_shape=jax.ShapeDtypeStruct((T, MLA_WIDTH), BF16),
        scratch_shapes=[pltpu.VMEM((nq, tq, tq), F32), pltpu.VMEM((tq, LANE), F32),
                        pltpu.VMEM((tq, LANE), F32), pltpu.VMEM((tq, MLA_V), F32)],
        compiler_params=_params("parallel", "parallel", "arbitrary"),
    )(q, k, v, p_silu)


def _sgu_kernel(u_ref, v_ref, sz_ref, vg_ref, ws_ref, bs_ref, o_ref, *, tm):
    v = v_ref[...].astype(F32)
    vn = (v * lax.rsqrt(jnp.mean(v * v, axis=-1, keepdims=True) + EPS) * vg_ref[...]).astype(BF16)
    row = lax.broadcasted_iota(jnp.int32, (SGU_CHUNK, SGU_CHUNK), 0)
    col = lax.broadcasted_iota(jnp.int32, (SGU_CHUNK, SGU_CHUNK), 1)
    gd = SGU_WIDTH // SGU_GROUPS
    for g in range(SGU_GROUPS):
        w = jnp.where(row >= col, ws_ref[g], 0.0).astype(BF16)
        bias = bs_ref[:, g:g + 1]
        for c in range(tm // SGU_CHUNK):
            rs = slice(c * SGU_CHUNK, (c + 1) * SGU_CHUNK)
            cs = slice(g * gd, (g + 1) * gd)
            mix = _dot(w, vn[rs, cs]) + bias
            o_ref[rs, cs] = (u_ref[rs, cs].astype(F32) * mix
                             * sz_ref[rs, cs].astype(F32)).astype(o_ref.dtype)


def _sgu_call(p_gelu, p_silu, vg, ws, bs_t, tm):
    T = p_gelu.shape[0]
    return pl.pallas_call(
        functools.partial(_sgu_kernel, tm=tm),
        grid=(T // tm,),
        in_specs=[pl.BlockSpec((tm, SGU_WIDTH), lambda i: (i, GELU_U_B)),
                  pl.BlockSpec((tm, SGU_WIDTH), lambda i: (i, GELU_V_B)),
                  pl.BlockSpec((tm, SGU_WIDTH), lambda i: (i, SILU_Z_B)),
                  _resident((1, SGU_WIDTH)),
                  _resident((SGU_GROUPS, SGU_CHUNK, SGU_CHUNK)),
                  _resident((SGU_CHUNK, SGU_GROUPS))],
        out_specs=pl.BlockSpec((tm, SGU_WIDTH), lambda i: (i, 0)),
        out_shape=jax.ShapeDtypeStruct((T, SGU_WIDTH), BF16),
        compiler_params=_params("parallel"),
    )(p_gelu, p_gelu, p_silu, vg, ws, bs_t)


def _gla_kernel(qk_ref, v_ref, sz_ref, small_ref, wa_ref, ba_ref, og_ref, o_ref, st_ref):
    L = GLA_CHUNK

    @pl.when(pl.program_id(1) == 0)
    def _():
        st_ref[...] = jnp.zeros(st_ref.shape, F32)

    xg = jnp.dot(small_ref[...], wa_ref[...], preferred_element_type=F32,
                 precision=lax.Precision.HIGHEST) + ba_ref[...]
    log_a = (jnp.minimum(xg, 0.0) - jnp.log(1.0 + jnp.exp(-jnp.abs(xg)))) * (1.0 / GLA_TAU)
    row = lax.broadcasted_iota(jnp.int32, (L, L), 0)
    col = lax.broadcasted_iota(jnp.int32, (L, L), 1)
    causal = row >= col
    bcum = jnp.dot(causal.astype(F32), log_a, preferred_element_type=F32,
                   precision=lax.Precision.HIGHEST)
    og = og_ref[...]

    for hd in range(GLA_HEADS):
        ks = slice(hd * GLA_DK, (hd + 1) * GLA_DK)
        vs = slice(hd * GLA_DV, (hd + 1) * GLA_DV)
        b = bcum[:, ks]
        b_mid = b[L // 2:L // 2 + 1, :]
        b_last = b[L - 1:L, :]
        q = qk_ref[:, ks].astype(F32) * (GLA_DK ** -0.5)
        k = qk_ref[:, GLA_HEADS * GLA_DK + hd * GLA_DK:GLA_HEADS * GLA_DK + (hd + 1) * GLA_DK].astype(F32)
        v = v_ref[:, vs]
        q_t = (q * jnp.exp(b - b_mid)).astype(BF16)
        k_t = (k * jnp.exp(b_mid - b)).astype(BF16)
        attn = jnp.where(causal, _dot_nt(q_t, k_t), 0.0).astype(BF16)
        st = st_ref[hd]
        o = _dot_nt((q * jnp.exp(b)).astype(BF16), st.astype(BF16)) + _dot(attn, v)
        k_s = (k * jnp.exp(b_last - b)).astype(BF16)
        st_ref[hd] = jnp.exp(b_last) * st + _dot_tn(v, k_s)
        on = o * lax.rsqrt(jnp.mean(o * o, axis=-1, keepdims=True) + EPS) * og
        o_ref[:, vs] = (on * sz_ref[:, vs].astype(F32)).astype(o_ref.dtype)


def _gla_call(p_lin, p_silu, small, wa_pad, ba, og, B, S):
    T = p_lin.shape[0]
    nc = S // GLA_CHUNK
    L = GLA_CHUNK
    return pl.pallas_call(
        _gla_kernel,
        grid=(B, nc),
        in_specs=[pl.BlockSpec((L, PROJ_TN), lambda b, c: (b * nc + c, LIN_QK_C)),
                  pl.BlockSpec((L, GLA_WIDTH), lambda b, c: (b * nc + c, LIN_V_C)),
                  pl.BlockSpec((L, GLA_WIDTH), lambda b, c: (b * nc + c, SILU_Z_C)),
                  pl.BlockSpec((L, SMALL_W), lambda b, c: (b * nc + c, 0)),
                  _resident((SMALL_W, GLA_HEADS * GLA_DK)),
                  _resident((1, GLA_HEADS * GLA_DK)),
                  _resident((1, GLA_DV))],
        out_specs=pl.BlockSpec((L, GLA_WIDTH), lambda b, c: (b * nc + c, 0)),
        out_shape=jax.ShapeDtypeStruct((T, GLA_WIDTH), BF16),
        scratch_shapes=[pltpu.VMEM((GLA_HEADS, GLA_DV, GLA_DK), F32)],
        compiler_params=_params("parallel", "arbitrary"),
    )(p_lin, p_lin, p_silu, small, wa_pad, ba, og)


def _merge_kernel(x_ref, ya_ref, yb_ref, yc_ref, ga_ref, gb_ref, gc_ref, wb_ref, wo_ref, o_ref):
    merged = ga_ref[...].astype(F32) * _dot(ya_ref[...], wb_ref[0])
    merged += gb_ref[...].astype(F32) * _dot(yb_ref[...], wb_ref[1])
    merged += gc_ref[...].astype(F32) * _dot(yc_ref[...], wb_ref[2])
    o_ref[...] = x_ref[...] + _dot(merged.astype(BF16), wo_ref[...])


def _merge_call(x2, ya, yb, yc, p_gate, wb, wo, tm):
    T, D = x2.shape
    yspec = pl.BlockSpec((tm, BRANCH_WIDTH), lambda i: (i, 0))
    return pl.pallas_call(
        _merge_kernel,
        grid=(T // tm,),
        in_specs=[pl.BlockSpec((tm, D), lambda i: (i, 0)),
                  yspec, yspec, yspec,
                  pl.BlockSpec((tm, D), lambda i: (i, 0)),
                  pl.BlockSpec((tm, D), lambda i: (i, 1)),
                  pl.BlockSpec((tm, D), lambda i: (i, 2)),
                  _resident((N_BRANCH, BRANCH_WIDTH, D)),
                  _resident((D, D))],
        out_specs=pl.BlockSpec((tm, D), lambda i: (i, 0)),
        out_shape=jax.ShapeDtypeStruct((T, D), F32),
        compiler_params=_params("parallel"),
    )(x2, ya, yb, yc, p_gate, p_gate, p_gate, wb, wo)


def _split_w_in(w_in):
    sizes = (MLA_Q_RANK, MLA_KV_RANK, MLA_ROPE, MLA_WIDTH, SGU_WIDTH, SGU_WIDTH, SGU_WIDTH,
             GLA_HEADS * GLA_DK, GLA_HEADS * GLA_DK, GLA_WIDTH, GLA_GATE_RANK, GLA_WIDTH)
    parts, off = [], 0
    for s in sizes:
        parts.append(w_in[:, off:off + s])
        off += s
    parts.append(w_in[:, off:])
    (c_q, c_kv, k_rope, z_a, u_b, v_b, z_b, q_c, k_c, v_c, a_r, z_c, gates) = parts
    w_lin = jnp.concatenate([c_q, c_kv, q_c, k_c, v_c], axis=1).astype(BF16)
    w_silu = jnp.concatenate([z_a, z_b, z_c], axis=1).astype(BF16)
    w_gelu = jnp.concatenate([u_b, v_b], axis=1).astype(BF16)
    w_gate = gates.astype(BF16)
    pad = jnp.zeros((w_in.shape[0], SMALL_W - MLA_ROPE - GLA_GATE_RANK), w_in.dtype)
    w_small = jnp.concatenate([k_rope, a_r, pad], axis=1).astype(BF16)
    return w_lin, w_silu, w_gelu, w_gate, w_small


def _pad_heads(w, per_head, lo, hi, width):
    r = w.shape[0]
    w = w.reshape(r, MLA_HEADS, per_head)[:, :, lo:hi]
    w = jnp.pad(w, ((0, 0), (0, 0), (0, width - (hi - lo))))
    return w.reshape(r, MLA_HEADS * width)


def _rope_tables(positions):
    half = MLA_ROPE // 2
    inv_freq = 1.0 / (ROPE_THETA ** (jnp.arange(0, MLA_ROPE, 2, dtype=F32) / MLA_ROPE))
    ang = positions.astype(F32).reshape(-1, 1) * inv_freq
    cos, sin = jnp.cos(ang), jnp.sin(ang)
    z = jnp.zeros((ang.shape[0], LANE - 2 * half), F32)
    return (jnp.concatenate([cos, cos, z], axis=1), jnp.concatenate([-sin, sin, z], axis=1))


def _pick_tile(n, want):
    t = min(n, want)
    while n % t:
        t //= 2
    return t


def kernel(x, positions, norm_g, w_in, mla_cq_norm, mla_ckv_norm, mla_w_uq, mla_w_ukv,
           mla_q_norm, mla_k_norm, sgu_v_norm, sgu_w_s, sgu_b_s, gla_w_a2, gla_b_a,
           gla_o_norm, w_branch, w_out):
    B, S, D = x.shape
    T = B * S
    depth = w_in.shape[0]
    tm_proj = _pick_tile(T, 1024)
    tm_row = _pick_tile(T, 512)
    tm_merge = _pick_tile(T, 256)
    tq = _pick_tile(S, 512)

    cosf, sins = _rope_tables(positions)
    x2 = x.reshape(T, D)
    for l in range(depth):
        w_lin, w_silu, w_gelu, w_gate, w_small = _split_w_in(w_in[l])
        wuq = _pad_heads(mla_w_uq[l], MLA_QK, 0, MLA_QK, MLA_QK_PAD).astype(BF16)
        wuk = _pad_heads(mla_w_ukv[l], MLA_NOPE + MLA_V, 0, MLA_NOPE, MLA_NOPE).astype(BF16)
        wuv = _pad_heads(mla_w_ukv[l], MLA_NOPE + MLA_V, MLA_NOPE, MLA_NOPE + MLA_V, MLA_V).astype(BF16)
        qg = jnp.pad(mla_q_norm[l], (0, MLA_QK_PAD - MLA_QK)).reshape(1, MLA_QK_PAD)
        kg = jnp.pad(mla_k_norm[l], (0, MLA_QK_PAD - MLA_QK)).reshape(1, MLA_QK_PAD)
        wa_pad = jnp.zeros((SMALL_W, GLA_HEADS * GLA_DK), F32).at[
            MLA_ROPE:MLA_ROPE + GLA_GATE_RANK].set(gla_w_a2[l])

        h, small = _norm_call(x2, norm_g[l].reshape(1, D), w_small, tm_row)
        p_lin = _inproj_call(h, w_lin, _identity, tm_proj)
        p_silu = _inproj_call(h, w_silu, _silu, tm_proj)
        p_gelu = _inproj_call(h, w_gelu, _gelu_tanh, tm_proj)
        p_gate = _inproj_call(h, w_gate, _sigmoid, tm_proj)
        q, k, v = _mla_prep_call(p_lin, small, cosf, sins,
                                 mla_cq_norm[l].reshape(1, -1), mla_ckv_norm[l].reshape(1, -1),
                                 wuq, wuk, wuv, qg, kg, tm_row)
        ya = _attn_call(q, k, v, p_silu, B, S, tq)
        yb = _sgu_call(p_gelu, p_silu, sgu_v_norm[l].reshape(1, -1), sgu_w_s[l], sgu_b_s[l].T, tm_row)
        yc = _gla_call(p_lin, p_silu, small, wa_pad, gla_b_a[l].reshape(1, -1),
                       gla_o_norm[l].reshape(1, -1), B, S)
        x2 = _merge_call(x2, ya, yb, yc, p_gate, w_branch[l].astype(BF16), w_out[l].astype(BF16), tm_merge)
    return x2.reshape(B, S, D)
```

```python
import functools
import math

import jax
import jax.numpy as jnp
from jax import lax
from jax.experimental import pallas as pl
from jax.experimental.pallas import tpu as pltpu

F32 = jnp.float32
BF16 = jnp.bfloat16

MLA_HEADS = 8
MLA_NOPE = 128
MLA_ROPE = 64
MLA_QK = MLA_NOPE + MLA_ROPE
MLA_V = 128
MLA_Q_RANK = 512
MLA_KV_RANK = 512
MLA_WIDTH = MLA_HEADS * MLA_V
MLA_QK_PAD = 256
ROPE_THETA = 10000.0
SGU_GROUPS = 8
SGU_CHUNK = 128
SGU_WIDTH = 1024
GLA_HEADS = 4
GLA_DK = 128
GLA_DV = 256
GLA_GATE_RANK = 16
GLA_TAU = 16.0
GLA_CHUNK = 128
GLA_WIDTH = GLA_HEADS * GLA_DV
N_BRANCH = 3
BRANCH_WIDTH = 1024
EPS = 1e-6
LANE = 128
LOG2_E = math.log2(math.e)
MASK_VALUE = -1e30
SMALL_W = LANE

PROJ_TN = 1024
LIN_CQKV, LIN_QK_C, LIN_V_C = range(3)
SILU_Z_A, SILU_Z_B, SILU_Z_C = range(3)
GELU_U_B, GELU_V_B = range(2)

VMEM_LIMIT = 56 * 1024 * 1024


def _params(*sem):
    return pltpu.CompilerParams(dimension_semantics=sem, vmem_limit_bytes=VMEM_LIMIT)


def _dot(a, b):
    return jnp.dot(a, b, preferred_element_type=F32)


def _dot_nt(a, b):
    return lax.dot_general(a, b, (((1,), (1,)), ((), ())), preferred_element_type=F32)


def _dot_tn(a, b):
    return lax.dot_general(a, b, (((0,), (0,)), ((), ())), preferred_element_type=F32)


def _sigmoid(x):
    return 1.0 / (1.0 + jnp.exp(-x))


def _gelu_tanh(x):
    c = math.sqrt(2.0 / math.pi)
    return 0.5 * x * (1.0 + jnp.tanh(c * (x + 0.044715 * (x * x * x))))


def _silu(x):
    return x * _sigmoid(x)


def _identity(x):
    return x


def _resident(shape):
    nd = len(shape)
    return pl.BlockSpec(shape, lambda *_: (0,) * nd, pipeline_mode=pl.Buffered(1))


def _norm_kernel(x_ref, g_ref, ws_ref, h_ref, small_ref):
    x = x_ref[...]
    rstd = lax.rsqrt(jnp.mean(x * x, axis=-1, keepdims=True) + EPS)
    h = (x * rstd * g_ref[...]).astype(BF16)
    h_ref[...] = h
    small_ref[...] = _dot(h, ws_ref[...])


def _norm_call(x2, g, w_small, tm):
    T, D = x2.shape
    return pl.pallas_call(
        _norm_kernel,
        grid=(T // tm,),
        in_specs=[pl.BlockSpec((tm, D), lambda i: (i, 0)),
                  _resident((1, D)),
                  _resident((D, SMALL_W))],
        out_specs=[pl.BlockSpec((tm, D), lambda i: (i, 0)),
                   pl.BlockSpec((tm, SMALL_W), lambda i: (i, 0))],
        out_shape=[jax.ShapeDtypeStruct((T, D), BF16),
                   jax.ShapeDtypeStruct((T, SMALL_W), F32)],
        compiler_params=_params("parallel"),
    )(x2, g, w_small)


def _inproj_kernel(h_ref, w_ref, o_ref, *, act):
    o_ref[...] = act(_dot(h_ref[...], w_ref[...])).astype(o_ref.dtype)


def _inproj_call(h, w, act, tm):
    T, D = h.shape
    N = w.shape[1]
    return pl.pallas_call(
        functools.partial(_inproj_kernel, act=act),
        grid=(T // tm, N // PROJ_TN),
        in_specs=[pl.BlockSpec((tm, D), lambda i, j: (i, 0)),
                  pl.BlockSpec((D, PROJ_TN), lambda i, j: (0, j))],
        out_specs=pl.BlockSpec((tm, PROJ_TN), lambda i, j: (i, j)),
        out_shape=jax.ShapeDtypeStruct((T, N), BF16),
        compiler_params=_params("parallel", "arbitrary"),
    )(h, w)


def _rope128(r, cosf, sins):
    lane = lax.broadcasted_iota(jnp.int32, r.shape, 1)
    partner = jnp.where(lane < MLA_ROPE // 2,
                        pltpu.roll(r, LANE - MLA_ROPE // 2, axis=1),
                        pltpu.roll(r, MLA_ROPE // 2, axis=1))
    return r * cosf + partner * sins


def _mla_prep_kernel(cq_ref, ckv_ref, small_ref, cos_ref, sin_ref,
                     cqg_ref, ckvg_ref, wuq_ref, wuk_ref, wuv_ref, qg_ref, kg_ref,
                     q_ref, k_ref, v_ref):
    cosf = cos_ref[...]
    sins = sin_ref[...]
    scale = MLA_QK ** -0.5 * LOG2_E

    cq = cq_ref[...].astype(F32)
    nq = cq * lax.rsqrt(jnp.mean(cq * cq, axis=-1, keepdims=True) + EPS) * cqg_ref[...]
    q_all = _dot(nq.astype(BF16), wuq_ref[...])

    ckv = ckv_ref[...].astype(F32)
    nkv = ckv * lax.rsqrt(jnp.mean(ckv * ckv, axis=-1, keepdims=True) + EPS) * ckvg_ref[...]
    nkv = nkv.astype(BF16)
    k_all = _dot(nkv, wuk_ref[...])
    v_ref[...] = _dot(nkv, wuv_ref[...]).astype(v_ref.dtype)

    qg = qg_ref[...]
    kg = kg_ref[...]
    lane = lax.broadcasted_iota(jnp.int32, (1, LANE), 1)
    kr = jnp.where(lane < MLA_ROPE, small_ref[...], 0.0)
    kr_ss = jnp.sum(kr * kr, axis=-1, keepdims=True)
    kr_rot = _rope128(kr * kg[:, MLA_NOPE:], cosf, sins)

    for hd in range(MLA_HEADS):
        qb = q_all[:, hd * MLA_QK_PAD:(hd + 1) * MLA_QK_PAD]
        rq = lax.rsqrt(jnp.sum(qb * qb, axis=-1, keepdims=True) * (1.0 / MLA_QK) + EPS) * scale
        qn = qb * rq * qg
        q_ref[:, hd * MLA_QK_PAD:hd * MLA_QK_PAD + MLA_NOPE] = qn[:, :MLA_NOPE].astype(q_ref.dtype)
        q_ref[:, hd * MLA_QK_PAD + MLA_NOPE:(hd + 1) * MLA_QK_PAD] = _rope128(
            qn[:, MLA_NOPE:], cosf, sins).astype(q_ref.dtype)

        kb = k_all[:, hd * MLA_NOPE:(hd + 1) * MLA_NOPE]
        rk = lax.rsqrt((jnp.sum(kb * kb, axis=-1, keepdims=True) + kr_ss) * (1.0 / MLA_QK) + EPS)
        k_ref[:, hd * MLA_QK_PAD:hd * MLA_QK_PAD + MLA_NOPE] = (
            kb * rk * kg[:, :MLA_NOPE]).astype(k_ref.dtype)
        k_ref[:, hd * MLA_QK_PAD + MLA_NOPE:(hd + 1) * MLA_QK_PAD] = (kr_rot * rk).astype(k_ref.dtype)


def _mla_prep_call(proj, small, cosf, sins, cqg, ckvg, wuq, wuk, wuv, qg, kg, tm):
    T = proj.shape[0]
    HQ = MLA_HEADS * MLA_QK_PAD
    return pl.pallas_call(
        _mla_prep_kernel,
        grid=(T // tm,),
        in_specs=[pl.BlockSpec((tm, MLA_Q_RANK), lambda i: (i, 0)),
                  pl.BlockSpec((tm, MLA_KV_RANK), lambda i: (i, 1)),
                  pl.BlockSpec((tm, SMALL_W), lambda i: (i, 0)),
                  pl.BlockSpec((tm, LANE), lambda i: (i, 0)),
                  pl.BlockSpec((tm, LANE), lambda i: (i, 0)),
                  _resident((1, MLA_Q_RANK)),
                  _resident((1, MLA_KV_RANK)),
                  _resident((MLA_Q_RANK, HQ)),
                  _resident((MLA_KV_RANK, MLA_HEADS * MLA_NOPE)),
                  _resident((MLA_KV_RANK, MLA_WIDTH)),
                  _resident((1, MLA_QK_PAD)),
                  _resident((1, MLA_QK_PAD))],
        out_specs=[pl.BlockSpec((tm, HQ), lambda i: (i, 0)),
                   pl.BlockSpec((tm, HQ), lambda i: (i, 0)),
                   pl.BlockSpec((tm, MLA_WIDTH), lambda i: (i, 0))],
        out_shape=[jax.ShapeDtypeStruct((T, HQ), BF16),
                   jax.ShapeDtypeStruct((T, HQ), BF16),
                   jax.ShapeDtypeStruct((T, MLA_WIDTH), BF16)],
        compiler_params=_params("parallel"),
    )(proj, proj, small, cosf, sins, cqg, ckvg, wuq, wuk, wuv, qg, kg)


def _attn_kernel(qlo_ref, qhi_ref, k_ref, v_ref, szlo_ref, szhi_ref, o_ref,
                 q2_ref, s_ref, mf_ref, lf_ref, pv_ref, *, tq, nq, n_steps):
    g = pl.program_id(0)
    half = nq // 2
    nt = nq + 1
    nl = tq // LANE
    i1 = jnp.minimum(g, n_steps - 1) % half
    i2 = jnp.maximum(g - 1, 0) % half

    @pl.when(g == 0)
    def _():
        for t in range(nt):
            s_ref[t] = jnp.zeros((tq, tq), F32)
            mf_ref[t] = jnp.zeros((tq, LANE), F32)

    q2_ref[0] = qlo_ref[...]
    q2_ref[1] = qhi_ref[...]

    def lane_fold(x, op):
        r = x[:, :LANE]
        for c in range(1, nl):
            r = op(r, x[:, c * LANE:(c + 1) * LANE])
        return r

    def is_lo(p, i):
        return True if p == 0 else (False if p >= half else p <= i)

    def pick(p, i, lo, hi):
        c = is_lo(p, i)
        if c is True:
            return lo
        if c is False:
            return hi
        return jnp.where(c, lo, hi)

    def kv_rows(p, i):
        if p == 0:
            j = i
        elif p == nq:
            j = nq - 1 - i
        else:
            j = pick(p, i, p - 1, p - i - 1)
        return pl.ds(pl.multiple_of(j * tq, tq), tq)

    def combine(ref, i, op, init):
        lo = jnp.full((tq, LANE), init, F32)
        hi = jnp.full((tq, LANE), init, F32)
        for p in range(nt):
            c = is_lo(p, i)
            x = ref[p]
            if c is True:
                lo = op(lo, x)
            elif c is False:
                hi = op(hi, x)
            else:
                lo = op(lo, jnp.where(c, x, init))
                hi = op(hi, jnp.where(c, init, x))
        return lo, hi

    m_lo, m_hi = combine(mf_ref, i2, jnp.maximum, MASK_VALUE)
    m_lo = jnp.max(m_lo, axis=-1, keepdims=True)
    m_hi = jnp.max(m_hi, axis=-1, keepdims=True)

    causal = (lax.broadcasted_iota(jnp.int32, (tq, tq), 0)
              >= lax.broadcasted_iota(jnp.int32, (tq, tq), 1))
    for p in range(nt):
        e = jnp.exp2(s_ref[p] - pick(p, i2, m_lo, m_hi))
        lf_ref[p] = lane_fold(e, jnp.add)
        pv_ref[p] = _dot(e.astype(BF16), v_ref[kv_rows(p, i2), :])
        s = _dot_nt(q2_ref[pick(p, i1, 0, 1)], k_ref[kv_rows(p, i1), :])
        if p == 0 or p == nq:
            s = jnp.where(causal, s, MASK_VALUE)
        s_ref[p] = s
        mf_ref[p] = lane_fold(s, jnp.maximum)

    l_lo, l_hi = combine(lf_ref, i2, jnp.add, 0.0)
    a_lo, a_hi = combine(pv_ref, i2, jnp.add, 0.0)
    l_lo = jnp.sum(l_lo, axis=-1, keepdims=True)
    l_hi = jnp.sum(l_hi, axis=-1, keepdims=True)
    o_ref[:tq, :] = (a_lo / l_lo * szlo_ref[...].astype(F32)).astype(o_ref.dtype)
    o_ref[tq:, :] = (a_hi / l_hi * szhi_ref[...].astype(F32)).astype(o_ref.dtype)


def _attn_call(q, k, v, p_silu, B, S, tq):
    T = q.shape[0]
    nq = S // tq
    assert nq % 2 == 0 and MLA_V == LANE
    half = nq // 2
    n_steps = B * MLA_HEADS * half
    sz_col0 = SILU_Z_A * PROJ_TN // MLA_V

    def bhi(g):
        bh, i = g // half, g % half
        return bh // MLA_HEADS, bh % MLA_HEADS, i

    def first(g):
        return bhi(jnp.minimum(g, n_steps - 1))

    def second(g):
        return bhi(jnp.maximum(g - 1, 0))

    def lo_blk(f, col0=0):
        def index(g):
            b, h, i = f(g)
            return b * nq + i, col0 + h
        return index

    def hi_blk(f, col0=0):
        def index(g):
            b, h, i = f(g)
            return b * nq + nq - 1 - i, col0 + h
        return index

    def seq_blk(f):
        def index(g):
            b, h, _ = f(g)
            return b, h
        return index

    def out_blk(g):
        b, h, i = second(g)
        return b * half + i, h

    nt = nq + 1
    return pl.pallas_call(
        functools.partial(_attn_kernel, tq=tq, nq=nq, n_steps=n_steps),
        grid=(n_steps + 1,),
        in_specs=[pl.BlockSpec((tq, MLA_QK_PAD), lo_blk(first)),
                  pl.BlockSpec((tq, MLA_QK_PAD), hi_blk(first)),
                  pl.BlockSpec((S, MLA_QK_PAD), seq_blk(first)),
                  pl.BlockSpec((S, MLA_V), seq_blk(second)),
                  pl.BlockSpec((tq, MLA_V), lo_blk(second, sz_col0)),
                  pl.BlockSpec((tq, MLA_V), hi_blk(second, sz_col0))],
        out_specs=pl.BlockSpec((2 * tq, MLA_V), out_blk),
        out_shape=jax.ShapeDtypeStruct((T, MLA_WIDTH), BF16),
        scratch_shapes=[pltpu.VMEM((2, tq, MLA_QK_PAD), BF16),
                        pltpu.VMEM((nt, tq, tq), F32),
                        pltpu.VMEM((nt, tq, LANE), F32),
                        pltpu.VMEM((nt, tq, LANE), F32),
                        pltpu.VMEM((nt, tq, MLA_V), F32)],
        compiler_params=_params("arbitrary"),
    )(q, q, k, v, p_silu, p_silu)


def _paired_block(r, S, tq, tm):
    per_seq, per_tile, nq = S // tm, tq // tm, S // tq
    b, w = r // per_seq, r % per_seq
    u, sub = w // per_tile, w % per_tile
    pos = jnp.where(u < nq // 2, 2 * u, 2 * (nq - 1 - u) + 1)
    return b * per_seq + pos * per_tile + sub


def _sgu_kernel(u_ref, v_ref, sz_ref, vg_ref, ws_ref, bs_ref, o_ref, *, tm):
    v = v_ref[...].astype(F32)
    vn = (v * lax.rsqrt(jnp.mean(v * v, axis=-1, keepdims=True) + EPS) * vg_ref[...]).astype(BF16)
    row = lax.broadcasted_iota(jnp.int32, (SGU_CHUNK, SGU_CHUNK), 0)
    col = lax.broadcasted_iota(jnp.int32, (SGU_CHUNK, SGU_CHUNK), 1)
    gd = SGU_WIDTH // SGU_GROUPS
    for g in range(SGU_GROUPS):
        w = jnp.where(row >= col, ws_ref[g], 0.0).astype(BF16)
        bias = bs_ref[:, g:g + 1]
        for c in range(tm // SGU_CHUNK):
            rs = slice(c * SGU_CHUNK, (c + 1) * SGU_CHUNK)
            cs = slice(g * gd, (g + 1) * gd)
            mix = _dot(w, vn[rs, cs]) + bias
            o_ref[rs, cs] = (u_ref[rs, cs].astype(F32) * mix
                             * sz_ref[rs, cs].astype(F32)).astype(o_ref.dtype)


def _sgu_call(p_gelu, p_silu, vg, ws, bs_t, tm):
    T = p_gelu.shape[0]
    return pl.pallas_call(
        functools.partial(_sgu_kernel, tm=tm),
        grid=(T // tm,),
        in_specs=[pl.BlockSpec((tm, SGU_WIDTH), lambda i: (i, GELU_U_B)),
                  pl.BlockSpec((tm, SGU_WIDTH), lambda i: (i, GELU_V_B)),
                  pl.BlockSpec((tm, SGU_WIDTH), lambda i: (i, SILU_Z_B)),
                  _resident((1, SGU_WIDTH)),
                  _resident((SGU_GROUPS, SGU_CHUNK, SGU_CHUNK)),
                  _resident((SGU_CHUNK, SGU_GROUPS))],
        out_specs=pl.BlockSpec((tm, SGU_WIDTH), lambda i: (i, 0)),
        out_shape=jax.ShapeDtypeStruct((T, SGU_WIDTH), BF16),
        compiler_params=_params("parallel"),
    )(p_gelu, p_gelu, p_silu, vg, ws, bs_t)


def _gla_kernel(qk_ref, v_ref, sz_ref, small_ref, wa_ref, ba_ref, og_ref, o_ref, st_ref):
    L = GLA_CHUNK

    @pl.when(pl.program_id(1) == 0)
    def _():
        st_ref[...] = jnp.zeros(st_ref.shape, F32)

    xg = jnp.dot(small_ref[...], wa_ref[...], preferred_element_type=F32,
                 precision=lax.Precision.HIGHEST) + ba_ref[...]
    log_a = (jnp.minimum(xg, 0.0) - jnp.log(1.0 + jnp.exp(-jnp.abs(xg)))) * (1.0 / GLA_TAU)
    row = lax.broadcasted_iota(jnp.int32, (L, L), 0)
    col = lax.broadcasted_iota(jnp.int32, (L, L), 1)
    causal = row >= col
    bcum = jnp.dot(causal.astype(F32), log_a, preferred_element_type=F32,
                   precision=lax.Precision.HIGHEST)
    og = og_ref[...]

    for hd in range(GLA_HEADS):
        ks = slice(hd * GLA_DK, (hd + 1) * GLA_DK)
        vs = slice(hd * GLA_DV, (hd + 1) * GLA_DV)
        b = bcum[:, ks]
        b_mid = b[L // 2:L // 2 + 1, :]
        b_last = b[L - 1:L, :]
        q = qk_ref[:, ks].astype(F32) * (GLA_DK ** -0.5)
        k = qk_ref[:, GLA_HEADS * GLA_DK + hd * GLA_DK:GLA_HEADS * GLA_DK + (hd + 1) * GLA_DK].astype(F32)
        v = v_ref[:, vs]
        q_t = (q * jnp.exp(b - b_mid)).astype(BF16)
        k_t = (k * jnp.exp(b_mid - b)).astype(BF16)
        attn = jnp.where(causal, _dot_nt(q_t, k_t), 0.0).astype(BF16)
        st = st_ref[hd]
        o = _dot_nt((q * jnp.exp(b)).astype(BF16), st.astype(BF16)) + _dot(attn, v)
        k_s = (k * jnp.exp(b_last - b)).astype(BF16)
        st_ref[hd] = jnp.exp(b_last) * st + _dot_tn(v, k_s)
        on = o * lax.rsqrt(jnp.mean(o * o, axis=-1, keepdims=True) + EPS) * og
        o_ref[:, vs] = (on * sz_ref[:, vs].astype(F32)).astype(o_ref.dtype)


def _gla_call(p_lin, p_silu, small, wa_pad, ba, og, B, S):
    T = p_lin.shape[0]
    nc = S // GLA_CHUNK
    L = GLA_CHUNK
    return pl.pallas_call(
        _gla_kernel,
        grid=(B, nc),
        in_specs=[pl.BlockSpec((L, PROJ_TN), lambda b, c: (b * nc + c, LIN_QK_C)),
                  pl.BlockSpec((L, GLA_WIDTH), lambda b, c: (b * nc + c, LIN_V_C)),
                  pl.BlockSpec((L, GLA_WIDTH), lambda b, c: (b * nc + c, SILU_Z_C)),
                  pl.BlockSpec((L, SMALL_W), lambda b, c: (b * nc + c, 0)),
                  _resident((SMALL_W, GLA_HEADS * GLA_DK)),
                  _resident((1, GLA_HEADS * GLA_DK)),
                  _resident((1, GLA_DV))],
        out_specs=pl.BlockSpec((L, GLA_WIDTH), lambda b, c: (b * nc + c, 0)),
        out_shape=jax.ShapeDtypeStruct((T, GLA_WIDTH), BF16),
        scratch_shapes=[pltpu.VMEM((GLA_HEADS, GLA_DV, GLA_DK), F32)],
        compiler_params=_params("parallel", "arbitrary"),
    )(p_lin, p_lin, p_silu, small, wa_pad, ba, og)


def _merge_kernel(x_ref, ya_ref, yb_ref, yc_ref, ga_ref, gb_ref, gc_ref, wb_ref, wo_ref, o_ref):
    merged = ga_ref[...].astype(F32) * _dot(ya_ref[...], wb_ref[0])
    merged += gb_ref[...].astype(F32) * _dot(yb_ref[...], wb_ref[1])
    merged += gc_ref[...].astype(F32) * _dot(yc_ref[...], wb_ref[2])
    o_ref[...] = x_ref[...] + _dot(merged.astype(BF16), wo_ref[...])


def _merge_call(x2, ya, yb, yc, p_gate, wb, wo, tm, S, tq):
    T, D = x2.shape
    yspec = pl.BlockSpec((tm, BRANCH_WIDTH), lambda i: (i, 0))
    ya_spec = pl.BlockSpec((tm, BRANCH_WIDTH), lambda i: (_paired_block(i, S, tq, tm), 0))
    return pl.pallas_call(
        _merge_kernel,
        grid=(T // tm,),
        in_specs=[pl.BlockSpec((tm, D), lambda i: (i, 0)),
                  ya_spec, yspec, yspec,
                  pl.BlockSpec((tm, D), lambda i: (i, 0)),
                  pl.BlockSpec((tm, D), lambda i: (i, 1)),
                  pl.BlockSpec((tm, D), lambda i: (i, 2)),
                  _resident((N_BRANCH, BRANCH_WIDTH, D)),
                  _resident((D, D))],
        out_specs=pl.BlockSpec((tm, D), lambda i: (i, 0)),
        out_shape=jax.ShapeDtypeStruct((T, D), F32),
        compiler_params=_params("parallel"),
    )(x2, ya, yb, yc, p_gate, p_gate, p_gate, wb, wo)


def _split_w_in(w_in):
    sizes = (MLA_Q_RANK, MLA_KV_RANK, MLA_ROPE, MLA_WIDTH, SGU_WIDTH, SGU_WIDTH, SGU_WIDTH,
             GLA_HEADS * GLA_DK, GLA_HEADS * GLA_DK, GLA_WIDTH, GLA_GATE_RANK, GLA_WIDTH)
    parts, off = [], 0
    for s in sizes:
        parts.append(w_in[:, off:off + s])
        off += s
    parts.append(w_in[:, off:])
    (c_q, c_kv, k_rope, z_a, u_b, v_b, z_b, q_c, k_c, v_c, a_r, z_c, gates) = parts
    w_lin = jnp.concatenate([c_q, c_kv, q_c, k_c, v_c], axis=1).astype(BF16)
    w_silu = jnp.concatenate([z_a, z_b, z_c], axis=1).astype(BF16)
    w_gelu = jnp.concatenate([u_b, v_b], axis=1).astype(BF16)
    w_gate = gates.astype(BF16)
    pad = jnp.zeros((w_in.shape[0], SMALL_W - MLA_ROPE - GLA_GATE_RANK), w_in.dtype)
    w_small = jnp.concatenate([k_rope, a_r, pad], axis=1).astype(BF16)
    return w_lin, w_silu, w_gelu, w_gate, w_small


def _pad_heads(w, per_head, lo, hi, width):
    r = w.shape[0]
    w = w.reshape(r, MLA_HEADS, per_head)[:, :, lo:hi]
    w = jnp.pad(w, ((0, 0), (0, 0), (0, width - (hi - lo))))
    return w.reshape(r, MLA_HEADS * width)


def _rope_tables(positions):
    half = MLA_ROPE // 2
    inv_freq = 1.0 / (ROPE_THETA ** (jnp.arange(0, MLA_ROPE, 2, dtype=F32) / MLA_ROPE))
    ang = positions.astype(F32).reshape(-1, 1) * inv_freq
    cos, sin = jnp.cos(ang), jnp.sin(ang)
    z = jnp.zeros((ang.shape[0], LANE - 2 * half), F32)
    return (jnp.concatenate([cos, cos, z], axis=1), jnp.concatenate([-sin, sin, z], axis=1))


def _pick_tile(n, want):
    t = min(n, want)
    while n % t:
        t //= 2
    return t


def kernel(x, positions, norm_g, w_in, mla_cq_norm, mla_ckv_norm, mla_w_uq, mla_w_ukv,
           mla_q_norm, mla_k_norm, sgu_v_norm, sgu_w_s, sgu_b_s, gla_w_a2, gla_b_a,
           gla_o_norm, w_branch, w_out):
    B, S, D = x.shape
    T = B * S
    depth = w_in.shape[0]
    tm_proj = _pick_tile(T, 1024)
    tm_row = _pick_tile(T, 512)
    tm_merge = _pick_tile(T, 256)
    tq = _pick_tile(S, 512)

    cosf, sins = _rope_tables(positions)
    x2 = x.reshape(T, D)
    for l in range(depth):
        w_lin, w_silu, w_gelu, w_gate, w_small = _split_w_in(w_in[l])
        wuq = _pad_heads(mla_w_uq[l], MLA_QK, 0, MLA_QK, MLA_QK_PAD).astype(BF16)
        wuk = _pad_heads(mla_w_ukv[l], MLA_NOPE + MLA_V, 0, MLA_NOPE, MLA_NOPE).astype(BF16)
        wuv = _pad_heads(mla_w_ukv[l], MLA_NOPE + MLA_V, MLA_NOPE, MLA_NOPE + MLA_V, MLA_V).astype(BF16)
        qg = jnp.pad(mla_q_norm[l], (0, MLA_QK_PAD - MLA_QK)).reshape(1, MLA_QK_PAD)
        kg = jnp.pad(mla_k_norm[l], (0, MLA_QK_PAD - MLA_QK)).reshape(1, MLA_QK_PAD)
        wa_pad = jnp.zeros((SMALL_W, GLA_HEADS * GLA_DK), F32).at[
            MLA_ROPE:MLA_ROPE + GLA_GATE_RANK].set(gla_w_a2[l])

        h, small = _norm_call(x2, norm_g[l].reshape(1, D), w_small, tm_row)
        p_lin = _inproj_call(h, w_lin, _identity, tm_proj)
        p_silu = _inproj_call(h, w_silu, _silu, tm_proj)
        p_gelu = _inproj_call(h, w_gelu, _gelu_tanh, tm_proj)
        p_gate = _inproj_call(h, w_gate, _sigmoid, tm_proj)
        q, k, v = _mla_prep_call(p_lin, small, cosf, sins,
                                 mla_cq_norm[l].reshape(1, -1), mla_ckv_norm[l].reshape(1, -1),
                                 wuq, wuk, wuv, qg, kg, tm_row)
        ya = _attn_call(q, k, v, p_silu, B, S, tq)
        yb = _sgu_call(p_gelu, p_silu, sgu_v_norm[l].reshape(1, -1), sgu_w_s[l], sgu_b_s[l].T, tm_row)
        yc = _gla_call(p_lin, p_silu, small, wa_pad, gla_b_a[l].reshape(1, -1),
                       gla_o_norm[l].reshape(1, -1), B, S)
        x2 = _merge_call(x2, ya, yb, yc, p_gate, w_branch[l].astype(BF16), w_out[l].astype(BF16), tm_merge, S, tq)
    return x2.reshape(B, S, D)
```

```python
import functools
import math

import jax
import jax.numpy as jnp
from jax import lax
from jax.experimental import pallas as pl
from jax.experimental.pallas import tpu as pltpu

F32 = jnp.float32
BF16 = jnp.bfloat16

MLA_HEADS = 8
MLA_NOPE = 128
MLA_ROPE = 64
MLA_QK = MLA_NOPE + MLA_ROPE
MLA_V = 128
MLA_Q_RANK = 512
MLA_KV_RANK = 512
MLA_WIDTH = MLA_HEADS * MLA_V
MLA_QK_PAD = 256
ROPE_THETA = 10000.0
SGU_GROUPS = 8
SGU_CHUNK = 128
SGU_WIDTH = 1024
GLA_HEADS = 4
GLA_DK = 128
GLA_DV = 256
GLA_GATE_RANK = 16
GLA_TAU = 16.0
GLA_CHUNK = 128
GLA_WIDTH = GLA_HEADS * GLA_DV
N_BRANCH = 3
BRANCH_WIDTH = 1024
EPS = 1e-6
LANE = 128
LOG2_E = math.log2(math.e)
MASK_VALUE = -1e30
SMALL_W = LANE

PROJ_TN = 1024
LIN_CQKV, LIN_QK_C, LIN_V_C = range(3)
SILU_Z_A, SILU_Z_B, SILU_Z_C = range(3)
GELU_U_B, GELU_V_B = range(2)

VMEM_LIMIT = 56 * 1024 * 1024


def _params(*sem):
    return pltpu.CompilerParams(dimension_semantics=sem, vmem_limit_bytes=VMEM_LIMIT)


def _dot(a, b):
    return jnp.dot(a, b, preferred_element_type=F32)


def _dot_nt(a, b):
    return lax.dot_general(a, b, (((1,), (1,)), ((), ())), preferred_element_type=F32)


def _dot_tn(a, b):
    return lax.dot_general(a, b, (((0,), (0,)), ((), ())), preferred_element_type=F32)


def _sigmoid(x):
    return 0.5 * jnp.tanh(0.5 * x) + 0.5


def _gelu_tanh(x):
    c = math.sqrt(2.0 / math.pi)
    hx = 0.5 * x
    return hx * jnp.tanh(x * (c + (c * 0.044715) * (x * x))) + hx


def _silu(x):
    hx = 0.5 * x
    return hx * jnp.tanh(hx) + hx


def _identity(x):
    return x


def _resident(shape):
    nd = len(shape)
    return pl.BlockSpec(shape, lambda *_: (0,) * nd, pipeline_mode=pl.Buffered(1))


def _norm_kernel(x_ref, g_ref, ws_ref, h_ref, small_ref):
    x = x_ref[...]
    rstd = lax.rsqrt(jnp.mean(x * x, axis=-1, keepdims=True) + EPS)
    h = (x * rstd * g_ref[...]).astype(BF16)
    h_ref[...] = h
    small_ref[...] = _dot(h, ws_ref[...])


def _norm_call(x2, g, w_small, tm):
    T, D = x2.shape
    return pl.pallas_call(
        _norm_kernel,
        grid=(T // tm,),
        in_specs=[pl.BlockSpec((tm, D), lambda i: (i, 0)),
                  _resident((1, D)),
                  _resident((D, SMALL_W))],
        out_specs=[pl.BlockSpec((tm, D), lambda i: (i, 0)),
                   pl.BlockSpec((tm, SMALL_W), lambda i: (i, 0))],
        out_shape=[jax.ShapeDtypeStruct((T, D), BF16),
                   jax.ShapeDtypeStruct((T, SMALL_W), F32)],
        compiler_params=_params("parallel"),
    )(x2, g, w_small)


def _inproj_kernel(h_ref, w_ref, o_ref, *, act):
    o_ref[...] = act(_dot(h_ref[...], w_ref[...])).astype(o_ref.dtype)


def _inproj_call(h, w, act, tm):
    T, D = h.shape
    N = w.shape[1]
    return pl.pallas_call(
        functools.partial(_inproj_kernel, act=act),
        grid=(T // tm, N // PROJ_TN),
        in_specs=[pl.BlockSpec((tm, D), lambda i, j: (i, 0)),
                  pl.BlockSpec((D, PROJ_TN), lambda i, j: (0, j))],
        out_specs=pl.BlockSpec((tm, PROJ_TN), lambda i, j: (i, j)),
        out_shape=jax.ShapeDtypeStruct((T, N), BF16),
        compiler_params=_params("parallel", "arbitrary"),
    )(h, w)


def _rope128(r, cosf, sins):
    lane = lax.broadcasted_iota(jnp.int32, r.shape, 1)
    partner = jnp.where(lane < MLA_ROPE // 2,
                        pltpu.roll(r, LANE - MLA_ROPE // 2, axis=1),
                        pltpu.roll(r, MLA_ROPE // 2, axis=1))
    return r * cosf + partner * sins


def _mla_prep_kernel(cq_ref, ckv_ref, small_ref, cos_ref, sin_ref,
                     cqg_ref, ckvg_ref, wuq_ref, wuk_ref, wuv_ref, qg_ref, kg_ref,
                     q_ref, k_ref, v_ref):
    cosf = cos_ref[...]
    sins = sin_ref[...]
    scale = MLA_QK ** -0.5 * LOG2_E

    cq = cq_ref[...].astype(F32)
    nq = cq * lax.rsqrt(jnp.mean(cq * cq, axis=-1, keepdims=True) + EPS) * cqg_ref[...]
    q_all = _dot(nq.astype(BF16), wuq_ref[...])

    ckv = ckv_ref[...].astype(F32)
    nkv = ckv * lax.rsqrt(jnp.mean(ckv * ckv, axis=-1, keepdims=True) + EPS) * ckvg_ref[...]
    nkv = nkv.astype(BF16)
    k_all = _dot(nkv, wuk_ref[...])
    v_ref[...] = _dot(nkv, wuv_ref[...]).astype(v_ref.dtype)

    qg = qg_ref[...]
    kg = kg_ref[...]
    lane = lax.broadcasted_iota(jnp.int32, (1, LANE), 1)
    kr = jnp.where(lane < MLA_ROPE, small_ref[...], 0.0)
    kr_ss = jnp.sum(kr * kr, axis=-1, keepdims=True)
    kr_rot = _rope128(kr * kg[:, MLA_NOPE:], cosf, sins)

    for hd in range(MLA_HEADS):
        qb = q_all[:, hd * MLA_QK_PAD:(hd + 1) * MLA_QK_PAD]
        rq = lax.rsqrt(jnp.sum(qb * qb, axis=-1, keepdims=True) * (1.0 / MLA_QK) + EPS) * scale
        qn = qb * rq * qg
        q_ref[:, hd * MLA_QK_PAD:hd * MLA_QK_PAD + MLA_NOPE] = qn[:, :MLA_NOPE].astype(q_ref.dtype)
        q_ref[:, hd * MLA_QK_PAD + MLA_NOPE:(hd + 1) * MLA_QK_PAD] = _rope128(
            qn[:, MLA_NOPE:], cosf, sins).astype(q_ref.dtype)

        kb = k_all[:, hd * MLA_NOPE:(hd + 1) * MLA_NOPE]
        rk = lax.rsqrt((jnp.sum(kb * kb, axis=-1, keepdims=True) + kr_ss) * (1.0 / MLA_QK) + EPS)
        k_ref[:, hd * MLA_QK_PAD:hd * MLA_QK_PAD + MLA_NOPE] = (
            kb * rk * kg[:, :MLA_NOPE]).astype(k_ref.dtype)
        k_ref[:, hd * MLA_QK_PAD + MLA_NOPE:(hd + 1) * MLA_QK_PAD] = (kr_rot * rk).astype(k_ref.dtype)


def _mla_prep_call(proj, small, cosf, sins, cqg, ckvg, wuq, wuk, wuv, qg, kg, tm):
    T = proj.shape[0]
    HQ = MLA_HEADS * MLA_QK_PAD
    return pl.pallas_call(
        _mla_prep_kernel,
        grid=(T // tm,),
        in_specs=[pl.BlockSpec((tm, MLA_Q_RANK), lambda i: (i, 0)),
                  pl.BlockSpec((tm, MLA_KV_RANK), lambda i: (i, 1)),
                  pl.BlockSpec((tm, SMALL_W), lambda i: (i, 0)),
                  pl.BlockSpec((tm, LANE), lambda i: (i, 0)),
                  pl.BlockSpec((tm, LANE), lambda i: (i, 0)),
                  _resident((1, MLA_Q_RANK)),
                  _resident((1, MLA_KV_RANK)),
                  _resident((MLA_Q_RANK, HQ)),
                  _resident((MLA_KV_RANK, MLA_HEADS * MLA_NOPE)),
                  _resident((MLA_KV_RANK, MLA_WIDTH)),
                  _resident((1, MLA_QK_PAD)),
                  _resident((1, MLA_QK_PAD))],
        out_specs=[pl.BlockSpec((tm, HQ), lambda i: (i, 0)),
                   pl.BlockSpec((tm, HQ), lambda i: (i, 0)),
                   pl.BlockSpec((tm, MLA_WIDTH), lambda i: (i, 0))],
        out_shape=[jax.ShapeDtypeStruct((T, HQ), BF16),
                   jax.ShapeDtypeStruct((T, HQ), BF16),
                   jax.ShapeDtypeStruct((T, MLA_WIDTH), BF16)],
        compiler_params=_params("parallel"),
    )(proj, proj, small, cosf, sins, cqg, ckvg, wuq, wuk, wuv, qg, kg)


def _attn_kernel(qlo_ref, qhi_ref, k_ref, v_ref, szlo_ref, szhi_ref, o_ref,
                 q2_ref, s_ref, mf_ref, lf_ref, pv_ref, *, tq, nq, n_steps):
    g = pl.program_id(0)
    half = nq // 2
    nt = nq + 1
    nl = tq // LANE
    i1 = jnp.minimum(g, n_steps - 1) % half
    i2 = jnp.maximum(g - 1, 0) % half

    @pl.when(g == 0)
    def _():
        for t in range(nt):
            s_ref[t] = jnp.zeros((tq, tq), F32)
            mf_ref[t] = jnp.zeros((tq, LANE), F32)

    q2_ref[0] = qlo_ref[...]
    q2_ref[1] = qhi_ref[...]

    def lane_fold(x, op):
        r = x[:, :LANE]
        for c in range(1, nl):
            r = op(r, x[:, c * LANE:(c + 1) * LANE])
        return r

    def is_lo(p, i):
        return True if p == 0 else (False if p >= half else p <= i)

    def pick(p, i, lo, hi):
        c = is_lo(p, i)
        if c is True:
            return lo
        if c is False:
            return hi
        return jnp.where(c, lo, hi)

    def kv_rows(p, i):
        if p == 0:
            j = i
        elif p == nq:
            j = nq - 1 - i
        else:
            j = pick(p, i, p - 1, p - i - 1)
        return pl.ds(pl.multiple_of(j * tq, tq), tq)

    def combine(ref, i, op, init):
        lo = jnp.full((tq, LANE), init, F32)
        hi = jnp.full((tq, LANE), init, F32)
        for p in range(nt):
            c = is_lo(p, i)
            x = ref[p]
            if c is True:
                lo = op(lo, x)
            elif c is False:
                hi = op(hi, x)
            else:
                lo = op(lo, jnp.where(c, x, init))
                hi = op(hi, jnp.where(c, init, x))
        return lo, hi

    m_lo, m_hi = combine(mf_ref, i2, jnp.maximum, MASK_VALUE)
    m_lo = jnp.max(m_lo, axis=-1, keepdims=True)
    m_hi = jnp.max(m_hi, axis=-1, keepdims=True)

    causal = (lax.broadcasted_iota(jnp.int32, (tq, tq), 0)
              >= lax.broadcasted_iota(jnp.int32, (tq, tq), 1))
    order_zero = jnp.zeros((tq, 1), F32)
    for p in range(nt):
        e = jnp.exp2(s_ref[p] - (pick(p, i2, m_lo, m_hi) + order_zero))
        lf_ref[p] = lane_fold(e, jnp.add)
        pv_ref[p] = _dot(e.astype(BF16), v_ref[kv_rows(p, i2), :])
        s = _dot_nt(q2_ref[pick(p, i1, 0, 1)], k_ref[kv_rows(p, i1), :])
        if p == 0 or p == nq:
            s = jnp.where(causal, s, MASK_VALUE)
        s_ref[p] = s
        mf = lane_fold(s, jnp.maximum)
        mf_ref[p] = mf
        bits = lax.bitcast_convert_type(mf[:, :1], jnp.uint32)
        order_zero = lax.bitcast_convert_type((bits >> 16) >> 16, F32)

    l_lo, l_hi = combine(lf_ref, i2, jnp.add, 0.0)
    a_lo, a_hi = combine(pv_ref, i2, jnp.add, 0.0)
    l_lo = jnp.sum(l_lo, axis=-1, keepdims=True)
    l_hi = jnp.sum(l_hi, axis=-1, keepdims=True)
    o_ref[:tq, :] = (a_lo / l_lo * szlo_ref[...].astype(F32)).astype(o_ref.dtype)
    o_ref[tq:, :] = (a_hi / l_hi * szhi_ref[...].astype(F32)).astype(o_ref.dtype)


def _attn_call(q, k, v, p_silu, B, S, tq):
    T = q.shape[0]
    nq = S // tq
    assert nq % 2 == 0 and MLA_V == LANE
    half = nq // 2
    n_steps = B * MLA_HEADS * half
    sz_col0 = SILU_Z_A * PROJ_TN // MLA_V

    def bhi(g):
        bh, i = g // half, g % half
        return bh // MLA_HEADS, bh % MLA_HEADS, i

    def first(g):
        return bhi(jnp.minimum(g, n_steps - 1))

    def second(g):
        return bhi(jnp.maximum(g - 1, 0))

    def lo_blk(f, col0=0):
        def index(g):
            b, h, i = f(g)
            return b * nq + i, col0 + h
        return index

    def hi_blk(f, col0=0):
        def index(g):
            b, h, i = f(g)
            return b * nq + nq - 1 - i, col0 + h
        return index

    def seq_blk(f):
        def index(g):
            b, h, _ = f(g)
            return b, h
        return index

    def out_blk(g):
        b, h, i = second(g)
        return b * half + i, h

    nt = nq + 1
    return pl.pallas_call(
        functools.partial(_attn_kernel, tq=tq, nq=nq, n_steps=n_steps),
        grid=(n_steps + 1,),
        in_specs=[pl.BlockSpec((tq, MLA_QK_PAD), lo_blk(first)),
                  pl.BlockSpec((tq, MLA_QK_PAD), hi_blk(first)),
                  pl.BlockSpec((S, MLA_QK_PAD), seq_blk(first)),
                  pl.BlockSpec((S, MLA_V), seq_blk(second)),
                  pl.BlockSpec((tq, MLA_V), lo_blk(second, sz_col0)),
                  pl.BlockSpec((tq, MLA_V), hi_blk(second, sz_col0))],
        out_specs=pl.BlockSpec((2 * tq, MLA_V), out_blk),
        out_shape=jax.ShapeDtypeStruct((T, MLA_WIDTH), BF16),
        scratch_shapes=[pltpu.VMEM((2, tq, MLA_QK_PAD), BF16),
                        pltpu.VMEM((nt, tq, tq), F32),
                        pltpu.VMEM((nt, tq, LANE), F32),
                        pltpu.VMEM((nt, tq, LANE), F32),
                        pltpu.VMEM((nt, tq, MLA_V), F32)],
        compiler_params=_params("arbitrary"),
    )(q, q, k, v, p_silu, p_silu)


def _paired_block(r, S, tq, tm):
    per_seq, per_tile, nq = S // tm, tq // tm, S // tq
    b, w = r // per_seq, r % per_seq
    u, sub = w // per_tile, w % per_tile
    pos = jnp.where(u < nq // 2, 2 * u, 2 * (nq - 1 - u) + 1)
    return b * per_seq + pos * per_tile + sub


def _sgu_kernel(u_ref, v_ref, sz_ref, vg_ref, ws_ref, bs_ref, o_ref, *, tm):
    v = v_ref[...].astype(F32)
    vn = (v * lax.rsqrt(jnp.mean(v * v, axis=-1, keepdims=True) + EPS) * vg_ref[...]).astype(BF16)
    row = lax.broadcasted_iota(jnp.int32, (SGU_CHUNK, SGU_CHUNK), 0)
    col = lax.broadcasted_iota(jnp.int32, (SGU_CHUNK, SGU_CHUNK), 1)
    gd = SGU_WIDTH // SGU_GROUPS
    for g in range(SGU_GROUPS):
        w = jnp.where(row >= col, ws_ref[g], 0.0).astype(BF16)
        bias = bs_ref[:, g:g + 1]
        for c in range(tm // SGU_CHUNK):
            rs = slice(c * SGU_CHUNK, (c + 1) * SGU_CHUNK)
            cs = slice(g * gd, (g + 1) * gd)
            mix = _dot(w, vn[rs, cs]) + bias
            o_ref[rs, cs] = (u_ref[rs, cs].astype(F32) * mix
                             * sz_ref[rs, cs].astype(F32)).astype(o_ref.dtype)


def _sgu_call(p_gelu, p_silu, vg, ws, bs_t, tm):
    T = p_gelu.shape[0]
    return pl.pallas_call(
        functools.partial(_sgu_kernel, tm=tm),
        grid=(T // tm,),
        in_specs=[pl.BlockSpec((tm, SGU_WIDTH), lambda i: (i, GELU_U_B)),
                  pl.BlockSpec((tm, SGU_WIDTH), lambda i: (i, GELU_V_B)),
                  pl.BlockSpec((tm, SGU_WIDTH), lambda i: (i, SILU_Z_B)),
                  _resident((1, SGU_WIDTH)),
                  _resident((SGU_GROUPS, SGU_CHUNK, SGU_CHUNK)),
                  _resident((SGU_CHUNK, SGU_GROUPS))],
        out_specs=pl.BlockSpec((tm, SGU_WIDTH), lambda i: (i, 0)),
        out_shape=jax.ShapeDtypeStruct((T, SGU_WIDTH), BF16),
        compiler_params=_params("parallel"),
    )(p_gelu, p_gelu, p_silu, vg, ws, bs_t)


def _gla_kernel(qk_ref, v_ref, sz_ref, small_ref, wa_ref, ba_ref, og_ref, o_ref, st_ref):
    L = GLA_CHUNK

    @pl.when(pl.program_id(1) == 0)
    def _():
        st_ref[...] = jnp.zeros(st_ref.shape, F32)

    xg = jnp.dot(small_ref[...], wa_ref[...], preferred_element_type=F32,
                 precision=lax.Precision.HIGHEST) + ba_ref[...]
    log_a = (jnp.minimum(xg, 0.0) - jnp.log(1.0 + jnp.exp(-jnp.abs(xg)))) * (1.0 / GLA_TAU)
    row = lax.broadcasted_iota(jnp.int32, (L, L), 0)
    col = lax.broadcasted_iota(jnp.int32, (L, L), 1)
    causal = row >= col
    bcum = jnp.dot(causal.astype(F32), log_a, preferred_element_type=F32,
                   precision=lax.Precision.HIGHEST)
    og = og_ref[...]

    for hd in range(GLA_HEADS):
        ks = slice(hd * GLA_DK, (hd + 1) * GLA_DK)
        vs = slice(hd * GLA_DV, (hd + 1) * GLA_DV)
        b = bcum[:, ks]
        b_mid = b[L // 2:L // 2 + 1, :]
        b_last = b[L - 1:L, :]
        q = qk_ref[:, ks].astype(F32) * (GLA_DK ** -0.5)
        k = qk_ref[:, GLA_HEADS * GLA_DK + hd * GLA_DK:GLA_HEADS * GLA_DK + (hd + 1) * GLA_DK].astype(F32)
        v = v_ref[:, vs]
        q_t = (q * jnp.exp(b - b_mid)).astype(BF16)
        k_t = (k * jnp.exp(b_mid - b)).astype(BF16)
        attn = jnp.where(causal, _dot_nt(q_t, k_t), 0.0).astype(BF16)
        st = st_ref[hd]
        o = _dot_nt((q * jnp.exp(b)).astype(BF16), st.astype(BF16)) + _dot(attn, v)
        k_s = (k * jnp.exp(b_last - b)).astype(BF16)
        st_ref[hd] = jnp.exp(b_last) * st + _dot_tn(v, k_s)
        on = o * lax.rsqrt(jnp.mean(o * o, axis=-1, keepdims=True) + EPS) * og
        o_ref[:, vs] = (on * sz_ref[:, vs].astype(F32)).astype(o_ref.dtype)


def _gla_call(p_lin, p_silu, small, wa_pad, ba, og, B, S):
    T = p_lin.shape[0]
    nc = S // GLA_CHUNK
    L = GLA_CHUNK
    return pl.pallas_call(
        _gla_kernel,
        grid=(B, nc),
        in_specs=[pl.BlockSpec((L, PROJ_TN), lambda b, c: (b * nc + c, LIN_QK_C)),
                  pl.BlockSpec((L, GLA_WIDTH), lambda b, c: (b * nc + c, LIN_V_C)),
                  pl.BlockSpec((L, GLA_WIDTH), lambda b, c: (b * nc + c, SILU_Z_C)),
                  pl.BlockSpec((L, SMALL_W), lambda b, c: (b * nc + c, 0)),
                  _resident((SMALL_W, GLA_HEADS * GLA_DK)),
                  _resident((1, GLA_HEADS * GLA_DK)),
                  _resident((1, GLA_DV))],
        out_specs=pl.BlockSpec((L, GLA_WIDTH), lambda b, c: (b * nc + c, 0)),
        out_shape=jax.ShapeDtypeStruct((T, GLA_WIDTH), BF16),
        scratch_shapes=[pltpu.VMEM((GLA_HEADS, GLA_DV, GLA_DK), F32)],
        compiler_params=_params("parallel", "arbitrary"),
    )(p_lin, p_lin, p_silu, small, wa_pad, ba, og)


def _merge_kernel(x_ref, ya_ref, yb_ref, yc_ref, ga_ref, gb_ref, gc_ref, wb_ref, wo_ref, o_ref):
    merged = ga_ref[...].astype(F32) * _dot(ya_ref[...], wb_ref[0])
    merged += gb_ref[...].astype(F32) * _dot(yb_ref[...], wb_ref[1])
    merged += gc_ref[...].astype(F32) * _dot(yc_ref[...], wb_ref[2])
    o_ref[...] = x_ref[...] + _dot(merged.astype(BF16), wo_ref[...])


def _merge_call(x2, ya, yb, yc, p_gate, wb, wo, tm, S, tq):
    T, D = x2.shape
    yspec = pl.BlockSpec((tm, BRANCH_WIDTH), lambda i: (i, 0))
    ya_spec = pl.BlockSpec((tm, BRANCH_WIDTH), lambda i: (_paired_block(i, S, tq, tm), 0))
    return pl.pallas_call(
        _merge_kernel,
        grid=(T // tm,),
        in_specs=[pl.BlockSpec((tm, D), lambda i: (i, 0)),
                  ya_spec, yspec, yspec,
                  pl.BlockSpec((tm, D), lambda i: (i, 0)),
                  pl.BlockSpec((tm, D), lambda i: (i, 1)),
                  pl.BlockSpec((tm, D), lambda i: (i, 2)),
                  _resident((N_BRANCH, BRANCH_WIDTH, D)),
                  _resident((D, D))],
        out_specs=pl.BlockSpec((tm, D), lambda i: (i, 0)),
        out_shape=jax.ShapeDtypeStruct((T, D), F32),
        compiler_params=_params("parallel"),
    )(x2, ya, yb, yc, p_gate, p_gate, p_gate, wb, wo)


def _split_w_in(w_in):
    sizes = (MLA_Q_RANK, MLA_KV_RANK, MLA_ROPE, MLA_WIDTH, SGU_WIDTH, SGU_WIDTH, SGU_WIDTH,
             GLA_HEADS * GLA_DK, GLA_HEADS * GLA_DK, GLA_WIDTH, GLA_GATE_RANK, GLA_WIDTH)
    parts, off = [], 0
    for s in sizes:
        parts.append(w_in[:, off:off + s])
        off += s
    parts.append(w_in[:, off:])
    (c_q, c_kv, k_rope, z_a, u_b, v_b, z_b, q_c, k_c, v_c, a_r, z_c, gates) = parts
    w_lin = jnp.concatenate([c_q, c_kv, q_c, k_c, v_c], axis=1).astype(BF16)
    w_silu = jnp.concatenate([z_a, z_b, z_c], axis=1).astype(BF16)
    w_gelu = jnp.concatenate([u_b, v_b], axis=1).astype(BF16)
    w_gate = gates.astype(BF16)
    pad = jnp.zeros((w_in.shape[0], SMALL_W - MLA_ROPE - GLA_GATE_RANK), w_in.dtype)
    w_small = jnp.concatenate([k_rope, a_r, pad], axis=1).astype(BF16)
    return w_lin, w_silu, w_gelu, w_gate, w_small


def _pad_heads(w, per_head, lo, hi, width):
    r = w.shape[0]
    w = w.reshape(r, MLA_HEADS, per_head)[:, :, lo:hi]
    w = jnp.pad(w, ((0, 0), (0, 0), (0, width - (hi - lo))))
    return w.reshape(r, MLA_HEADS * width)


def _rope_tables(positions):
    half = MLA_ROPE // 2
    inv_freq = 1.0 / (ROPE_THETA ** (jnp.arange(0, MLA_ROPE, 2, dtype=F32) / MLA_ROPE))
    ang = positions.astype(F32).reshape(-1, 1) * inv_freq
    cos, sin = jnp.cos(ang), jnp.sin(ang)
    z = jnp.zeros((ang.shape[0], LANE - 2 * half), F32)
    return (jnp.concatenate([cos, cos, z], axis=1), jnp.concatenate([-sin, sin, z], axis=1))


def _pick_tile(n, want):
    t = min(n, want)
    while n % t:
        t //= 2
    return t


def kernel(x, positions, norm_g, w_in, mla_cq_norm, mla_ckv_norm, mla_w_uq, mla_w_ukv,
           mla_q_norm, mla_k_norm, sgu_v_norm, sgu_w_s, sgu_b_s, gla_w_a2, gla_b_a,
           gla_o_norm, w_branch, w_out):
    B, S, D = x.shape
    T = B * S
    depth = w_in.shape[0]
    tm_proj = _pick_tile(T, 2048)
    tm_row = _pick_tile(T, 512)
    tm_merge = _pick_tile(T, 256)
    tq = _pick_tile(S, 512)

    cosf, sins = _rope_tables(positions)
    x2 = x.reshape(T, D)
    for l in range(depth):
        w_lin, w_silu, w_gelu, w_gate, w_small = _split_w_in(w_in[l])
        wuq = _pad_heads(mla_w_uq[l], MLA_QK, 0, MLA_QK, MLA_QK_PAD).astype(BF16)
        wuk = _pad_heads(mla_w_ukv[l], MLA_NOPE + MLA_V, 0, MLA_NOPE, MLA_NOPE).astype(BF16)
        wuv = _pad_heads(mla_w_ukv[l], MLA_NOPE + MLA_V, MLA_NOPE, MLA_NOPE + MLA_V, MLA_V).astype(BF16)
        qg = jnp.pad(mla_q_norm[l], (0, MLA_QK_PAD - MLA_QK)).reshape(1, MLA_QK_PAD)
        kg = jnp.pad(mla_k_norm[l], (0, MLA_QK_PAD - MLA_QK)).reshape(1, MLA_QK_PAD)
        wa_pad = jnp.zeros((SMALL_W, GLA_HEADS * GLA_DK), F32).at[
            MLA_ROPE:MLA_ROPE + GLA_GATE_RANK].set(gla_w_a2[l])

        h, small = _norm_call(x2, norm_g[l].reshape(1, D), w_small, tm_row)
        p_lin = _inproj_call(h, w_lin, _identity, tm_proj)
        p_silu = _inproj_call(h, w_silu, _silu, tm_proj)
        p_gelu = _inproj_call(h, w_gelu, _gelu_tanh, tm_proj)
        p_gate = _inproj_call(h, w_gate, _sigmoid, tm_proj)
        q, k, v = _mla_prep_call(p_lin, small, cosf, sins,
                                 mla_cq_norm[l].reshape(1, -1), mla_ckv_norm[l].reshape(1, -1),
                                 wuq, wuk, wuv, qg, kg, tm_row)
        ya = _attn_call(q, k, v, p_silu, B, S, tq)
        yb = _sgu_call(p_gelu, p_silu, sgu_v_norm[l].reshape(1, -1), sgu_w_s[l], sgu_b_s[l].T, tm_row)
        yc = _gla_call(p_lin, p_silu, small, wa_pad, gla_b_a[l].reshape(1, -1),
                       gla_o_norm[l].reshape(1, -1), B, S)
        x2 = _merge_call(x2, ya, yb, yc, p_gate, w_branch[l].astype(BF16), w_out[l].astype(BF16), tm_merge, S, tq)
    return x2.reshape(B, S, D)
```

```python
import functools
import math

import jax
import jax.numpy as jnp
from jax import lax
from jax.experimental import pallas as pl
from jax.experimental.pallas import tpu as pltpu

F32 = jnp.float32
BF16 = jnp.bfloat16

MLA_HEADS = 8
MLA_NOPE = 128
MLA_ROPE = 64
MLA_QK = MLA_NOPE + MLA_ROPE
MLA_V = 128
MLA_Q_RANK = 512
MLA_KV_RANK = 512
MLA_WIDTH = MLA_HEADS * MLA_V
MLA_QK_PAD = 256
ROPE_THETA = 10000.0
SGU_GROUPS = 8
SGU_CHUNK = 128
SGU_WIDTH = 1024
GLA_HEADS = 4
GLA_DK = 128
GLA_DV = 256
GLA_GATE_RANK = 16
GLA_TAU = 16.0
GLA_CHUNK = 128
GLA_WIDTH = GLA_HEADS * GLA_DV
N_BRANCH = 3
BRANCH_WIDTH = 1024
EPS = 1e-6
LANE = 128
LOG2_E = math.log2(math.e)
MASK_VALUE = -1e30
SMALL_W = LANE

PROJ_TN = 1024
LIN_CQKV, LIN_QK_C, LIN_V_C = range(3)
SILU_Z_A, SILU_Z_B, SILU_Z_C = range(3)
GELU_U_B, GELU_V_B = range(2)

VMEM_LIMIT = 56 * 1024 * 1024


def _params(*sem):
    return pltpu.CompilerParams(dimension_semantics=sem, vmem_limit_bytes=VMEM_LIMIT)


def _dot(a, b):
    return jnp.dot(a, b, preferred_element_type=F32)


def _dot_nt(a, b):
    return lax.dot_general(a, b, (((1,), (1,)), ((), ())), preferred_element_type=F32)


def _dot_tn(a, b):
    return lax.dot_general(a, b, (((0,), (0,)), ((), ())), preferred_element_type=F32)


def _sigmoid(x):
    return 0.5 * jnp.tanh(0.5 * x) + 0.5


def _gelu_tanh(x):
    c = math.sqrt(2.0 / math.pi)
    hx = 0.5 * x
    return hx * jnp.tanh(x * (c + (c * 0.044715) * (x * x))) + hx


def _silu(x):
    hx = 0.5 * x
    return hx * jnp.tanh(hx) + hx


def _identity(x):
    return x


def _resident(shape):
    nd = len(shape)
    return pl.BlockSpec(shape, lambda *_: (0,) * nd, pipeline_mode=pl.Buffered(1))


def _norm_kernel(x_ref, g_ref, ws_ref, h_ref, small_ref):
    x = x_ref[...]
    rstd = lax.rsqrt(jnp.mean(x * x, axis=-1, keepdims=True) + EPS)
    h = (x * rstd * g_ref[...]).astype(BF16)
    h_ref[...] = h
    small_ref[...] = _dot(h, ws_ref[...])


def _norm_call(x2, g, w_small, tm):
    T, D = x2.shape
    return pl.pallas_call(
        _norm_kernel,
        grid=(T // tm,),
        in_specs=[pl.BlockSpec((tm, D), lambda i: (i, 0)),
                  _resident((1, D)),
                  _resident((D, SMALL_W))],
        out_specs=[pl.BlockSpec((tm, D), lambda i: (i, 0)),
                   pl.BlockSpec((tm, SMALL_W), lambda i: (i, 0))],
        out_shape=[jax.ShapeDtypeStruct((T, D), BF16),
                   jax.ShapeDtypeStruct((T, SMALL_W), F32)],
        compiler_params=_params("parallel"),
    )(x2, g, w_small)


def _inproj_kernel(h_ref, w_ref, o_ref, *, act):
    o_ref[...] = act(_dot(h_ref[...], w_ref[...])).astype(o_ref.dtype)


def _inproj_call(h, w, act, tm):
    T, D = h.shape
    N = w.shape[1]
    return pl.pallas_call(
        functools.partial(_inproj_kernel, act=act),
        grid=(T // tm, N // PROJ_TN),
        in_specs=[pl.BlockSpec((tm, D), lambda i, j: (i, 0)),
                  pl.BlockSpec((D, PROJ_TN), lambda i, j: (0, j))],
        out_specs=pl.BlockSpec((tm, PROJ_TN), lambda i, j: (i, j)),
        out_shape=jax.ShapeDtypeStruct((T, N), BF16),
        compiler_params=_params("parallel", "arbitrary"),
    )(h, w)


def _rope128(r, cosf, sins):
    lane = lax.broadcasted_iota(jnp.int32, r.shape, 1)
    partner = jnp.where(lane < MLA_ROPE // 2,
                        pltpu.roll(r, LANE - MLA_ROPE // 2, axis=1),
                        pltpu.roll(r, MLA_ROPE // 2, axis=1))
    return r * cosf + partner * sins


def _mla_prep_kernel(cq_ref, ckv_ref, small_ref, cos_ref, sin_ref,
                     cqg_ref, ckvg_ref, wuq_ref, wuk_ref, wuv_ref, qg_ref, kg_ref,
                     q_ref, k_ref, v_ref):
    cosf = cos_ref[...]
    sins = sin_ref[...]
    scale = MLA_QK ** -0.5 * LOG2_E

    cq = cq_ref[...].astype(F32)
    nq = cq * lax.rsqrt(jnp.mean(cq * cq, axis=-1, keepdims=True) + EPS) * cqg_ref[...]
    q_all = _dot(nq.astype(BF16), wuq_ref[...])

    ckv = ckv_ref[...].astype(F32)
    nkv = ckv * lax.rsqrt(jnp.mean(ckv * ckv, axis=-1, keepdims=True) + EPS) * ckvg_ref[...]
    nkv = nkv.astype(BF16)
    k_all = _dot(nkv, wuk_ref[...])
    v_ref[...] = _dot(nkv, wuv_ref[...]).astype(v_ref.dtype)

    qg = qg_ref[...]
    kg = kg_ref[...]
    lane = lax.broadcasted_iota(jnp.int32, (1, LANE), 1)
    kr = jnp.where(lane < MLA_ROPE, small_ref[...], 0.0)
    kr_ss = jnp.sum(kr * kr, axis=-1, keepdims=True)
    kr_rot = _rope128(kr * kg[:, MLA_NOPE:], cosf, sins)

    for hd in range(MLA_HEADS):
        qb = q_all[:, hd * MLA_QK_PAD:(hd + 1) * MLA_QK_PAD]
        rq = lax.rsqrt(jnp.sum(qb * qb, axis=-1, keepdims=True) * (1.0 / MLA_QK) + EPS) * scale
        qn = qb * rq * qg
        q_ref[:, hd * MLA_QK_PAD:hd * MLA_QK_PAD + MLA_NOPE] = qn[:, :MLA_NOPE].astype(q_ref.dtype)
        q_ref[:, hd * MLA_QK_PAD + MLA_NOPE:(hd + 1) * MLA_QK_PAD] = _rope128(
            qn[:, MLA_NOPE:], cosf, sins).astype(q_ref.dtype)

        kb = k_all[:, hd * MLA_NOPE:(hd + 1) * MLA_NOPE]
        rk = lax.rsqrt((jnp.sum(kb * kb, axis=-1, keepdims=True) + kr_ss) * (1.0 / MLA_QK) + EPS)
        k_ref[:, hd * MLA_QK_PAD:hd * MLA_QK_PAD + MLA_NOPE] = (
            kb * rk * kg[:, :MLA_NOPE]).astype(k_ref.dtype)
        k_ref[:, hd * MLA_QK_PAD + MLA_NOPE:(hd + 1) * MLA_QK_PAD] = (kr_rot * rk).astype(k_ref.dtype)


def _mla_prep_call(proj, small, cosf, sins, cqg, ckvg, wuq, wuk, wuv, qg, kg, tm):
    T = proj.shape[0]
    HQ = MLA_HEADS * MLA_QK_PAD
    return pl.pallas_call(
        _mla_prep_kernel,
        grid=(T // tm,),
        in_specs=[pl.BlockSpec((tm, MLA_Q_RANK), lambda i: (i, 0)),
                  pl.BlockSpec((tm, MLA_KV_RANK), lambda i: (i, 1)),
                  pl.BlockSpec((tm, SMALL_W), lambda i: (i, 0)),
                  pl.BlockSpec((tm, LANE), lambda i: (i, 0)),
                  pl.BlockSpec((tm, LANE), lambda i: (i, 0)),
                  _resident((1, MLA_Q_RANK)),
                  _resident((1, MLA_KV_RANK)),
                  _resident((MLA_Q_RANK, HQ)),
                  _resident((MLA_KV_RANK, MLA_HEADS * MLA_NOPE)),
                  _resident((MLA_KV_RANK, MLA_WIDTH)),
                  _resident((1, MLA_QK_PAD)),
                  _resident((1, MLA_QK_PAD))],
        out_specs=[pl.BlockSpec((tm, HQ), lambda i: (i, 0)),
                   pl.BlockSpec((tm, HQ), lambda i: (i, 0)),
                   pl.BlockSpec((tm, MLA_WIDTH), lambda i: (i, 0))],
        out_shape=[jax.ShapeDtypeStruct((T, HQ), BF16),
                   jax.ShapeDtypeStruct((T, HQ), BF16),
                   jax.ShapeDtypeStruct((T, MLA_WIDTH), BF16)],
        compiler_params=_params("parallel"),
    )(proj, proj, small, cosf, sins, cqg, ckvg, wuq, wuk, wuv, qg, kg)


def _attn_kernel(qlo_ref, qhi_ref, k_ref, v_ref, szlo_ref, szhi_ref, o_ref,
                 q2_ref, s_ref, mf_ref, lf_ref, pv_ref, *, tq, nq, n_steps):
    g = pl.program_id(0)
    half = nq // 2
    nt = nq + 1
    nl = tq // LANE
    i1 = jnp.minimum(g, n_steps - 1) % half
    i2 = jnp.maximum(g - 1, 0) % half

    @pl.when(g == 0)
    def _():
        for t in range(nt):
            s_ref[t] = jnp.zeros((tq, tq), F32)
            mf_ref[t] = jnp.zeros((tq, LANE), F32)

    q2_ref[0] = qlo_ref[...]
    q2_ref[1] = qhi_ref[...]

    def lane_fold(x, op):
        r = x[:, :LANE]
        for c in range(1, nl):
            r = op(r, x[:, c * LANE:(c + 1) * LANE])
        return r

    def is_lo(p, i):
        return True if p == 0 else (False if p >= half else p <= i)

    def pick(p, i, lo, hi):
        c = is_lo(p, i)
        if c is True:
            return lo
        if c is False:
            return hi
        return jnp.where(c, lo, hi)

    def kv_rows(p, i):
        if p == 0:
            j = i
        elif p == nq:
            j = nq - 1 - i
        else:
            j = pick(p, i, p - 1, p - i - 1)
        return pl.ds(pl.multiple_of(j * tq, tq), tq)

    def combine(ref, i, op, init):
        lo = jnp.full((tq, LANE), init, F32)
        hi = jnp.full((tq, LANE), init, F32)
        for p in range(nt):
            c = is_lo(p, i)
            x = ref[p]
            if c is True:
                lo = op(lo, x)
            elif c is False:
                hi = op(hi, x)
            else:
                lo = op(lo, jnp.where(c, x, init))
                hi = op(hi, jnp.where(c, init, x))
        return lo, hi

    m_lo, m_hi = combine(mf_ref, i2, jnp.maximum, MASK_VALUE)
    m_lo = jnp.max(m_lo, axis=-1, keepdims=True)
    m_hi = jnp.max(m_hi, axis=-1, keepdims=True)

    causal = (lax.broadcasted_iota(jnp.int32, (tq, tq), 0)
              >= lax.broadcasted_iota(jnp.int32, (tq, tq), 1))
    order_zero = jnp.zeros((tq, 1), F32)
    for p in range(nt):
        e = jnp.exp2(s_ref[p] - (pick(p, i2, m_lo, m_hi) + order_zero))
        lf_ref[p] = lane_fold(e, jnp.add)
        pv_ref[p] = _dot(e.astype(BF16), v_ref[kv_rows(p, i2), :])
        s = _dot_nt(q2_ref[pick(p, i1, 0, 1)], k_ref[kv_rows(p, i1), :])
        if p == 0 or p == nq:
            s = jnp.where(causal, s, MASK_VALUE)
        s_ref[p] = s
        mf = lane_fold(s, jnp.maximum)
        mf_ref[p] = mf
        bits = lax.bitcast_convert_type(mf[:, :1], jnp.uint32)
        order_zero = lax.bitcast_convert_type((bits >> 16) >> 16, F32)

    l_lo, l_hi = combine(lf_ref, i2, jnp.add, 0.0)
    a_lo, a_hi = combine(pv_ref, i2, jnp.add, 0.0)
    l_lo = jnp.sum(l_lo, axis=-1, keepdims=True)
    l_hi = jnp.sum(l_hi, axis=-1, keepdims=True)
    o_ref[:tq, :] = (a_lo / l_lo * szlo_ref[...].astype(F32)).astype(o_ref.dtype)
    o_ref[tq:, :] = (a_hi / l_hi * szhi_ref[...].astype(F32)).astype(o_ref.dtype)


def _attn_call(q, k, v, p_silu, B, S, tq):
    T = q.shape[0]
    nq = S // tq
    assert nq % 2 == 0 and MLA_V == LANE
    half = nq // 2
    n_steps = B * MLA_HEADS * half
    sz_col0 = SILU_Z_A * PROJ_TN // MLA_V

    def bhi(g):
        bh, i = g // half, g % half
        return bh // MLA_HEADS, bh % MLA_HEADS, i

    def first(g):
        return bhi(jnp.minimum(g, n_steps - 1))

    def second(g):
        return bhi(jnp.maximum(g - 1, 0))

    def lo_blk(f, col0=0):
        def index(g):
            b, h, i = f(g)
            return b * nq + i, col0 + h
        return index

    def hi_blk(f, col0=0):
        def index(g):
            b, h, i = f(g)
            return b * nq + nq - 1 - i, col0 + h
        return index

    def seq_blk(f):
        def index(g):
            b, h, _ = f(g)
            return b, h
        return index

    def out_blk(g):
        b, h, i = second(g)
        return b * half + i, h

    nt = nq + 1
    return pl.pallas_call(
        functools.partial(_attn_kernel, tq=tq, nq=nq, n_steps=n_steps),
        grid=(n_steps + 1,),
        in_specs=[pl.BlockSpec((tq, MLA_QK_PAD), lo_blk(first)),
                  pl.BlockSpec((tq, MLA_QK_PAD), hi_blk(first)),
                  pl.BlockSpec((S, MLA_QK_PAD), seq_blk(first)),
                  pl.BlockSpec((S, MLA_V), seq_blk(second)),
                  pl.BlockSpec((tq, MLA_V), lo_blk(second, sz_col0)),
                  pl.BlockSpec((tq, MLA_V), hi_blk(second, sz_col0))],
        out_specs=pl.BlockSpec((2 * tq, MLA_V), out_blk),
        out_shape=jax.ShapeDtypeStruct((T, MLA_WIDTH), BF16),
        scratch_shapes=[pltpu.VMEM((2, tq, MLA_QK_PAD), BF16),
                        pltpu.VMEM((nt, tq, tq), F32),
                        pltpu.VMEM((nt, tq, LANE), F32),
                        pltpu.VMEM((nt, tq, LANE), F32),
                        pltpu.VMEM((nt, tq, MLA_V), F32)],
        compiler_params=_params("arbitrary"),
    )(q, q, k, v, p_silu, p_silu)


def _paired_block(r, S, tq, tm):
    per_seq, per_tile, nq = S // tm, tq // tm, S // tq
    b, w = r // per_seq, r % per_seq
    u, sub = w // per_tile, w % per_tile
    pos = jnp.where(u < nq // 2, 2 * u, 2 * (nq - 1 - u) + 1)
    return b * per_seq + pos * per_tile + sub


def _sgu_kernel(u_ref, v_ref, sz_ref, vg_ref, ws_ref, bs_ref, o_ref, *, tm):
    v = v_ref[...].astype(F32)
    vn = (v * lax.rsqrt(jnp.mean(v * v, axis=-1, keepdims=True) + EPS) * vg_ref[...]).astype(BF16)
    row = lax.broadcasted_iota(jnp.int32, (SGU_CHUNK, SGU_CHUNK), 0)
    col = lax.broadcasted_iota(jnp.int32, (SGU_CHUNK, SGU_CHUNK), 1)
    gd = SGU_WIDTH // SGU_GROUPS
    for g in range(SGU_GROUPS):
        w = jnp.where(row >= col, ws_ref[g], 0.0).astype(BF16)
        bias = bs_ref[:, g:g + 1]
        for c in range(tm // SGU_CHUNK):
            rs = slice(c * SGU_CHUNK, (c + 1) * SGU_CHUNK)
            cs = slice(g * gd, (g + 1) * gd)
            mix = _dot(w, vn[rs, cs]) + bias
            o_ref[rs, cs] = (u_ref[rs, cs].astype(F32) * mix
                             * sz_ref[rs, cs].astype(F32)).astype(o_ref.dtype)


def _sgu_call(p_gelu, p_silu, vg, ws, bs_t, tm):
    T = p_gelu.shape[0]
    return pl.pallas_call(
        functools.partial(_sgu_kernel, tm=tm),
        grid=(T // tm,),
        in_specs=[pl.BlockSpec((tm, SGU_WIDTH), lambda i: (i, GELU_U_B)),
                  pl.BlockSpec((tm, SGU_WIDTH), lambda i: (i, GELU_V_B)),
                  pl.BlockSpec((tm, SGU_WIDTH), lambda i: (i, SILU_Z_B)),
                  _resident((1, SGU_WIDTH)),
                  _resident((SGU_GROUPS, SGU_CHUNK, SGU_CHUNK)),
                  _resident((SGU_CHUNK, SGU_GROUPS))],
        out_specs=pl.BlockSpec((tm, SGU_WIDTH), lambda i: (i, 0)),
        out_shape=jax.ShapeDtypeStruct((T, SGU_WIDTH), BF16),
        compiler_params=_params("parallel"),
    )(p_gelu, p_gelu, p_silu, vg, ws, bs_t)


def _split_bf16(x):
    hi = x.astype(BF16)
    lo = (x - hi.astype(F32)).astype(BF16)
    return hi, lo


def _gla_kernel(qk_ref, v_ref, sz_ref, small_ref, wa_ref, ba_ref, og_ref, o_ref, st_ref, *, cpb):
    L = GLA_CHUNK

    @pl.when(pl.program_id(1) == 0)
    def _():
        st_ref[...] = jnp.zeros(st_ref.shape, F32)

    a_hi, a_lo = _split_bf16(small_ref[...])
    w_hi, w_lo = _split_bf16(wa_ref[...])
    xg = _dot(a_hi, w_hi) + _dot(a_lo, w_hi) + _dot(a_hi, w_lo) + ba_ref[...]
    log_a = (jnp.minimum(xg, 0.0) - jnp.log(1.0 + jnp.exp(-jnp.abs(xg)))) * (1.0 / GLA_TAU)
    row = lax.broadcasted_iota(jnp.int32, (L, L), 0)
    col = lax.broadcasted_iota(jnp.int32, (L, L), 1)
    causal = row >= col
    ones_tril = causal.astype(BF16)
    og = og_ref[...]
    states = [st_ref[hd] for hd in range(GLA_HEADS)]

    for c in range(cpb):
        rs = slice(c * L, (c + 1) * L)
        la_hi, la_lo = _split_bf16(log_a[rs, :])
        bcum = _dot(ones_tril, la_hi) + _dot(ones_tril, la_lo)
        for hd in range(GLA_HEADS):
            ks = slice(hd * GLA_DK, (hd + 1) * GLA_DK)
            k2 = slice(GLA_HEADS * GLA_DK + hd * GLA_DK, GLA_HEADS * GLA_DK + (hd + 1) * GLA_DK)
            vs = slice(hd * GLA_DV, (hd + 1) * GLA_DV)
            b = bcum[:, ks]
            b_mid = b[L // 2:L // 2 + 1, :]
            b_last = b[L - 1:L, :]
            q = qk_ref[rs, ks].astype(F32) * (GLA_DK ** -0.5)
            k = qk_ref[rs, k2].astype(F32)
            v = v_ref[rs, vs]
            q_t = (q * jnp.exp(b - b_mid)).astype(BF16)
            k_t = (k * jnp.exp(b_mid - b)).astype(BF16)
            attn = jnp.where(causal, _dot_nt(q_t, k_t), 0.0).astype(BF16)
            st = states[hd]
            o = _dot_nt((q * jnp.exp(b)).astype(BF16), st.astype(BF16)) + _dot(attn, v)
            k_s = (k * jnp.exp(b_last - b)).astype(BF16)
            states[hd] = jnp.exp(b_last) * st + _dot_tn(v, k_s)
            on = o * lax.rsqrt(jnp.mean(o * o, axis=-1, keepdims=True) + EPS) * og
            o_ref[rs, vs] = (on * sz_ref[rs, vs].astype(F32)).astype(o_ref.dtype)

    for hd in range(GLA_HEADS):
        st_ref[hd] = states[hd]


def _gla_call(p_lin, p_silu, small, wa_pad, ba, og, B, S, cpb):
    T = p_lin.shape[0]
    R = cpb * GLA_CHUNK
    ns = S // R
    return pl.pallas_call(
        functools.partial(_gla_kernel, cpb=cpb),
        grid=(B, ns),
        in_specs=[pl.BlockSpec((R, PROJ_TN), lambda b, c: (b * ns + c, LIN_QK_C)),
                  pl.BlockSpec((R, GLA_WIDTH), lambda b, c: (b * ns + c, LIN_V_C)),
                  pl.BlockSpec((R, GLA_WIDTH), lambda b, c: (b * ns + c, SILU_Z_C)),
                  pl.BlockSpec((R, SMALL_W), lambda b, c: (b * ns + c, 0)),
                  _resident((SMALL_W, GLA_HEADS * GLA_DK)),
                  _resident((1, GLA_HEADS * GLA_DK)),
                  _resident((1, GLA_DV))],
        out_specs=pl.BlockSpec((R, GLA_WIDTH), lambda b, c: (b * ns + c, 0)),
        out_shape=jax.ShapeDtypeStruct((T, GLA_WIDTH), BF16),
        scratch_shapes=[pltpu.VMEM((GLA_HEADS, GLA_DV, GLA_DK), F32)],
        compiler_params=_params("parallel", "arbitrary"),
    )(p_lin, p_lin, p_silu, small, wa_pad, ba, og)


def _merge_kernel(x_ref, ya_ref, yb_ref, yc_ref, ga_ref, gb_ref, gc_ref, wb_ref, wo_ref, o_ref):
    merged = ga_ref[...].astype(F32) * _dot(ya_ref[...], wb_ref[0])
    merged += gb_ref[...].astype(F32) * _dot(yb_ref[...], wb_ref[1])
    merged += gc_ref[...].astype(F32) * _dot(yc_ref[...], wb_ref[2])
    o_ref[...] = x_ref[...] + _dot(merged.astype(BF16), wo_ref[...])


def _merge_call(x2, ya, yb, yc, p_gate, wb, wo, tm, S, tq):
    T, D = x2.shape
    yspec = pl.BlockSpec((tm, BRANCH_WIDTH), lambda i: (i, 0))
    ya_spec = pl.BlockSpec((tm, BRANCH_WIDTH), lambda i: (_paired_block(i, S, tq, tm), 0))
    return pl.pallas_call(
        _merge_kernel,
        grid=(T // tm,),
        in_specs=[pl.BlockSpec((tm, D), lambda i: (i, 0)),
                  ya_spec, yspec, yspec,
                  pl.BlockSpec((tm, D), lambda i: (i, 0)),
                  pl.BlockSpec((tm, D), lambda i: (i, 1)),
                  pl.BlockSpec((tm, D), lambda i: (i, 2)),
                  _resident((N_BRANCH, BRANCH_WIDTH, D)),
                  _resident((D, D))],
        out_specs=pl.BlockSpec((tm, D), lambda i: (i, 0)),
        out_shape=jax.ShapeDtypeStruct((T, D), F32),
        compiler_params=_params("parallel"),
    )(x2, ya, yb, yc, p_gate, p_gate, p_gate, wb, wo)


def _split_w_in(w_in):
    sizes = (MLA_Q_RANK, MLA_KV_RANK, MLA_ROPE, MLA_WIDTH, SGU_WIDTH, SGU_WIDTH, SGU_WIDTH,
             GLA_HEADS * GLA_DK, GLA_HEADS * GLA_DK, GLA_WIDTH, GLA_GATE_RANK, GLA_WIDTH)
    parts, off = [], 0
    for s in sizes:
        parts.append(w_in[:, off:off + s])
        off += s
    parts.append(w_in[:, off:])
    (c_q, c_kv, k_rope, z_a, u_b, v_b, z_b, q_c, k_c, v_c, a_r, z_c, gates) = parts
    w_lin = jnp.concatenate([c_q, c_kv, q_c, k_c, v_c], axis=1).astype(BF16)
    w_silu = jnp.concatenate([z_a, z_b, z_c], axis=1).astype(BF16)
    w_gelu = jnp.concatenate([u_b, v_b], axis=1).astype(BF16)
    w_gate = gates.astype(BF16)
    pad = jnp.zeros((w_in.shape[0], SMALL_W - MLA_ROPE - GLA_GATE_RANK), w_in.dtype)
    w_small = jnp.concatenate([k_rope, a_r, pad], axis=1).astype(BF16)
    return w_lin, w_silu, w_gelu, w_gate, w_small


def _pad_heads(w, per_head, lo, hi, width):
    r = w.shape[0]
    w = w.reshape(r, MLA_HEADS, per_head)[:, :, lo:hi]
    w = jnp.pad(w, ((0, 0), (0, 0), (0, width - (hi - lo))))
    return w.reshape(r, MLA_HEADS * width)


def _rope_tables(positions):
    half = MLA_ROPE // 2
    inv_freq = 1.0 / (ROPE_THETA ** (jnp.arange(0, MLA_ROPE, 2, dtype=F32) / MLA_ROPE))
    ang = positions.astype(F32).reshape(-1, 1) * inv_freq
    cos, sin = jnp.cos(ang), jnp.sin(ang)
    z = jnp.zeros((ang.shape[0], LANE - 2 * half), F32)
    return (jnp.concatenate([cos, cos, z], axis=1), jnp.concatenate([-sin, sin, z], axis=1))


def _pick_tile(n, want):
    t = min(n, want)
    while n % t:
        t //= 2
    return t


def kernel(x, positions, norm_g, w_in, mla_cq_norm, mla_ckv_norm, mla_w_uq, mla_w_ukv,
           mla_q_norm, mla_k_norm, sgu_v_norm, sgu_w_s, sgu_b_s, gla_w_a2, gla_b_a,
           gla_o_norm, w_branch, w_out):
    B, S, D = x.shape
    T = B * S
    depth = w_in.shape[0]
    tm_proj = _pick_tile(T, 2048)
    tm_row = _pick_tile(T, 512)
    tm_merge = _pick_tile(T, 256)
    tq = _pick_tile(S, 512)
    cpb = _pick_tile(S // GLA_CHUNK, 4)

    cosf, sins = _rope_tables(positions)
    x2 = x.reshape(T, D)
    for l in range(depth):
        w_lin, w_silu, w_gelu, w_gate, w_small = _split_w_in(w_in[l])
        wuq = _pad_heads(mla_w_uq[l], MLA_QK, 0, MLA_QK, MLA_QK_PAD).astype(BF16)
        wuk = _pad_heads(mla_w_ukv[l], MLA_NOPE + MLA_V, 0, MLA_NOPE, MLA_NOPE).astype(BF16)
        wuv = _pad_heads(mla_w_ukv[l], MLA_NOPE + MLA_V, MLA_NOPE, MLA_NOPE + MLA_V, MLA_V).astype(BF16)
        qg = jnp.pad(mla_q_norm[l], (0, MLA_QK_PAD - MLA_QK)).reshape(1, MLA_QK_PAD)
        kg = jnp.pad(mla_k_norm[l], (0, MLA_QK_PAD - MLA_QK)).reshape(1, MLA_QK_PAD)
        wa_pad = jnp.zeros((SMALL_W, GLA_HEADS * GLA_DK), F32).at[
            MLA_ROPE:MLA_ROPE + GLA_GATE_RANK].set(gla_w_a2[l])

        h, small = _norm_call(x2, norm_g[l].reshape(1, D), w_small, tm_row)
        p_lin = _inproj_call(h, w_lin, _identity, tm_proj)
        p_silu = _inproj_call(h, w_silu, _silu, tm_proj)
        p_gelu = _inproj_call(h, w_gelu, _gelu_tanh, tm_proj)
        p_gate = _inproj_call(h, w_gate, _sigmoid, tm_proj)
        q, k, v = _mla_prep_call(p_lin, small, cosf, sins,
                                 mla_cq_norm[l].reshape(1, -1), mla_ckv_norm[l].reshape(1, -1),
                                 wuq, wuk, wuv, qg, kg, tm_row)
        ya = _attn_call(q, k, v, p_silu, B, S, tq)
        yb = _sgu_call(p_gelu, p_silu, sgu_v_norm[l].reshape(1, -1), sgu_w_s[l], sgu_b_s[l].T, tm_row)
        yc = _gla_call(p_lin, p_silu, small, wa_pad, gla_b_a[l].reshape(1, -1),
                       gla_o_norm[l].reshape(1, -1), B, S, cpb)
        x2 = _merge_call(x2, ya, yb, yc, p_gate, w_branch[l].astype(BF16), w_out[l].astype(BF16), tm_merge, S, tq)
    return x2.reshape(B, S, D)
```

```python
import functools
import math

import jax
import jax.numpy as jnp
from jax import lax
from jax.experimental import pallas as pl
from jax.experimental.pallas import tpu as pltpu

F32 = jnp.float32
BF16 = jnp.bfloat16

MLA_HEADS = 8
MLA_NOPE = 128
MLA_ROPE = 64
MLA_QK = MLA_NOPE + MLA_ROPE
MLA_V = 128
MLA_Q_RANK = 512
MLA_KV_RANK = 512
MLA_WIDTH = MLA_HEADS * MLA_V
MLA_QK_PAD = 256
ROPE_THETA = 10000.0
SGU_GROUPS = 8
SGU_CHUNK = 128
SGU_WIDTH = 1024
GLA_HEADS = 4
GLA_DK = 128
GLA_DV = 256
GLA_GATE_RANK = 16
GLA_TAU = 16.0
GLA_CHUNK = 128
GLA_WIDTH = GLA_HEADS * GLA_DV
N_BRANCH = 3
BRANCH_WIDTH = 1024
EPS = 1e-6
LANE = 128
LOG2_E = math.log2(math.e)
MASK_VALUE = -1e30
SMALL_W = LANE

PROJ_TN = 1024
LIN_CQKV, LIN_QK_C, LIN_V_C = range(3)
SILU_Z_A, SILU_Z_B, SILU_Z_C = range(3)
GELU_U_B, GELU_V_B = range(2)

VMEM_LIMIT = 56 * 1024 * 1024


def _params(*sem):
    return pltpu.CompilerParams(dimension_semantics=sem, vmem_limit_bytes=VMEM_LIMIT)


def _dot(a, b):
    return jnp.dot(a, b, preferred_element_type=F32)


def _dot_nt(a, b):
    return lax.dot_general(a, b, (((1,), (1,)), ((), ())), preferred_element_type=F32)


def _dot_tn(a, b):
    return lax.dot_general(a, b, (((0,), (0,)), ((), ())), preferred_element_type=F32)


def _sigmoid(x):
    return 0.5 * jnp.tanh(0.5 * x) + 0.5


def _gelu_tanh(x):
    c = math.sqrt(2.0 / math.pi)
    hx = 0.5 * x
    return hx * jnp.tanh(x * (c + (c * 0.044715) * (x * x))) + hx


def _silu(x):
    hx = 0.5 * x
    return hx * jnp.tanh(hx) + hx


def _identity(x):
    return x


def _resident(shape, layer=None):
    nd = len(shape)
    if layer is None:
        return pl.BlockSpec(shape, lambda *_: (0,) * nd, pipeline_mode=pl.Buffered(1))
    return pl.BlockSpec((None,) + tuple(shape), lambda *_: (layer,) + (0,) * nd,
                        pipeline_mode=pl.Buffered(1))


def _norm_kernel(x_ref, g_ref, ws_ref, h_ref, small_ref):
    x = x_ref[...]
    rstd = lax.rsqrt(jnp.mean(x * x, axis=-1, keepdims=True) + EPS)
    h = (x * rstd * g_ref[...]).astype(BF16)
    h_ref[...] = h
    small_ref[...] = _dot(h, ws_ref[...])


def _norm_call(x2, g, w_small, layer, tm):
    T, D = x2.shape
    return pl.pallas_call(
        _norm_kernel,
        grid=(T // tm,),
        in_specs=[pl.BlockSpec((tm, D), lambda i: (i, 0)),
                  _resident((1, D)),
                  _resident((D, SMALL_W), layer)],
        out_specs=[pl.BlockSpec((tm, D), lambda i: (i, 0)),
                   pl.BlockSpec((tm, SMALL_W), lambda i: (i, 0))],
        out_shape=[jax.ShapeDtypeStruct((T, D), BF16),
                   jax.ShapeDtypeStruct((T, SMALL_W), F32)],
        compiler_params=_params("parallel"),
    )(x2, g, w_small)


def _inproj_kernel(h_ref, w_ref, o_ref, *, act):
    o_ref[...] = act(_dot(h_ref[...], w_ref[...])).astype(o_ref.dtype)


def _inproj_call(h, w, layer, act, tm):
    T, D = h.shape
    N = w.shape[2]
    return pl.pallas_call(
        functools.partial(_inproj_kernel, act=act),
        grid=(T // tm, N // PROJ_TN),
        in_specs=[pl.BlockSpec((tm, D), lambda i, j: (i, 0)),
                  pl.BlockSpec((None, D, PROJ_TN), lambda i, j: (layer, 0, j))],
        out_specs=pl.BlockSpec((tm, PROJ_TN), lambda i, j: (i, j)),
        out_shape=jax.ShapeDtypeStruct((T, N), BF16),
        compiler_params=_params("parallel", "arbitrary"),
    )(h, w)


def _rope128(r, cosf, sins):
    lane = lax.broadcasted_iota(jnp.int32, r.shape, 1)
    partner = jnp.where(lane < MLA_ROPE // 2,
                        pltpu.roll(r, LANE - MLA_ROPE // 2, axis=1),
                        pltpu.roll(r, MLA_ROPE // 2, axis=1))
    return r * cosf + partner * sins


def _mla_prep_kernel(cq_ref, ckv_ref, small_ref, cos_ref, sin_ref,
                     cqg_ref, ckvg_ref, wuq_ref, wuk_ref, wuv_ref, qg_ref, kg_ref,
                     q_ref, k_ref, v_ref):
    cosf = cos_ref[...]
    sins = sin_ref[...]
    scale = MLA_QK ** -0.5 * LOG2_E

    cq = cq_ref[...].astype(F32)
    nq = cq * lax.rsqrt(jnp.mean(cq * cq, axis=-1, keepdims=True) + EPS) * cqg_ref[...]
    q_all = _dot(nq.astype(BF16), wuq_ref[...])

    ckv = ckv_ref[...].astype(F32)
    nkv = ckv * lax.rsqrt(jnp.mean(ckv * ckv, axis=-1, keepdims=True) + EPS) * ckvg_ref[...]
    nkv = nkv.astype(BF16)
    k_all = _dot(nkv, wuk_ref[...])
    v_ref[...] = _dot(nkv, wuv_ref[...]).astype(v_ref.dtype)

    qg = qg_ref[...]
    kg = kg_ref[...]
    lane = lax.broadcasted_iota(jnp.int32, (1, LANE), 1)
    kr = jnp.where(lane < MLA_ROPE, small_ref[...], 0.0)
    kr_ss = jnp.sum(kr * kr, axis=-1, keepdims=True)
    kr_rot = _rope128(kr * kg[:, MLA_NOPE:], cosf, sins)

    for hd in range(MLA_HEADS):
        qb = q_all[:, hd * MLA_QK_PAD:(hd + 1) * MLA_QK_PAD]
        rq = lax.rsqrt(jnp.sum(qb * qb, axis=-1, keepdims=True) * (1.0 / MLA_QK) + EPS) * scale
        qn = qb * rq * qg
        q_ref[:, hd * MLA_QK_PAD:hd * MLA_QK_PAD + MLA_NOPE] = qn[:, :MLA_NOPE].astype(q_ref.dtype)
        q_ref[:, hd * MLA_QK_PAD + MLA_NOPE:(hd + 1) * MLA_QK_PAD] = _rope128(
            qn[:, MLA_NOPE:], cosf, sins).astype(q_ref.dtype)

        kb = k_all[:, hd * MLA_NOPE:(hd + 1) * MLA_NOPE]
        rk = lax.rsqrt((jnp.sum(kb * kb, axis=-1, keepdims=True) + kr_ss) * (1.0 / MLA_QK) + EPS)
        k_ref[:, hd * MLA_QK_PAD:hd * MLA_QK_PAD + MLA_NOPE] = (
            kb * rk * kg[:, :MLA_NOPE]).astype(k_ref.dtype)
        k_ref[:, hd * MLA_QK_PAD + MLA_NOPE:(hd + 1) * MLA_QK_PAD] = (kr_rot * rk).astype(k_ref.dtype)


def _mla_prep_call(proj, small, cosf, sins, cqg, ckvg, wuq, wuk, wuv, qg, kg, tm):
    T = proj.shape[0]
    HQ = MLA_HEADS * MLA_QK_PAD
    return pl.pallas_call(
        _mla_prep_kernel,
        grid=(T // tm,),
        in_specs=[pl.BlockSpec((tm, MLA_Q_RANK), lambda i: (i, 0)),
                  pl.BlockSpec((tm, MLA_KV_RANK), lambda i: (i, 1)),
                  pl.BlockSpec((tm, SMALL_W), lambda i: (i, 0)),
                  pl.BlockSpec((tm, LANE), lambda i: (i, 0)),
                  pl.BlockSpec((tm, LANE), lambda i: (i, 0)),
                  _resident((1, MLA_Q_RANK)),
                  _resident((1, MLA_KV_RANK)),
                  _resident((MLA_Q_RANK, HQ)),
                  _resident((MLA_KV_RANK, MLA_HEADS * MLA_NOPE)),
                  _resident((MLA_KV_RANK, MLA_WIDTH)),
                  _resident((1, MLA_QK_PAD)),
                  _resident((1, MLA_QK_PAD))],
        out_specs=[pl.BlockSpec((tm, HQ), lambda i: (i, 0)),
                   pl.BlockSpec((tm, HQ), lambda i: (i, 0)),
                   pl.BlockSpec((tm, MLA_WIDTH), lambda i: (i, 0))],
        out_shape=[jax.ShapeDtypeStruct((T, HQ), BF16),
                   jax.ShapeDtypeStruct((T, HQ), BF16),
                   jax.ShapeDtypeStruct((T, MLA_WIDTH), BF16)],
        compiler_params=_params("parallel"),
    )(proj, proj, small, cosf, sins, cqg, ckvg, wuq, wuk, wuv, qg, kg)


def _attn_kernel(qlo_ref, qhi_ref, k_ref, v_ref, szlo_ref, szhi_ref, o_ref,
                 q2_ref, s_ref, mf_ref, lf_ref, pv_ref, *, tq, nq, n_steps):
    g = pl.program_id(0)
    half = nq // 2
    nt = nq + 1
    nl = tq // LANE
    i1 = jnp.minimum(g, n_steps - 1) % half
    i2 = jnp.maximum(g - 1, 0) % half

    @pl.when(g == 0)
    def _():
        for t in range(nt):
            s_ref[t] = jnp.zeros((tq, tq), F32)
            mf_ref[t] = jnp.zeros((tq, LANE), F32)

    q2_ref[0] = qlo_ref[...]
    q2_ref[1] = qhi_ref[...]

    def lane_fold(x, op):
        r = x[:, :LANE]
        for c in range(1, nl):
            r = op(r, x[:, c * LANE:(c + 1) * LANE])
        return r

    def is_lo(p, i):
        return True if p == 0 else (False if p >= half else p <= i)

    def pick(p, i, lo, hi):
        c = is_lo(p, i)
        if c is True:
            return lo
        if c is False:
            return hi
        return jnp.where(c, lo, hi)

    def kv_rows(p, i):
        if p == 0:
            j = i
        elif p == nq:
            j = nq - 1 - i
        else:
            j = pick(p, i, p - 1, p - i - 1)
        return pl.ds(pl.multiple_of(j * tq, tq), tq)

    def combine(ref, i, op, init):
        lo = jnp.full((tq, LANE), init, F32)
        hi = jnp.full((tq, LANE), init, F32)
        for p in range(nt):
            c = is_lo(p, i)
            x = ref[p]
            if c is True:
                lo = op(lo, x)
            elif c is False:
                hi = op(hi, x)
            else:
                lo = op(lo, jnp.where(c, x, init))
                hi = op(hi, jnp.where(c, init, x))
        return lo, hi

    m_lo, m_hi = combine(mf_ref, i2, jnp.maximum, MASK_VALUE)
    m_lo = jnp.max(m_lo, axis=-1, keepdims=True)
    m_hi = jnp.max(m_hi, axis=-1, keepdims=True)

    causal = (lax.broadcasted_iota(jnp.int32, (tq, tq), 0)
              >= lax.broadcasted_iota(jnp.int32, (tq, tq), 1))
    order_zero = jnp.zeros((tq, 1), F32)
    for p in range(nt):
        e = jnp.exp2(s_ref[p] - (pick(p, i2, m_lo, m_hi) + order_zero))
        lf_ref[p] = lane_fold(e, jnp.add)
        pv_ref[p] = _dot(e.astype(BF16), v_ref[kv_rows(p, i2), :])
        s = _dot_nt(q2_ref[pick(p, i1, 0, 1)], k_ref[kv_rows(p, i1), :])
        if p == 0 or p == nq:
            s = jnp.where(causal, s, MASK_VALUE)
        s_ref[p] = s
        mf = lane_fold(s, jnp.maximum)
        mf_ref[p] = mf
        bits = lax.bitcast_convert_type(mf[:, :1], jnp.uint32)
        order_zero = lax.bitcast_convert_type((bits >> 16) >> 16, F32)

    l_lo, l_hi = combine(lf_ref, i2, jnp.add, 0.0)
    a_lo, a_hi = combine(pv_ref, i2, jnp.add, 0.0)
    l_lo = jnp.sum(l_lo, axis=-1, keepdims=True)
    l_hi = jnp.sum(l_hi, axis=-1, keepdims=True)
    o_ref[:tq, :] = (a_lo / l_lo * szlo_ref[...].astype(F32)).astype(o_ref.dtype)
    o_ref[tq:, :] = (a_hi / l_hi * szhi_ref[...].astype(F32)).astype(o_ref.dtype)


def _attn_call(q, k, v, p_silu, B, S, tq):
    T = q.shape[0]
    nq = S // tq
    assert nq % 2 == 0 and MLA_V == LANE
    half = nq // 2
    n_steps = B * MLA_HEADS * half
    sz_col0 = SILU_Z_A * PROJ_TN // MLA_V

    def bhi(g):
        bh, i = g // half, g % half
        return bh // MLA_HEADS, bh % MLA_HEADS, i

    def first(g):
        return bhi(jnp.minimum(g, n_steps - 1))

    def second(g):
        return bhi(jnp.maximum(g - 1, 0))

    def lo_blk(f, col0=0):
        def index(g):
            b, h, i = f(g)
            return b * nq + i, col0 + h
        return index

    def hi_blk(f, col0=0):
        def index(g):
            b, h, i = f(g)
            return b * nq + nq - 1 - i, col0 + h
        return index

    def seq_blk(f):
        def index(g):
            b, h, _ = f(g)
            return b, h
        return index

    def out_blk(g):
        b, h, i = second(g)
        return b * half + i, h

    nt = nq + 1
    return pl.pallas_call(
        functools.partial(_attn_kernel, tq=tq, nq=nq, n_steps=n_steps),
        grid=(n_steps + 1,),
        in_specs=[pl.BlockSpec((tq, MLA_QK_PAD), lo_blk(first)),
                  pl.BlockSpec((tq, MLA_QK_PAD), hi_blk(first)),
                  pl.BlockSpec((S, MLA_QK_PAD), seq_blk(first)),
                  pl.BlockSpec((S, MLA_V), seq_blk(second)),
                  pl.BlockSpec((tq, MLA_V), lo_blk(second, sz_col0)),
                  pl.BlockSpec((tq, MLA_V), hi_blk(second, sz_col0))],
        out_specs=pl.BlockSpec((2 * tq, MLA_V), out_blk),
        out_shape=jax.ShapeDtypeStruct((T, MLA_WIDTH), BF16),
        scratch_shapes=[pltpu.VMEM((2, tq, MLA_QK_PAD), BF16),
                        pltpu.VMEM((nt, tq, tq), F32),
                        pltpu.VMEM((nt, tq, LANE), F32),
                        pltpu.VMEM((nt, tq, LANE), F32),
                        pltpu.VMEM((nt, tq, MLA_V), F32)],
        compiler_params=_params("arbitrary"),
    )(q, q, k, v, p_silu, p_silu)


def _paired_block(r, S, tq, tm):
    per_seq, per_tile, nq = S // tm, tq // tm, S // tq
    b, w = r // per_seq, r % per_seq
    u, sub = w // per_tile, w % per_tile
    pos = jnp.where(u < nq // 2, 2 * u, 2 * (nq - 1 - u) + 1)
    return b * per_seq + pos * per_tile + sub


def _sgu_kernel(u_ref, v_ref, sz_ref, vg_ref, ws_ref, bs_ref, o_ref, *, tm):
    v = v_ref[...].astype(F32)
    vn = (v * lax.rsqrt(jnp.mean(v * v, axis=-1, keepdims=True) + EPS) * vg_ref[...]).astype(BF16)
    row = lax.broadcasted_iota(jnp.int32, (SGU_CHUNK, SGU_CHUNK), 0)
    col = lax.broadcasted_iota(jnp.int32, (SGU_CHUNK, SGU_CHUNK), 1)
    gd = SGU_WIDTH // SGU_GROUPS
    for g in range(SGU_GROUPS):
        w = jnp.where(row >= col, ws_ref[g], 0.0).astype(BF16)
        bias = bs_ref[:, g:g + 1]
        for c in range(tm // SGU_CHUNK):
            rs = slice(c * SGU_CHUNK, (c + 1) * SGU_CHUNK)
            cs = slice(g * gd, (g + 1) * gd)
            mix = _dot(w, vn[rs, cs]) + bias
            o_ref[rs, cs] = (u_ref[rs, cs].astype(F32) * mix
                             * sz_ref[rs, cs].astype(F32)).astype(o_ref.dtype)


def _sgu_call(p_gelu, p_silu, vg, ws, bs_t, tm):
    T = p_gelu.shape[0]
    return pl.pallas_call(
        functools.partial(_sgu_kernel, tm=tm),
        grid=(T // tm,),
        in_specs=[pl.BlockSpec((tm, SGU_WIDTH), lambda i: (i, GELU_U_B)),
                  pl.BlockSpec((tm, SGU_WIDTH), lambda i: (i, GELU_V_B)),
                  pl.BlockSpec((tm, SGU_WIDTH), lambda i: (i, SILU_Z_B)),
                  _resident((1, SGU_WIDTH)),
                  _resident((SGU_GROUPS, SGU_CHUNK, SGU_CHUNK)),
                  _resident((SGU_CHUNK, SGU_GROUPS))],
        out_specs=pl.BlockSpec((tm, SGU_WIDTH), lambda i: (i, 0)),
        out_shape=jax.ShapeDtypeStruct((T, SGU_WIDTH), BF16),
        compiler_params=_params("parallel"),
    )(p_gelu, p_gelu, p_silu, vg, ws, bs_t)


def _split_bf16(x):
    hi = x.astype(BF16)
    lo = (x - hi.astype(F32)).astype(BF16)
    return hi, lo


def _gla_kernel(qk_ref, v_ref, sz_ref, small_ref, wa_ref, ba_ref, og_ref, o_ref, st_ref, *, cpb):
    L = GLA_CHUNK

    @pl.when(pl.program_id(1) == 0)
    def _():
        st_ref[...] = jnp.zeros(st_ref.shape, F32)

    a_hi, a_lo = _split_bf16(small_ref[...])
    w_hi, w_lo = _split_bf16(wa_ref[...])
    xg = _dot(a_hi, w_hi) + _dot(a_lo, w_hi) + _dot(a_hi, w_lo) + ba_ref[...]
    log_a = (jnp.minimum(xg, 0.0) - jnp.log(1.0 + jnp.exp(-jnp.abs(xg)))) * (1.0 / GLA_TAU)
    row = lax.broadcasted_iota(jnp.int32, (L, L), 0)
    col = lax.broadcasted_iota(jnp.int32, (L, L), 1)
    causal = row >= col
    ones_tril = causal.astype(BF16)
    og = og_ref[...]
    states = [st_ref[hd] for hd in range(GLA_HEADS)]

    for c in range(cpb):
        rs = slice(c * L, (c + 1) * L)
        la_hi, la_lo = _split_bf16(log_a[rs, :])
        bcum = _dot(ones_tril, la_hi) + _dot(ones_tril, la_lo)
        for hd in range(GLA_HEADS):
            ks = slice(hd * GLA_DK, (hd + 1) * GLA_DK)
            k2 = slice(GLA_HEADS * GLA_DK + hd * GLA_DK, GLA_HEADS * GLA_DK + (hd + 1) * GLA_DK)
            vs = slice(hd * GLA_DV, (hd + 1) * GLA_DV)
            b = bcum[:, ks]
            b_mid = b[L // 2:L // 2 + 1, :]
            b_last = b[L - 1:L, :]
            q = qk_ref[rs, ks].astype(F32) * (GLA_DK ** -0.5)
            k = qk_ref[rs, k2].astype(F32)
            v = v_ref[rs, vs]
            q_t = (q * jnp.exp(b - b_mid)).astype(BF16)
            k_t = (k * jnp.exp(b_mid - b)).astype(BF16)
            attn = jnp.where(causal, _dot_nt(q_t, k_t), 0.0).astype(BF16)
            st = states[hd]
            o = _dot_nt((q * jnp.exp(b)).astype(BF16), st.astype(BF16)) + _dot(attn, v)
            k_s = (k * jnp.exp(b_last - b)).astype(BF16)
            states[hd] = jnp.exp(b_last) * st + _dot_tn(v, k_s)
            on = o * lax.rsqrt(jnp.mean(o * o, axis=-1, keepdims=True) + EPS) * og
            o_ref[rs, vs] = (on * sz_ref[rs, vs].astype(F32)).astype(o_ref.dtype)

    for hd in range(GLA_HEADS):
        st_ref[hd] = states[hd]


def _gla_call(p_lin, p_silu, small, wa_pad, ba, og, B, S, cpb):
    T = p_lin.shape[0]
    R = cpb * GLA_CHUNK
    ns = S // R
    return pl.pallas_call(
        functools.partial(_gla_kernel, cpb=cpb),
        grid=(B, ns),
        in_specs=[pl.BlockSpec((R, PROJ_TN), lambda b, c: (b * ns + c, LIN_QK_C)),
                  pl.BlockSpec((R, GLA_WIDTH), lambda b, c: (b * ns + c, LIN_V_C)),
                  pl.BlockSpec((R, GLA_WIDTH), lambda b, c: (b * ns + c, SILU_Z_C)),
                  pl.BlockSpec((R, SMALL_W), lambda b, c: (b * ns + c, 0)),
                  _resident((SMALL_W, GLA_HEADS * GLA_DK)),
                  _resident((1, GLA_HEADS * GLA_DK)),
                  _resident((1, GLA_DV))],
        out_specs=pl.BlockSpec((R, GLA_WIDTH), lambda b, c: (b * ns + c, 0)),
        out_shape=jax.ShapeDtypeStruct((T, GLA_WIDTH), BF16),
        scratch_shapes=[pltpu.VMEM((GLA_HEADS, GLA_DV, GLA_DK), F32)],
        compiler_params=_params("parallel", "arbitrary"),
    )(p_lin, p_lin, p_silu, small, wa_pad, ba, og)


def _merge_kernel(x_ref, ya_ref, yb_ref, yc_ref, ga_ref, gb_ref, gc_ref, wb_ref, wo_ref, o_ref):
    merged = ga_ref[...].astype(F32) * _dot(ya_ref[...], wb_ref[0])
    merged += gb_ref[...].astype(F32) * _dot(yb_ref[...], wb_ref[1])
    merged += gc_ref[...].astype(F32) * _dot(yc_ref[...], wb_ref[2])
    o_ref[...] = x_ref[...] + _dot(merged.astype(BF16), wo_ref[...])


def _merge_call(x2, ya, yb, yc, p_gate, wb, wo, layer, tm, S, tq):
    T, D = x2.shape
    yspec = pl.BlockSpec((tm, BRANCH_WIDTH), lambda i: (i, 0))
    ya_spec = pl.BlockSpec((tm, BRANCH_WIDTH), lambda i: (_paired_block(i, S, tq, tm), 0))
    return pl.pallas_call(
        _merge_kernel,
        grid=(T // tm,),
        in_specs=[pl.BlockSpec((tm, D), lambda i: (i, 0)),
                  ya_spec, yspec, yspec,
                  pl.BlockSpec((tm, D), lambda i: (i, 0)),
                  pl.BlockSpec((tm, D), lambda i: (i, 1)),
                  pl.BlockSpec((tm, D), lambda i: (i, 2)),
                  _resident((N_BRANCH, BRANCH_WIDTH, D), layer),
                  _resident((D, D), layer)],
        out_specs=pl.BlockSpec((tm, D), lambda i: (i, 0)),
        out_shape=jax.ShapeDtypeStruct((T, D), F32),
        compiler_params=_params("parallel"),
    )(x2, ya, yb, yc, p_gate, p_gate, p_gate, wb, wo)


def _split_w_in(w_in):
    sizes = (MLA_Q_RANK, MLA_KV_RANK, MLA_ROPE, MLA_WIDTH, SGU_WIDTH, SGU_WIDTH, SGU_WIDTH,
             GLA_HEADS * GLA_DK, GLA_HEADS * GLA_DK, GLA_WIDTH, GLA_GATE_RANK, GLA_WIDTH)
    parts, off = [], 0
    for s in sizes:
        parts.append(w_in[..., off:off + s])
        off += s
    parts.append(w_in[..., off:])
    (c_q, c_kv, k_rope, z_a, u_b, v_b, z_b, q_c, k_c, v_c, a_r, z_c, gates) = parts
    w_lin = jnp.concatenate([c_q, c_kv, q_c, k_c, v_c], axis=-1).astype(BF16)
    w_silu = jnp.concatenate([z_a, z_b, z_c], axis=-1).astype(BF16)
    w_gelu = jnp.concatenate([u_b, v_b], axis=-1).astype(BF16)
    w_gate = gates.astype(BF16)
    pad = jnp.zeros(w_in.shape[:-1] + (SMALL_W - MLA_ROPE - GLA_GATE_RANK,), w_in.dtype)
    w_small = jnp.concatenate([k_rope, a_r, pad], axis=-1).astype(BF16)
    return w_lin, w_silu, w_gelu, w_gate, w_small


def _pad_heads(w, per_head, lo, hi, width):
    r = w.shape[0]
    w = w.reshape(r, MLA_HEADS, per_head)[:, :, lo:hi]
    w = jnp.pad(w, ((0, 0), (0, 0), (0, width - (hi - lo))))
    return w.reshape(r, MLA_HEADS * width)


def _rope_tables(positions):
    half = MLA_ROPE // 2
    inv_freq = 1.0 / (ROPE_THETA ** (jnp.arange(0, MLA_ROPE, 2, dtype=F32) / MLA_ROPE))
    ang = positions.astype(F32).reshape(-1, 1) * inv_freq
    cos, sin = jnp.cos(ang), jnp.sin(ang)
    z = jnp.zeros((ang.shape[0], LANE - 2 * half), F32)
    return (jnp.concatenate([cos, cos, z], axis=1), jnp.concatenate([-sin, sin, z], axis=1))


def _pick_tile(n, want):
    t = min(n, want)
    while n % t:
        t //= 2
    return t


def kernel(x, positions, norm_g, w_in, mla_cq_norm, mla_ckv_norm, mla_w_uq, mla_w_ukv,
           mla_q_norm, mla_k_norm, sgu_v_norm, sgu_w_s, sgu_b_s, gla_w_a2, gla_b_a,
           gla_o_norm, w_branch, w_out):
    B, S, D = x.shape
    T = B * S
    depth = w_in.shape[0]
    tm_proj = _pick_tile(T, 2048)
    tm_row = _pick_tile(T, 512)
    tm_merge = _pick_tile(T, 256)
    tq = _pick_tile(S, 512)
    cpb = _pick_tile(S // GLA_CHUNK, 8)

    cosf, sins = _rope_tables(positions)
    x2 = x.reshape(T, D)
    w_lin, w_silu, w_gelu, w_gate, w_small = _split_w_in(w_in)
    wb, wo = w_branch.astype(BF16), w_out.astype(BF16)
    for l in range(depth):
        wuq = _pad_heads(mla_w_uq[l], MLA_QK, 0, MLA_QK, MLA_QK_PAD).astype(BF16)
        wuk = _pad_heads(mla_w_ukv[l], MLA_NOPE + MLA_V, 0, MLA_NOPE, MLA_NOPE).astype(BF16)
        wuv = _pad_heads(mla_w_ukv[l], MLA_NOPE + MLA_V, MLA_NOPE, MLA_NOPE + MLA_V, MLA_V).astype(BF16)
        qg = jnp.pad(mla_q_norm[l], (0, MLA_QK_PAD - MLA_QK)).reshape(1, MLA_QK_PAD)
        kg = jnp.pad(mla_k_norm[l], (0, MLA_QK_PAD - MLA_QK)).reshape(1, MLA_QK_PAD)
        wa_pad = jnp.zeros((SMALL_W, GLA_HEADS * GLA_DK), F32).at[
            MLA_ROPE:MLA_ROPE + GLA_GATE_RANK].set(gla_w_a2[l])

        h, small = _norm_call(x2, norm_g[l].reshape(1, D), w_small, l, tm_row)
        p_lin = _inproj_call(h, w_lin, l, _identity, tm_proj)
        p_silu = _inproj_call(h, w_silu, l, _silu, tm_proj)
        p_gelu = _inproj_call(h, w_gelu, l, _gelu_tanh, tm_proj)
        p_gate = _inproj_call(h, w_gate, l, _sigmoid, tm_proj)
        q, k, v = _mla_prep_call(p_lin, small, cosf, sins,
                                 mla_cq_norm[l].reshape(1, -1), mla_ckv_norm[l].reshape(1, -1),
                                 wuq, wuk, wuv, qg, kg, tm_row)
        ya = _attn_call(q, k, v, p_silu, B, S, tq)
        yb = _sgu_call(p_gelu, p_silu, sgu_v_norm[l].reshape(1, -1), sgu_w_s[l], sgu_b_s[l].T, tm_row)
        yc = _gla_call(p_lin, p_silu, small, wa_pad, gla_b_a[l].reshape(1, -1),
                       gla_o_norm[l].reshape(1, -1), B, S, cpb)
        x2 = _merge_call(x2, ya, yb, yc, p_gate, wb, wo, l, tm_merge, S, tq)
    return x2.reshape(B, S, D)
```

```python
import functools
import math

import jax
import jax.numpy as jnp
from jax import lax
from jax.experimental import pallas as pl
from jax.experimental.pallas import tpu as pltpu

F32 = jnp.float32
BF16 = jnp.bfloat16

MLA_HEADS = 8
MLA_NOPE = 128
MLA_ROPE = 64
MLA_QK = MLA_NOPE + MLA_ROPE
MLA_V = 128
MLA_Q_RANK = 512
MLA_KV_RANK = 512
MLA_WIDTH = MLA_HEADS * MLA_V
MLA_QK_PAD = 256
ROPE_THETA = 10000.0
SGU_GROUPS = 8
SGU_CHUNK = 128
SGU_WIDTH = 1024
GLA_HEADS = 4
GLA_DK = 128
GLA_DV = 256
GLA_GATE_RANK = 16
GLA_TAU = 16.0
GLA_CHUNK = 128
GLA_WIDTH = GLA_HEADS * GLA_DV
D_MODEL = 2048
N_BRANCH = 3
BRANCH_WIDTH = 1024
EPS = 1e-6
LANE = 128
LOG2_E = math.log2(math.e)
MASK_VALUE = -1e30
SMALL_W = LANE

PROJ_TN = 1024
LIN_CQKV, LIN_QK_C, LIN_V_C = range(3)
SILU_Z_A, SILU_Z_B, SILU_Z_C = range(3)
GELU_U_B, GELU_V_B = range(2)

VMEM_LIMIT = 56 * 1024 * 1024


def _params(*sem):
    return pltpu.CompilerParams(dimension_semantics=sem, vmem_limit_bytes=VMEM_LIMIT)


def _dot(a, b):
    return jnp.dot(a, b, preferred_element_type=F32)


def _dot_nt(a, b):
    return lax.dot_general(a, b, (((1,), (1,)), ((), ())), preferred_element_type=F32)


def _dot_tn(a, b):
    return lax.dot_general(a, b, (((0,), (0,)), ((), ())), preferred_element_type=F32)


def _sigmoid(x):
    return 0.5 * jnp.tanh(0.5 * x) + 0.5


def _gelu_tanh(x):
    c = math.sqrt(2.0 / math.pi)
    hx = 0.5 * x
    return hx * jnp.tanh(x * (c + (c * 0.044715) * (x * x))) + hx


def _silu(x):
    hx = 0.5 * x
    return hx * jnp.tanh(hx) + hx


def _identity(x):
    return x


def _resident(shape, layer=None):
    nd = len(shape)
    if layer is None:
        return pl.BlockSpec(shape, lambda *_: (0,) * nd, pipeline_mode=pl.Buffered(1))
    return pl.BlockSpec((None,) + tuple(shape), lambda *_: (layer,) + (0,) * nd,
                        pipeline_mode=pl.Buffered(1))


def _norm_kernel(x_ref, g_ref, ws_ref, h_ref, small_ref):
    x = x_ref[...]
    rstd = lax.rsqrt(jnp.mean(x * x, axis=-1, keepdims=True) + EPS)
    h = (x * rstd * g_ref[...]).astype(BF16)
    h_ref[...] = h
    small_ref[...] = _dot(h, ws_ref[...])


def _norm_call(x2, g, w_small, layer, tm):
    T, D = x2.shape
    return pl.pallas_call(
        _norm_kernel,
        grid=(T // tm,),
        in_specs=[pl.BlockSpec((tm, D), lambda i: (i, 0)),
                  _resident((1, D)),
                  _resident((D, SMALL_W), layer)],
        out_specs=[pl.BlockSpec((tm, D), lambda i: (i, 0)),
                   pl.BlockSpec((tm, SMALL_W), lambda i: (i, 0))],
        out_shape=[jax.ShapeDtypeStruct((T, D), BF16),
                   jax.ShapeDtypeStruct((T, SMALL_W), F32)],
        compiler_params=_params("parallel"),
    )(x2, g, w_small)


def _inproj_kernel(h_ref, w_ref, o_ref, *, act):
    o_ref[...] = act(_dot(h_ref[...], w_ref[...])).astype(o_ref.dtype)


def _inproj_call(h, w, layer, act, tm):
    T, D = h.shape
    N = w.shape[2]
    return pl.pallas_call(
        functools.partial(_inproj_kernel, act=act),
        grid=(T // tm, N // PROJ_TN),
        in_specs=[pl.BlockSpec((tm, D), lambda i, j: (i, 0)),
                  pl.BlockSpec((None, D, PROJ_TN), lambda i, j: (layer, 0, j))],
        out_specs=pl.BlockSpec((tm, PROJ_TN), lambda i, j: (i, j)),
        out_shape=jax.ShapeDtypeStruct((T, N), BF16),
        compiler_params=_params("parallel", "arbitrary"),
    )(h, w)


def _rope128(r, cosf, sins):
    lane = lax.broadcasted_iota(jnp.int32, r.shape, 1)
    partner = jnp.where(lane < MLA_ROPE // 2,
                        pltpu.roll(r, LANE - MLA_ROPE // 2, axis=1),
                        pltpu.roll(r, MLA_ROPE // 2, axis=1))
    return r * cosf + partner * sins


def _mla_prep_kernel(cq_ref, ckv_ref, small_ref, cos_ref, sin_ref,
                     cqg_ref, ckvg_ref, wuq_ref, wuk_ref, wuv_ref, qg_ref, kg_ref,
                     q_ref, k_ref, v_ref):
    cosf = cos_ref[...]
    sins = sin_ref[...]
    scale = MLA_QK ** -0.5 * LOG2_E

    cq = cq_ref[...].astype(F32)
    nq = cq * lax.rsqrt(jnp.mean(cq * cq, axis=-1, keepdims=True) + EPS) * cqg_ref[...]
    q_all = _dot(nq.astype(BF16), wuq_ref[...])

    ckv = ckv_ref[...].astype(F32)
    nkv = ckv * lax.rsqrt(jnp.mean(ckv * ckv, axis=-1, keepdims=True) + EPS) * ckvg_ref[...]
    nkv = nkv.astype(BF16)
    k_all = _dot(nkv, wuk_ref[...])
    v_ref[...] = _dot(nkv, wuv_ref[...]).astype(v_ref.dtype)

    qg = qg_ref[...]
    kg = kg_ref[...]
    lane = lax.broadcasted_iota(jnp.int32, (1, LANE), 1)
    kr = jnp.where(lane < MLA_ROPE, small_ref[...], 0.0)
    kr_ss = jnp.sum(kr * kr, axis=-1, keepdims=True)
    kr_rot = _rope128(kr * kg[:, MLA_NOPE:], cosf, sins)

    for hd in range(MLA_HEADS):
        qb = q_all[:, hd * MLA_QK_PAD:(hd + 1) * MLA_QK_PAD]
        rq = lax.rsqrt(jnp.sum(qb * qb, axis=-1, keepdims=True) * (1.0 / MLA_QK) + EPS) * scale
        qn = qb * rq * qg
        q_ref[:, hd * MLA_QK_PAD:hd * MLA_QK_PAD + MLA_NOPE] = qn[:, :MLA_NOPE].astype(q_ref.dtype)
        q_ref[:, hd * MLA_QK_PAD + MLA_NOPE:(hd + 1) * MLA_QK_PAD] = _rope128(
            qn[:, MLA_NOPE:], cosf, sins).astype(q_ref.dtype)

        kb = k_all[:, hd * MLA_NOPE:(hd + 1) * MLA_NOPE]
        rk = lax.rsqrt((jnp.sum(kb * kb, axis=-1, keepdims=True) + kr_ss) * (1.0 / MLA_QK) + EPS)
        k_ref[:, hd * MLA_QK_PAD:hd * MLA_QK_PAD + MLA_NOPE] = (
            kb * rk * kg[:, :MLA_NOPE]).astype(k_ref.dtype)
        k_ref[:, hd * MLA_QK_PAD + MLA_NOPE:(hd + 1) * MLA_QK_PAD] = (kr_rot * rk).astype(k_ref.dtype)


def _mla_prep_call(proj, small, cosf, sins, cqg, ckvg, wuq, wuk, wuv, qg, kg, tm):
    T = proj.shape[0]
    HQ = MLA_HEADS * MLA_QK_PAD
    return pl.pallas_call(
        _mla_prep_kernel,
        grid=(T // tm,),
        in_specs=[pl.BlockSpec((tm, MLA_Q_RANK), lambda i: (i, 0)),
                  pl.BlockSpec((tm, MLA_KV_RANK), lambda i: (i, 1)),
                  pl.BlockSpec((tm, SMALL_W), lambda i: (i, 0)),
                  pl.BlockSpec((tm, LANE), lambda i: (i, 0)),
                  pl.BlockSpec((tm, LANE), lambda i: (i, 0)),
                  _resident((1, MLA_Q_RANK)),
                  _resident((1, MLA_KV_RANK)),
                  _resident((MLA_Q_RANK, HQ)),
                  _resident((MLA_KV_RANK, MLA_HEADS * MLA_NOPE)),
                  _resident((MLA_KV_RANK, MLA_WIDTH)),
                  _resident((1, MLA_QK_PAD)),
                  _resident((1, MLA_QK_PAD))],
        out_specs=[pl.BlockSpec((tm, HQ), lambda i: (i, 0)),
                   pl.BlockSpec((tm, HQ), lambda i: (i, 0)),
                   pl.BlockSpec((tm, MLA_WIDTH), lambda i: (i, 0))],
        out_shape=[jax.ShapeDtypeStruct((T, HQ), BF16),
                   jax.ShapeDtypeStruct((T, HQ), BF16),
                   jax.ShapeDtypeStruct((T, MLA_WIDTH), BF16)],
        compiler_params=_params("parallel"),
    )(proj, proj, small, cosf, sins, cqg, ckvg, wuq, wuk, wuv, qg, kg)


def _attn_kernel(qlo_ref, qhi_ref, k_ref, v_ref, szlo_ref, szhi_ref, o_ref,
                 q2_ref, s_ref, mf_ref, lf_ref, pv_ref, *, tq, nq, n_steps):
    g = pl.program_id(0)
    half = nq // 2
    nt = nq + 1
    nl = tq // LANE
    i1 = jnp.minimum(g, n_steps - 1) % half
    i2 = jnp.maximum(g - 1, 0) % half

    @pl.when(g == 0)
    def _():
        for t in range(nt):
            s_ref[t] = jnp.zeros((tq, tq), F32)
            mf_ref[t] = jnp.zeros((tq, LANE), F32)

    q2_ref[0] = qlo_ref[...]
    q2_ref[1] = qhi_ref[...]

    def lane_fold(x, op):
        r = x[:, :LANE]
        for c in range(1, nl):
            r = op(r, x[:, c * LANE:(c + 1) * LANE])
        return r

    def is_lo(p, i):
        return True if p == 0 else (False if p >= half else p <= i)

    def pick(p, i, lo, hi):
        c = is_lo(p, i)
        if c is True:
            return lo
        if c is False:
            return hi
        return jnp.where(c, lo, hi)

    def kv_rows(p, i):
        if p == 0:
            j = i
        elif p == nq:
            j = nq - 1 - i
        else:
            j = pick(p, i, p - 1, p - i - 1)
        return pl.ds(pl.multiple_of(j * tq, tq), tq)

    def combine(ref, i, op, init):
        lo = jnp.full((tq, LANE), init, F32)
        hi = jnp.full((tq, LANE), init, F32)
        for p in range(nt):
            c = is_lo(p, i)
            x = ref[p]
            if c is True:
                lo = op(lo, x)
            elif c is False:
                hi = op(hi, x)
            else:
                lo = op(lo, jnp.where(c, x, init))
                hi = op(hi, jnp.where(c, init, x))
        return lo, hi

    m_lo, m_hi = combine(mf_ref, i2, jnp.maximum, MASK_VALUE)
    m_lo = jnp.max(m_lo, axis=-1, keepdims=True)
    m_hi = jnp.max(m_hi, axis=-1, keepdims=True)

    causal = (lax.broadcasted_iota(jnp.int32, (tq, tq), 0)
              >= lax.broadcasted_iota(jnp.int32, (tq, tq), 1))
    order_zero = jnp.zeros((tq, 1), F32)
    for p in range(nt):
        e = jnp.exp2(s_ref[p] - (pick(p, i2, m_lo, m_hi) + order_zero))
        lf_ref[p] = lane_fold(e, jnp.add)
        pv_ref[p] = _dot(e.astype(BF16), v_ref[kv_rows(p, i2), :])
        s = _dot_nt(q2_ref[pick(p, i1, 0, 1)], k_ref[kv_rows(p, i1), :])
        if p == 0 or p == nq:
            s = jnp.where(causal, s, MASK_VALUE)
        s_ref[p] = s
        mf = lane_fold(s, jnp.maximum)
        mf_ref[p] = mf
        bits = lax.bitcast_convert_type(mf[:, :1], jnp.uint32)
        order_zero = lax.bitcast_convert_type((bits >> 16) >> 16, F32)

    l_lo, l_hi = combine(lf_ref, i2, jnp.add, 0.0)
    a_lo, a_hi = combine(pv_ref, i2, jnp.add, 0.0)
    l_lo = jnp.sum(l_lo, axis=-1, keepdims=True)
    l_hi = jnp.sum(l_hi, axis=-1, keepdims=True)
    o_ref[:tq, :] = (a_lo / l_lo * szlo_ref[...].astype(F32)).astype(o_ref.dtype)
    o_ref[tq:, :] = (a_hi / l_hi * szhi_ref[...].astype(F32)).astype(o_ref.dtype)


def _attn_call(q, k, v, p_silu, B, S, tq):
    T = q.shape[0]
    nq = S // tq
    assert nq % 2 == 0 and MLA_V == LANE
    half = nq // 2
    n_steps = B * MLA_HEADS * half
    sz_col0 = SILU_Z_A * PROJ_TN // MLA_V

    def bhi(g):
        bh, i = g // half, g % half
        return bh // MLA_HEADS, bh % MLA_HEADS, i

    def first(g):
        return bhi(jnp.minimum(g, n_steps - 1))

    def second(g):
        return bhi(jnp.maximum(g - 1, 0))

    def lo_blk(f, col0=0):
        def index(g):
            b, h, i = f(g)
            return b * nq + i, col0 + h
        return index

    def hi_blk(f, col0=0):
        def index(g):
            b, h, i = f(g)
            return b * nq + nq - 1 - i, col0 + h
        return index

    def seq_blk(f):
        def index(g):
            b, h, _ = f(g)
            return b, h
        return index

    def out_blk(g):
        b, h, i = second(g)
        return b * half + i, h

    nt = nq + 1
    return pl.pallas_call(
        functools.partial(_attn_kernel, tq=tq, nq=nq, n_steps=n_steps),
        grid=(n_steps + 1,),
        in_specs=[pl.BlockSpec((tq, MLA_QK_PAD), lo_blk(first)),
                  pl.BlockSpec((tq, MLA_QK_PAD), hi_blk(first)),
                  pl.BlockSpec((S, MLA_QK_PAD), seq_blk(first)),
                  pl.BlockSpec((S, MLA_V), seq_blk(second)),
                  pl.BlockSpec((tq, MLA_V), lo_blk(second, sz_col0)),
                  pl.BlockSpec((tq, MLA_V), hi_blk(second, sz_col0))],
        out_specs=pl.BlockSpec((2 * tq, MLA_V), out_blk),
        out_shape=jax.ShapeDtypeStruct((T, MLA_WIDTH), BF16),
        scratch_shapes=[pltpu.VMEM((2, tq, MLA_QK_PAD), BF16),
                        pltpu.VMEM((nt, tq, tq), F32),
                        pltpu.VMEM((nt, tq, LANE), F32),
                        pltpu.VMEM((nt, tq, LANE), F32),
                        pltpu.VMEM((nt, tq, MLA_V), F32)],
        compiler_params=_params("arbitrary"),
    )(q, q, k, v, p_silu, p_silu)


def _paired_block(r, S, tq, tm):
    per_seq, per_tile, nq = S // tm, tq // tm, S // tq
    b, w = r // per_seq, r % per_seq
    u, sub = w // per_tile, w % per_tile
    pos = jnp.where(u < nq // 2, 2 * u, 2 * (nq - 1 - u) + 1)
    return b * per_seq + pos * per_tile + sub


def _sgu_kernel(u_ref, v_ref, sz_ref, vg_ref, ws_ref, bs_ref, o_ref, *, tm):
    v = v_ref[...].astype(F32)
    vn = (v * lax.rsqrt(jnp.mean(v * v, axis=-1, keepdims=True) + EPS) * vg_ref[...]).astype(BF16)
    row = lax.broadcasted_iota(jnp.int32, (SGU_CHUNK, SGU_CHUNK), 0)
    col = lax.broadcasted_iota(jnp.int32, (SGU_CHUNK, SGU_CHUNK), 1)
    gd = SGU_WIDTH // SGU_GROUPS
    for g in range(SGU_GROUPS):
        w = jnp.where(row >= col, ws_ref[g], 0.0).astype(BF16)
        bias = bs_ref[:, g:g + 1]
        for c in range(tm // SGU_CHUNK):
            rs = slice(c * SGU_CHUNK, (c + 1) * SGU_CHUNK)
            cs = slice(g * gd, (g + 1) * gd)
            mix = _dot(w, vn[rs, cs]) + bias
            o_ref[rs, cs] = (u_ref[rs, cs].astype(F32) * mix
                             * sz_ref[rs, cs].astype(F32)).astype(o_ref.dtype)


def _sgu_call(p_gelu, p_silu, vg, ws, bs_t, tm):
    T = p_gelu.shape[0]
    return pl.pallas_call(
        functools.partial(_sgu_kernel, tm=tm),
        grid=(T // tm,),
        in_specs=[pl.BlockSpec((tm, SGU_WIDTH), lambda i: (i, GELU_U_B)),
                  pl.BlockSpec((tm, SGU_WIDTH), lambda i: (i, GELU_V_B)),
                  pl.BlockSpec((tm, SGU_WIDTH), lambda i: (i, SILU_Z_B)),
                  _resident((1, SGU_WIDTH)),
                  _resident((SGU_GROUPS, SGU_CHUNK, SGU_CHUNK)),
                  _resident((SGU_CHUNK, SGU_GROUPS))],
        out_specs=pl.BlockSpec((tm, SGU_WIDTH), lambda i: (i, 0)),
        out_shape=jax.ShapeDtypeStruct((T, SGU_WIDTH), BF16),
        compiler_params=_params("parallel"),
    )(p_gelu, p_gelu, p_silu, vg, ws, bs_t)


def _split_bf16(x):
    hi = x.astype(BF16)
    lo = (x - hi.astype(F32)).astype(BF16)
    return hi, lo


def _gla_kernel(qk_ref, v_ref, sz_ref, small_ref, wa_ref, ba_ref, og_ref, o_ref, st_ref, *, cpb):
    L = GLA_CHUNK

    @pl.when(pl.program_id(1) == 0)
    def _():
        st_ref[...] = jnp.zeros(st_ref.shape, F32)

    a_hi, a_lo = _split_bf16(small_ref[...])
    w_hi, w_lo = _split_bf16(wa_ref[...])
    xg = _dot(a_hi, w_hi) + _dot(a_lo, w_hi) + _dot(a_hi, w_lo) + ba_ref[...]
    log_a = (jnp.minimum(xg, 0.0) - jnp.log(1.0 + jnp.exp(-jnp.abs(xg)))) * (1.0 / GLA_TAU)
    row = lax.broadcasted_iota(jnp.int32, (L, L), 0)
    col = lax.broadcasted_iota(jnp.int32, (L, L), 1)
    causal = row >= col
    ones_tril = causal.astype(BF16)
    og = og_ref[...]
    states = [st_ref[hd] for hd in range(GLA_HEADS)]

    for c in range(cpb):
        rs = slice(c * L, (c + 1) * L)
        la_hi, la_lo = _split_bf16(log_a[rs, :])
        bcum = _dot(ones_tril, la_hi) + _dot(ones_tril, la_lo)
        for hd in range(GLA_HEADS):
            ks = slice(hd * GLA_DK, (hd + 1) * GLA_DK)
            k2 = slice(GLA_HEADS * GLA_DK + hd * GLA_DK, GLA_HEADS * GLA_DK + (hd + 1) * GLA_DK)
            vs = slice(hd * GLA_DV, (hd + 1) * GLA_DV)
            b = bcum[:, ks]
            b_mid = b[L // 2:L // 2 + 1, :]
            b_last = b[L - 1:L, :]
            q = qk_ref[rs, ks].astype(F32) * (GLA_DK ** -0.5)
            k = qk_ref[rs, k2].astype(F32)
            v = v_ref[rs, vs]
            q_t = (q * jnp.exp(b - b_mid)).astype(BF16)
            k_t = (k * jnp.exp(b_mid - b)).astype(BF16)
            attn = jnp.where(causal, _dot_nt(q_t, k_t), 0.0).astype(BF16)
            st = states[hd]
            o = _dot_nt((q * jnp.exp(b)).astype(BF16), st.astype(BF16)) + _dot(attn, v)
            k_s = (k * jnp.exp(b_last - b)).astype(BF16)
            states[hd] = jnp.exp(b_last) * st + _dot_tn(v, k_s)
            on = o * lax.rsqrt(jnp.mean(o * o, axis=-1, keepdims=True) + EPS) * og
            o_ref[rs, vs] = (on * sz_ref[rs, vs].astype(F32)).astype(o_ref.dtype)

    for hd in range(GLA_HEADS):
        st_ref[hd] = states[hd]


def _gla_call(p_lin, p_silu, small, wa_pad, ba, og, B, S, cpb):
    T = p_lin.shape[0]
    R = cpb * GLA_CHUNK
    ns = S // R
    return pl.pallas_call(
        functools.partial(_gla_kernel, cpb=cpb),
        grid=(B, ns),
        in_specs=[pl.BlockSpec((R, PROJ_TN), lambda b, c: (b * ns + c, LIN_QK_C)),
                  pl.BlockSpec((R, GLA_WIDTH), lambda b, c: (b * ns + c, LIN_V_C)),
                  pl.BlockSpec((R, GLA_WIDTH), lambda b, c: (b * ns + c, SILU_Z_C)),
                  pl.BlockSpec((R, SMALL_W), lambda b, c: (b * ns + c, 0)),
                  _resident((SMALL_W, GLA_HEADS * GLA_DK)),
                  _resident((1, GLA_HEADS * GLA_DK)),
                  _resident((1, GLA_DV))],
        out_specs=pl.BlockSpec((R, GLA_WIDTH), lambda b, c: (b * ns + c, 0)),
        out_shape=jax.ShapeDtypeStruct((T, GLA_WIDTH), BF16),
        scratch_shapes=[pltpu.VMEM((GLA_HEADS, GLA_DV, GLA_DK), F32)],
        compiler_params=_params("parallel", "arbitrary"),
    )(p_lin, p_lin, p_silu, small, wa_pad, ba, og)


def _merge_kernel(x_ref, ya_ref, yb_ref, yc_ref, ga_ref, gb_ref, gc_ref, wb_ref, wo_ref, o_ref):
    merged = ga_ref[...].astype(F32) * _dot(ya_ref[...], wb_ref[0])
    merged += gb_ref[...].astype(F32) * _dot(yb_ref[...], wb_ref[1])
    merged += gc_ref[...].astype(F32) * _dot(yc_ref[...], wb_ref[2])
    o_ref[...] = x_ref[...] + _dot(merged.astype(BF16), wo_ref[...])


def _merge_call(x2, ya, yb, yc, p_gate, wb, wo, layer, tm, S, tq):
    T, D = x2.shape
    yspec = pl.BlockSpec((tm, BRANCH_WIDTH), lambda i: (i, 0))
    ya_spec = pl.BlockSpec((tm, BRANCH_WIDTH), lambda i: (_paired_block(i, S, tq, tm), 0))
    return pl.pallas_call(
        _merge_kernel,
        grid=(T // tm,),
        in_specs=[pl.BlockSpec((tm, D), lambda i: (i, 0)),
                  ya_spec, yspec, yspec,
                  pl.BlockSpec((tm, D), lambda i: (i, 0)),
                  pl.BlockSpec((tm, D), lambda i: (i, 1)),
                  pl.BlockSpec((tm, D), lambda i: (i, 2)),
                  _resident((N_BRANCH, BRANCH_WIDTH, D), layer),
                  _resident((D, D), layer)],
        out_specs=pl.BlockSpec((tm, D), lambda i: (i, 0)),
        out_shape=jax.ShapeDtypeStruct((T, D), F32),
        compiler_params=_params("parallel"),
    )(x2, ya, yb, yc, p_gate, p_gate, p_gate, wb, wo)


_IN_SIZES = (("c_q", MLA_Q_RANK), ("c_kv", MLA_KV_RANK), ("k_rope", MLA_ROPE), ("z_a", MLA_WIDTH),
             ("u_b", SGU_WIDTH), ("v_b", SGU_WIDTH), ("z_b", SGU_WIDTH),
             ("q_c", GLA_HEADS * GLA_DK), ("k_c", GLA_HEADS * GLA_DK), ("v_c", GLA_WIDTH),
             ("a_r", GLA_GATE_RANK), ("z_c", GLA_WIDTH), ("gates", N_BRANCH * D_MODEL))
_IN_SEG = {}
_off = 0
for _name, _size in _IN_SIZES:
    _IN_SEG[_name] = (_off, _size)
    _off += _size
IN_COLS = _off
_W_GROUPS = (("c_q", "c_kv", "q_c", "k_c", "v_c"), ("z_a", "z_b", "z_c"), ("u_b", "v_b"),
             ("gates",), ("k_rope", "a_r"))


def _wprep_kernel(w_ref, lin_ref, silu_ref, gelu_ref, gate_ref, small_ref):
    small_ref[...] = jnp.zeros(small_ref.shape, small_ref.dtype)
    for o_ref, names in zip((lin_ref, silu_ref, gelu_ref, gate_ref, small_ref), _W_GROUPS):
        dst = 0
        for name in names:
            src, size = _IN_SEG[name]
            o_ref[:, dst:dst + size] = w_ref[:, src:src + size].astype(o_ref.dtype)
            dst += size


def _split_w_in(w_in, tr):
    depth, D, cols = w_in.shape
    assert cols == IN_COLS
    widths = [sum(_IN_SEG[n][1] for n in names) for names in _W_GROUPS[:-1]] + [SMALL_W]
    return pl.pallas_call(
        _wprep_kernel,
        grid=(depth, D // tr),
        in_specs=[pl.BlockSpec((None, tr, cols), lambda l, r: (l, r, 0))],
        out_specs=[pl.BlockSpec((None, tr, n), lambda l, r: (l, r, 0)) for n in widths],
        out_shape=[jax.ShapeDtypeStruct((depth, D, n), BF16) for n in widths],
        compiler_params=_params("parallel", "parallel"),
    )(w_in)


def _pad_heads(w, per_head, lo, hi, width):
    r = w.shape[0]
    w = w.reshape(r, MLA_HEADS, per_head)[:, :, lo:hi]
    w = jnp.pad(w, ((0, 0), (0, 0), (0, width - (hi - lo))))
    return w.reshape(r, MLA_HEADS * width)


def _rope_tables(positions):
    half = MLA_ROPE // 2
    inv_freq = 1.0 / (ROPE_THETA ** (jnp.arange(0, MLA_ROPE, 2, dtype=F32) / MLA_ROPE))
    ang = positions.astype(F32).reshape(-1, 1) * inv_freq
    cos, sin = jnp.cos(ang), jnp.sin(ang)
    z = jnp.zeros((ang.shape[0], LANE - 2 * half), F32)
    return (jnp.concatenate([cos, cos, z], axis=1), jnp.concatenate([-sin, sin, z], axis=1))


def _pick_tile(n, want):
    t = min(n, want)
    while n % t:
        t //= 2
    return t


def kernel(x, positions, norm_g, w_in, mla_cq_norm, mla_ckv_norm, mla_w_uq, mla_w_ukv,
           mla_q_norm, mla_k_norm, sgu_v_norm, sgu_w_s, sgu_b_s, gla_w_a2, gla_b_a,
           gla_o_norm, w_branch, w_out):
    B, S, D = x.shape
    T = B * S
    depth = w_in.shape[0]
    tm_proj = _pick_tile(T, 2048)
    tm_row = _pick_tile(T, 512)
    tm_merge = _pick_tile(T, 256)
    tq = _pick_tile(S, 512)
    cpb = _pick_tile(S // GLA_CHUNK, 8)

    cosf, sins = _rope_tables(positions)
    x2 = x.reshape(T, D)
    w_lin, w_silu, w_gelu, w_gate, w_small = _split_w_in(w_in, _pick_tile(D, 128))
    wb, wo = w_branch.astype(BF16), w_out.astype(BF16)
    for l in range(depth):
        wuq = _pad_heads(mla_w_uq[l], MLA_QK, 0, MLA_QK, MLA_QK_PAD).astype(BF16)
        wuk = _pad_heads(mla_w_ukv[l], MLA_NOPE + MLA_V, 0, MLA_NOPE, MLA_NOPE).astype(BF16)
        wuv = _pad_heads(mla_w_ukv[l], MLA_NOPE + MLA_V, MLA_NOPE, MLA_NOPE + MLA_V, MLA_V).astype(BF16)
        qg = jnp.pad(mla_q_norm[l], (0, MLA_QK_PAD - MLA_QK)).reshape(1, MLA_QK_PAD)
        kg = jnp.pad(mla_k_norm[l], (0, MLA_QK_PAD - MLA_QK)).reshape(1, MLA_QK_PAD)
        wa_pad = jnp.zeros((SMALL_W, GLA_HEADS * GLA_DK), F32).at[
            MLA_ROPE:MLA_ROPE + GLA_GATE_RANK].set(gla_w_a2[l])

        h, small = _norm_call(x2, norm_g[l].reshape(1, D), w_small, l, tm_row)
        p_lin = _inproj_call(h, w_lin, l, _identity, tm_proj)
        p_silu = _inproj_call(h, w_silu, l, _silu, tm_proj)
        p_gelu = _inproj_call(h, w_gelu, l, _gelu_tanh, tm_proj)
        p_gate = _inproj_call(h, w_gate, l, _sigmoid, tm_proj)
        q, k, v = _mla_prep_call(p_lin, small, cosf, sins,
                                 mla_cq_norm[l].reshape(1, -1), mla_ckv_norm[l].reshape(1, -1),
                                 wuq, wuk, wuv, qg, kg, tm_row)
        ya = _attn_call(q, k, v, p_silu, B, S, tq)
        yb = _sgu_call(p_gelu, p_silu, sgu_v_norm[l].reshape(1, -1), sgu_w_s[l], sgu_b_s[l].T, tm_row)
        yc = _gla_call(p_lin, p_silu, small, wa_pad, gla_b_a[l].reshape(1, -1),
                       gla_o_norm[l].reshape(1, -1), B, S, cpb)
        x2 = _merge_call(x2, ya, yb, yc, p_gate, wb, wo, l, tm_merge, S, tq)
    return x2.reshape(B, S, D)
```

```python
import functools
import math

import jax
import jax.numpy as jnp
from jax import lax
from jax.experimental import pallas as pl
from jax.experimental.pallas import tpu as pltpu

F32 = jnp.float32
BF16 = jnp.bfloat16

MLA_HEADS = 8
MLA_NOPE = 128
MLA_ROPE = 64
MLA_QK = MLA_NOPE + MLA_ROPE
MLA_V = 128
MLA_Q_RANK = 512
MLA_KV_RANK = 512
MLA_WIDTH = MLA_HEADS * MLA_V
MLA_QK_PAD = 256
MLA_VT_ROWS = MLA_V + 16
SUBLANE = 8
ORDER_LAG = 3
ROPE_THETA = 10000.0
SGU_GROUPS = 8
SGU_CHUNK = 128
SGU_WIDTH = 1024
GLA_HEADS = 4
GLA_DK = 128
GLA_DV = 256
GLA_GATE_RANK = 16
GLA_TAU = 16.0
GLA_CHUNK = 128
GLA_WIDTH = GLA_HEADS * GLA_DV
D_MODEL = 2048
N_BRANCH = 3
BRANCH_WIDTH = 1024
EPS = 1e-6
LANE = 128
LOG2_E = math.log2(math.e)
MASK_VALUE = -1e30
SMALL_W = LANE

PROJ_TN = 1024
LIN_CQKV, LIN_QK_C, LIN_V_C = range(3)
SILU_Z_A, SILU_Z_B, SILU_Z_C = range(3)
GELU_U_B, GELU_V_B = range(2)

VMEM_LIMIT = 56 * 1024 * 1024


def _params(*sem):
    return pltpu.CompilerParams(dimension_semantics=sem, vmem_limit_bytes=VMEM_LIMIT)


def _dot(a, b):
    return jnp.dot(a, b, preferred_element_type=F32)


def _dot_nt(a, b):
    return lax.dot_general(a, b, (((1,), (1,)), ((), ())), preferred_element_type=F32)


def _dot_tn(a, b):
    return lax.dot_general(a, b, (((0,), (0,)), ((), ())), preferred_element_type=F32)


def _sigmoid(x):
    return 0.5 * jnp.tanh(0.5 * x) + 0.5


def _gelu_tanh(x):
    c = math.sqrt(2.0 / math.pi)
    hx = 0.5 * x
    return hx * jnp.tanh(x * (c + (c * 0.044715) * (x * x))) + hx


def _silu(x):
    hx = 0.5 * x
    return hx * jnp.tanh(hx) + hx


def _identity(x):
    return x


def _resident(shape, layer=None):
    nd = len(shape)
    if layer is None:
        return pl.BlockSpec(shape, lambda *_: (0,) * nd, pipeline_mode=pl.Buffered(1))
    return pl.BlockSpec((None,) + tuple(shape), lambda *_: (layer,) + (0,) * nd,
                        pipeline_mode=pl.Buffered(1))


def _norm_kernel(x_ref, g_ref, ws_ref, h_ref, small_ref):
    x = x_ref[...]
    rstd = lax.rsqrt(jnp.mean(x * x, axis=-1, keepdims=True) + EPS)
    h = (x * rstd * g_ref[...]).astype(BF16)
    h_ref[...] = h
    small_ref[...] = _dot(h, ws_ref[...])


def _norm_call(x2, g, w_small, layer, tm):
    T, D = x2.shape
    return pl.pallas_call(
        _norm_kernel,
        grid=(T // tm,),
        in_specs=[pl.BlockSpec((tm, D), lambda i: (i, 0)),
                  _resident((1, D)),
                  _resident((D, SMALL_W), layer)],
        out_specs=[pl.BlockSpec((tm, D), lambda i: (i, 0)),
                   pl.BlockSpec((tm, SMALL_W), lambda i: (i, 0))],
        out_shape=[jax.ShapeDtypeStruct((T, D), BF16),
                   jax.ShapeDtypeStruct((T, SMALL_W), F32)],
        compiler_params=_params("parallel"),
    )(x2, g, w_small)


def _inproj_kernel(h_ref, w_ref, o_ref, *, act):
    o_ref[...] = act(_dot(h_ref[...], w_ref[...])).astype(o_ref.dtype)


def _inproj_call(h, w, layer, act, tm):
    T, D = h.shape
    N = w.shape[2]
    return pl.pallas_call(
        functools.partial(_inproj_kernel, act=act),
        grid=(T // tm, N // PROJ_TN),
        in_specs=[pl.BlockSpec((tm, D), lambda i, j: (i, 0)),
                  pl.BlockSpec((None, D, PROJ_TN), lambda i, j: (layer, 0, j))],
        out_specs=pl.BlockSpec((tm, PROJ_TN), lambda i, j: (i, j)),
        out_shape=jax.ShapeDtypeStruct((T, N), BF16),
        compiler_params=_params("parallel", "arbitrary"),
    )(h, w)


def _rope128(r, cosf, sins):
    lane = lax.broadcasted_iota(jnp.int32, r.shape, 1)
    partner = jnp.where(lane < MLA_ROPE // 2,
                        pltpu.roll(r, LANE - MLA_ROPE // 2, axis=1),
                        pltpu.roll(r, MLA_ROPE // 2, axis=1))
    return r * cosf + partner * sins


def _mla_prep_kernel(cq_ref, ckv_ref, small_ref, cos_ref, sin_ref,
                     cqg_ref, ckvg_ref, wuq_ref, wuk_ref, wuv_ref, qg_ref, kg_ref,
                     q_ref, k_ref, vt_ref):
    cosf = cos_ref[...]
    sins = sin_ref[...]
    scale = MLA_QK ** -0.5 * LOG2_E

    cq = cq_ref[...].astype(F32)
    nq = cq * lax.rsqrt(jnp.mean(cq * cq, axis=-1, keepdims=True) + EPS) * cqg_ref[...]
    q_all = _dot(nq.astype(BF16), wuq_ref[...])

    ckv = ckv_ref[...].astype(F32)
    nkv = ckv * lax.rsqrt(jnp.mean(ckv * ckv, axis=-1, keepdims=True) + EPS) * ckvg_ref[...]
    nkv = nkv.astype(BF16)
    k_all = _dot(nkv, wuk_ref[...])
    vt = _dot_nt(wuv_ref[...], nkv)
    ones_row = (lax.broadcasted_iota(jnp.int32, (MLA_VT_ROWS - MLA_V, vt.shape[1]), 0) == 0)
    for hd in range(MLA_HEADS):
        vt_ref[hd, :MLA_V, :] = vt[hd * MLA_V:(hd + 1) * MLA_V, :].astype(vt_ref.dtype)
        vt_ref[hd, MLA_V:, :] = ones_row.astype(vt_ref.dtype)

    qg = qg_ref[...]
    kg = kg_ref[...]
    lane = lax.broadcasted_iota(jnp.int32, (1, LANE), 1)
    kr = jnp.where(lane < MLA_ROPE, small_ref[...], 0.0)
    kr_ss = jnp.sum(kr * kr, axis=-1, keepdims=True)
    kr_rot = _rope128(kr * kg[:, MLA_NOPE:], cosf, sins)

    for hd in range(MLA_HEADS):
        qb = q_all[:, hd * MLA_QK_PAD:(hd + 1) * MLA_QK_PAD]
        rq = lax.rsqrt(jnp.sum(qb * qb, axis=-1, keepdims=True) * (1.0 / MLA_QK) + EPS) * scale
        qn = qb * rq * qg
        q_ref[:, hd * MLA_QK_PAD:hd * MLA_QK_PAD + MLA_NOPE] = qn[:, :MLA_NOPE].astype(q_ref.dtype)
        q_ref[:, hd * MLA_QK_PAD + MLA_NOPE:(hd + 1) * MLA_QK_PAD] = _rope128(
            qn[:, MLA_NOPE:], cosf, sins).astype(q_ref.dtype)

        kb = k_all[:, hd * MLA_NOPE:(hd + 1) * MLA_NOPE]
        rk = lax.rsqrt((jnp.sum(kb * kb, axis=-1, keepdims=True) + kr_ss) * (1.0 / MLA_QK) + EPS)
        k_ref[:, hd * MLA_QK_PAD:hd * MLA_QK_PAD + MLA_NOPE] = (
            kb * rk * kg[:, :MLA_NOPE]).astype(k_ref.dtype)
        k_ref[:, hd * MLA_QK_PAD + MLA_NOPE:(hd + 1) * MLA_QK_PAD] = (kr_rot * rk).astype(k_ref.dtype)


def _mla_prep_call(proj, small, cosf, sins, cqg, ckvg, wuq, wuk, wuv, qg, kg, tm):
    T = proj.shape[0]
    HQ = MLA_HEADS * MLA_QK_PAD
    return pl.pallas_call(
        _mla_prep_kernel,
        grid=(T // tm,),
        in_specs=[pl.BlockSpec((tm, MLA_Q_RANK), lambda i: (i, 0)),
                  pl.BlockSpec((tm, MLA_KV_RANK), lambda i: (i, 1)),
                  pl.BlockSpec((tm, SMALL_W), lambda i: (i, 0)),
                  pl.BlockSpec((tm, LANE), lambda i: (i, 0)),
                  pl.BlockSpec((tm, LANE), lambda i: (i, 0)),
                  _resident((1, MLA_Q_RANK)),
                  _resident((1, MLA_KV_RANK)),
                  _resident((MLA_Q_RANK, HQ)),
                  _resident((MLA_KV_RANK, MLA_HEADS * MLA_NOPE)),
                  _resident((MLA_WIDTH, MLA_KV_RANK)),
                  _resident((1, MLA_QK_PAD)),
                  _resident((1, MLA_QK_PAD))],
        out_specs=[pl.BlockSpec((tm, HQ), lambda i: (i, 0)),
                   pl.BlockSpec((tm, HQ), lambda i: (i, 0)),
                   pl.BlockSpec((None, MLA_HEADS, MLA_VT_ROWS, tm), lambda i: (i, 0, 0, 0))],
        out_shape=[jax.ShapeDtypeStruct((T, HQ), BF16),
                   jax.ShapeDtypeStruct((T, HQ), BF16),
                   jax.ShapeDtypeStruct((T // tm, MLA_HEADS, MLA_VT_ROWS, tm), BF16)],
        compiler_params=_params("parallel"),
    )(proj, proj, small, cosf, sins, cqg, ckvg, wuq, wuk, wuv, qg, kg)


def _attn_kernel(qlo_ref, qhi_ref, k_ref, vt_ref, szlo_ref, szhi_ref, o_ref,
                 q2_ref, s_ref, mf_ref, pv_ref, *, tq, nq, n_steps):
    g = pl.program_id(0)
    half = nq // 2
    nt = nq + 1
    i1 = jnp.minimum(g, n_steps - 1) % half
    i2 = jnp.maximum(g - 1, 0) % half

    @pl.when(g == 0)
    def _():
        for t in range(nt):
            s_ref[t] = jnp.zeros((tq, tq), F32)
            mf_ref[t] = jnp.zeros((SUBLANE, tq), F32)

    q2_ref[0] = qlo_ref[...]
    q2_ref[1] = qhi_ref[...]

    def is_lo(p, i):
        return True if p == 0 else (False if p >= half else p <= i)

    def pick(p, i, lo, hi):
        c = is_lo(p, i)
        if c is True:
            return lo
        if c is False:
            return hi
        return jnp.where(c, lo, hi)

    def kv_tile(p, i):
        if p == 0:
            return i
        if p == nq:
            return nq - 1 - i
        return pick(p, i, p - 1, p - i - 1)

    def combine(ref, shape, i, op, init):
        lo = jnp.full(shape, init, F32)
        hi = jnp.full(shape, init, F32)
        for p in range(nt):
            c = is_lo(p, i)
            x = ref[p]
            if c is True:
                lo = op(lo, x)
            elif c is False:
                hi = op(hi, x)
            else:
                lo = op(lo, jnp.where(c, x, init))
                hi = op(hi, jnp.where(c, init, x))
        return lo, hi

    m_lo, m_hi = combine(mf_ref, (SUBLANE, tq), i2, jnp.maximum, MASK_VALUE)
    m_lo = jnp.max(m_lo, axis=0, keepdims=True)
    m_hi = jnp.max(m_hi, axis=0, keepdims=True)

    causal = (lax.broadcasted_iota(jnp.int32, (tq, tq), 0)
              <= lax.broadcasted_iota(jnp.int32, (tq, tq), 1))
    order_zeros = [jnp.zeros((1, tq), F32)] * ORDER_LAG
    for p in range(nt):
        e = jnp.exp2(s_ref[p] - (pick(p, i2, m_lo, m_hi) + order_zeros[0]))
        pv_ref[p] = _dot(vt_ref[kv_tile(p, i2)], e.astype(BF16))
        j = kv_tile(p, i1)
        s = _dot_nt(k_ref[pl.ds(pl.multiple_of(j * tq, tq), tq), :], q2_ref[pick(p, i1, 0, 1)])
        if p == 0 or p == nq:
            s = jnp.where(causal, s, MASK_VALUE)
        s_ref[p] = s
        mf = jnp.max(s.reshape(tq // SUBLANE, SUBLANE, tq), axis=0)
        mf_ref[p] = mf
        bits = lax.bitcast_convert_type(mf[:1, :], jnp.uint32)
        order_zeros = order_zeros[1:] + [lax.bitcast_convert_type((bits >> 16) >> 16, F32)]

    a_lo, a_hi = combine(pv_ref, (MLA_VT_ROWS, tq), i2, jnp.add, 0.0)
    for a, sz_ref, rows in ((a_lo, szlo_ref, slice(0, tq)), (a_hi, szhi_ref, slice(tq, 2 * tq))):
        ot = a[:MLA_V, :] / a[MLA_V:MLA_V + 1, :]
        o_ref[rows, :] = (ot.T * sz_ref[...].astype(F32)).astype(o_ref.dtype)


def _attn_call(q, k, vt, p_silu, B, S, tq):
    T = q.shape[0]
    nq = S // tq
    assert nq % 2 == 0 and MLA_V == LANE
    half = nq // 2
    n_steps = B * MLA_HEADS * half
    sz_col0 = SILU_Z_A * PROJ_TN // MLA_V

    def bhi(g):
        bh, i = g // half, g % half
        return bh // MLA_HEADS, bh % MLA_HEADS, i

    def first(g):
        return bhi(jnp.minimum(g, n_steps - 1))

    def second(g):
        return bhi(jnp.maximum(g - 1, 0))

    def lo_blk(f, col0=0):
        def index(g):
            b, h, i = f(g)
            return b * nq + i, col0 + h
        return index

    def hi_blk(f, col0=0):
        def index(g):
            b, h, i = f(g)
            return b * nq + nq - 1 - i, col0 + h
        return index

    def seq_blk(f):
        def index(g):
            b, h, _ = f(g)
            return b, h
        return index

    def seq_blk4(f):
        def index(g):
            b, h, _ = f(g)
            return b, h, 0, 0
        return index

    def out_blk(g):
        b, h, i = second(g)
        return b * half + i, h

    nt = nq + 1
    return pl.pallas_call(
        functools.partial(_attn_kernel, tq=tq, nq=nq, n_steps=n_steps),
        grid=(n_steps + 1,),
        in_specs=[pl.BlockSpec((tq, MLA_QK_PAD), lo_blk(first)),
                  pl.BlockSpec((tq, MLA_QK_PAD), hi_blk(first)),
                  pl.BlockSpec((S, MLA_QK_PAD), seq_blk(first)),
                  pl.BlockSpec((nq, None, MLA_VT_ROWS, tq), seq_blk4(second)),
                  pl.BlockSpec((tq, MLA_V), lo_blk(second, sz_col0)),
                  pl.BlockSpec((tq, MLA_V), hi_blk(second, sz_col0))],
        out_specs=pl.BlockSpec((2 * tq, MLA_V), out_blk),
        out_shape=jax.ShapeDtypeStruct((T, MLA_WIDTH), BF16),
        scratch_shapes=[pltpu.VMEM((2, tq, MLA_QK_PAD), BF16),
                        pltpu.VMEM((nt, tq, tq), F32),
                        pltpu.VMEM((nt, SUBLANE, tq), F32),
                        pltpu.VMEM((nt, MLA_VT_ROWS, tq), F32)],
        compiler_params=_params("arbitrary"),
    )(q, q, k, vt, p_silu, p_silu)


def _paired_block(r, S, tq, tm):
    per_seq, per_tile, nq = S // tm, tq // tm, S // tq
    b, w = r // per_seq, r % per_seq
    u, sub = w // per_tile, w % per_tile
    pos = jnp.where(u < nq // 2, 2 * u, 2 * (nq - 1 - u) + 1)
    return b * per_seq + pos * per_tile + sub


def _sgu_kernel(u_ref, v_ref, sz_ref, vg_ref, ws_ref, bs_ref, o_ref, *, tm):
    v = v_ref[...].astype(F32)
    vn = (v * lax.rsqrt(jnp.mean(v * v, axis=-1, keepdims=True) + EPS) * vg_ref[...]).astype(BF16)
    row = lax.broadcasted_iota(jnp.int32, (SGU_CHUNK, SGU_CHUNK), 0)
    col = lax.broadcasted_iota(jnp.int32, (SGU_CHUNK, SGU_CHUNK), 1)
    gd = SGU_WIDTH // SGU_GROUPS
    for g in range(SGU_GROUPS):
        w = jnp.where(row >= col, ws_ref[g], 0.0).astype(BF16)
        bias = bs_ref[:, g:g + 1]
        for c in range(tm // SGU_CHUNK):
            rs = slice(c * SGU_CHUNK, (c + 1) * SGU_CHUNK)
            cs = slice(g * gd, (g + 1) * gd)
            mix = _dot(w, vn[rs, cs]) + bias
            o_ref[rs, cs] = (u_ref[rs, cs].astype(F32) * mix
                             * sz_ref[rs, cs].astype(F32)).astype(o_ref.dtype)


def _sgu_call(p_gelu, p_silu, vg, ws, bs_t, tm):
    T = p_gelu.shape[0]
    return pl.pallas_call(
        functools.partial(_sgu_kernel, tm=tm),
        grid=(T // tm,),
        in_specs=[pl.BlockSpec((tm, SGU_WIDTH), lambda i: (i, GELU_U_B)),
                  pl.BlockSpec((tm, SGU_WIDTH), lambda i: (i, GELU_V_B)),
                  pl.BlockSpec((tm, SGU_WIDTH), lambda i: (i, SILU_Z_B)),
                  _resident((1, SGU_WIDTH)),
                  _resident((SGU_GROUPS, SGU_CHUNK, SGU_CHUNK)),
                  _resident((SGU_CHUNK, SGU_GROUPS))],
        out_specs=pl.BlockSpec((tm, SGU_WIDTH), lambda i: (i, 0)),
        out_shape=jax.ShapeDtypeStruct((T, SGU_WIDTH), BF16),
        compiler_params=_params("parallel"),
    )(p_gelu, p_gelu, p_silu, vg, ws, bs_t)


def _split_bf16(x):
    hi = x.astype(BF16)
    lo = (x - hi.astype(F32)).astype(BF16)
    return hi, lo


def _gla_kernel(qk_ref, v_ref, sz_ref, small_ref, wa_ref, ba_ref, og_ref, o_ref, st_ref, *, cpb):
    L = GLA_CHUNK

    @pl.when(pl.program_id(1) == 0)
    def _():
        st_ref[...] = jnp.zeros(st_ref.shape, F32)

    a_hi, a_lo = _split_bf16(small_ref[...])
    w_hi, w_lo = _split_bf16(wa_ref[...])
    xg = _dot(a_hi, w_hi) + _dot(a_lo, w_hi) + _dot(a_hi, w_lo) + ba_ref[...]
    log_a = (jnp.minimum(xg, 0.0) - jnp.log(1.0 + jnp.exp(-jnp.abs(xg)))) * (1.0 / GLA_TAU)
    row = lax.broadcasted_iota(jnp.int32, (L, L), 0)
    col = lax.broadcasted_iota(jnp.int32, (L, L), 1)
    causal = row >= col
    ones_tril = causal.astype(BF16)
    og = og_ref[...]
    states = [st_ref[hd] for hd in range(GLA_HEADS)]

    for c in range(cpb):
        rs = slice(c * L, (c + 1) * L)
        la_hi, la_lo = _split_bf16(log_a[rs, :])
        bcum = _dot(ones_tril, la_hi) + _dot(ones_tril, la_lo)
        for hd in range(GLA_HEADS):
            ks = slice(hd * GLA_DK, (hd + 1) * GLA_DK)
            k2 = slice(GLA_HEADS * GLA_DK + hd * GLA_DK, GLA_HEADS * GLA_DK + (hd + 1) * GLA_DK)
            vs = slice(hd * GLA_DV, (hd + 1) * GLA_DV)
            b = bcum[:, ks]
            b_mid = b[L // 2:L // 2 + 1, :]
            b_last = b[L - 1:L, :]
            q = qk_ref[rs, ks].astype(F32) * (GLA_DK ** -0.5)
            k = qk_ref[rs, k2].astype(F32)
            v = v_ref[rs, vs]
            q_t = (q * jnp.exp(b - b_mid)).astype(BF16)
            k_t = (k * jnp.exp(b_mid - b)).astype(BF16)
            attn = jnp.where(causal, _dot_nt(q_t, k_t), 0.0).astype(BF16)
            st = states[hd]
            o = _dot_nt((q * jnp.exp(b)).astype(BF16), st.astype(BF16)) + _dot(attn, v)
            k_s = (k * jnp.exp(b_last - b)).astype(BF16)
            states[hd] = jnp.exp(b_last) * st + _dot_tn(v, k_s)
            on = o * lax.rsqrt(jnp.mean(o * o, axis=-1, keepdims=True) + EPS) * og
            o_ref[rs, vs] = (on * sz_ref[rs, vs].astype(F32)).astype(o_ref.dtype)

    for hd in range(GLA_HEADS):
        st_ref[hd] = states[hd]


def _gla_call(p_lin, p_silu, small, wa_pad, ba, og, B, S, cpb):
    T = p_lin.shape[0]
    R = cpb * GLA_CHUNK
    ns = S // R
    return pl.pallas_call(
        functools.partial(_gla_kernel, cpb=cpb),
        grid=(B, ns),
        in_specs=[pl.BlockSpec((R, PROJ_TN), lambda b, c: (b * ns + c, LIN_QK_C)),
                  pl.BlockSpec((R, GLA_WIDTH), lambda b, c: (b * ns + c, LIN_V_C)),
                  pl.BlockSpec((R, GLA_WIDTH), lambda b, c: (b * ns + c, SILU_Z_C)),
                  pl.BlockSpec((R, SMALL_W), lambda b, c: (b * ns + c, 0)),
                  _resident((SMALL_W, GLA_HEADS * GLA_DK)),
                  _resident((1, GLA_HEADS * GLA_DK)),
                  _resident((1, GLA_DV))],
        out_specs=pl.BlockSpec((R, GLA_WIDTH), lambda b, c: (b * ns + c, 0)),
        out_shape=jax.ShapeDtypeStruct((T, GLA_WIDTH), BF16),
        scratch_shapes=[pltpu.VMEM((GLA_HEADS, GLA_DV, GLA_DK), F32)],
        compiler_params=_params("parallel", "arbitrary"),
    )(p_lin, p_lin, p_silu, small, wa_pad, ba, og)


def _merge_kernel(x_ref, ya_ref, yb_ref, yc_ref, ga_ref, gb_ref, gc_ref, wb_ref, wo_ref, o_ref):
    merged = ga_ref[...].astype(F32) * _dot(ya_ref[...], wb_ref[0])
    merged += gb_ref[...].astype(F32) * _dot(yb_ref[...], wb_ref[1])
    merged += gc_ref[...].astype(F32) * _dot(yc_ref[...], wb_ref[2])
    o_ref[...] = x_ref[...] + _dot(merged.astype(BF16), wo_ref[...])


def _merge_call(x2, ya, yb, yc, p_gate, wb, wo, layer, tm, S, tq):
    T, D = x2.shape
    yspec = pl.BlockSpec((tm, BRANCH_WIDTH), lambda i: (i, 0))
    ya_spec = pl.BlockSpec((tm, BRANCH_WIDTH), lambda i: (_paired_block(i, S, tq, tm), 0))
    return pl.pallas_call(
        _merge_kernel,
        grid=(T // tm,),
        in_specs=[pl.BlockSpec((tm, D), lambda i: (i, 0)),
                  ya_spec, yspec, yspec,
                  pl.BlockSpec((tm, D), lambda i: (i, 0)),
                  pl.BlockSpec((tm, D), lambda i: (i, 1)),
                  pl.BlockSpec((tm, D), lambda i: (i, 2)),
                  _resident((N_BRANCH, BRANCH_WIDTH, D), layer),
                  _resident((D, D), layer)],
        out_specs=pl.BlockSpec((tm, D), lambda i: (i, 0)),
        out_shape=jax.ShapeDtypeStruct((T, D), F32),
        compiler_params=_params("parallel"),
    )(x2, ya, yb, yc, p_gate, p_gate, p_gate, wb, wo)


_IN_SIZES = (("c_q", MLA_Q_RANK), ("c_kv", MLA_KV_RANK), ("k_rope", MLA_ROPE), ("z_a", MLA_WIDTH),
             ("u_b", SGU_WIDTH), ("v_b", SGU_WIDTH), ("z_b", SGU_WIDTH),
             ("q_c", GLA_HEADS * GLA_DK), ("k_c", GLA_HEADS * GLA_DK), ("v_c", GLA_WIDTH),
             ("a_r", GLA_GATE_RANK), ("z_c", GLA_WIDTH), ("gates", N_BRANCH * D_MODEL))
_IN_SEG = {}
_off = 0
for _name, _size in _IN_SIZES:
    _IN_SEG[_name] = (_off, _size)
    _off += _size
IN_COLS = _off
_W_GROUPS = (("c_q", "c_kv", "q_c", "k_c", "v_c"), ("z_a", "z_b", "z_c"), ("u_b", "v_b"),
             ("gates",), ("k_rope", "a_r"))


def _wprep_kernel(w_ref, lin_ref, silu_ref, gelu_ref, gate_ref, small_ref):
    small_ref[...] = jnp.zeros(small_ref.shape, small_ref.dtype)
    for o_ref, names in zip((lin_ref, silu_ref, gelu_ref, gate_ref, small_ref), _W_GROUPS):
        dst = 0
        for name in names:
            src, size = _IN_SEG[name]
            o_ref[:, dst:dst + size] = w_ref[:, src:src + size].astype(o_ref.dtype)
            dst += size


def _split_w_in(w_in, tr):
    depth, D, cols = w_in.shape
    assert cols == IN_COLS
    widths = [sum(_IN_SEG[n][1] for n in names) for names in _W_GROUPS[:-1]] + [SMALL_W]
    nr = D // tr
    return pl.pallas_call(
        _wprep_kernel,
        grid=(depth, nr),
        in_specs=[pl.BlockSpec((tr, cols), lambda l, r: (l * nr + r, 0))],
        out_specs=[pl.BlockSpec((None, tr, n), lambda l, r: (l, r, 0)) for n in widths],
        out_shape=[jax.ShapeDtypeStruct((depth, D, n), BF16) for n in widths],
        compiler_params=_params("parallel", "parallel"),
    )(w_in.reshape(depth * D, cols))


def _pad_heads(w, per_head, lo, hi, width):
    r = w.shape[0]
    w = w.reshape(r, MLA_HEADS, per_head)[:, :, lo:hi]
    w = jnp.pad(w, ((0, 0), (0, 0), (0, width - (hi - lo))))
    return w.reshape(r, MLA_HEADS * width)


def _rope_tables(positions):
    half = MLA_ROPE // 2
    inv_freq = 1.0 / (ROPE_THETA ** (jnp.arange(0, MLA_ROPE, 2, dtype=F32) / MLA_ROPE))
    ang = positions.astype(F32).reshape(-1, 1) * inv_freq
    cos, sin = jnp.cos(ang), jnp.sin(ang)
    z = jnp.zeros((ang.shape[0], LANE - 2 * half), F32)
    return (jnp.concatenate([cos, cos, z], axis=1), jnp.concatenate([-sin, sin, z], axis=1))


def _pick_tile(n, want):
    t = min(n, want)
    while n % t:
        t //= 2
    return t


def kernel(x, positions, norm_g, w_in, mla_cq_norm, mla_ckv_norm, mla_w_uq, mla_w_ukv,
           mla_q_norm, mla_k_norm, sgu_v_norm, sgu_w_s, sgu_b_s, gla_w_a2, gla_b_a,
           gla_o_norm, w_branch, w_out):
    B, S, D = x.shape
    T = B * S
    depth = w_in.shape[0]
    tm_proj = _pick_tile(T, 2048)
    tm_row = _pick_tile(T, 512)
    tm_merge = _pick_tile(T, 256)
    tq = _pick_tile(S, 512)
    cpb = _pick_tile(S // GLA_CHUNK, 8)

    cosf, sins = _rope_tables(positions)
    x2 = x.reshape(T, D)
    w_lin, w_silu, w_gelu, w_gate, w_small = _split_w_in(w_in, _pick_tile(D, 128))
    wb, wo = w_branch.astype(BF16), w_out.astype(BF16)
    for l in range(depth):
        wuq = _pad_heads(mla_w_uq[l], MLA_QK, 0, MLA_QK, MLA_QK_PAD).astype(BF16)
        wuk = _pad_heads(mla_w_ukv[l], MLA_NOPE + MLA_V, 0, MLA_NOPE, MLA_NOPE).astype(BF16)
        wuv = _pad_heads(mla_w_ukv[l], MLA_NOPE + MLA_V, MLA_NOPE, MLA_NOPE + MLA_V, MLA_V).T.astype(BF16)
        qg = jnp.pad(mla_q_norm[l], (0, MLA_QK_PAD - MLA_QK)).reshape(1, MLA_QK_PAD)
        kg = jnp.pad(mla_k_norm[l], (0, MLA_QK_PAD - MLA_QK)).reshape(1, MLA_QK_PAD)
        wa_pad = jnp.zeros((SMALL_W, GLA_HEADS * GLA_DK), F32).at[
            MLA_ROPE:MLA_ROPE + GLA_GATE_RANK].set(gla_w_a2[l])

        h, small = _norm_call(x2, norm_g[l].reshape(1, D), w_small, l, tm_row)
        p_lin = _inproj_call(h, w_lin, l, _identity, tm_proj)
        p_silu = _inproj_call(h, w_silu, l, _silu, tm_proj)
        p_gelu = _inproj_call(h, w_gelu, l, _gelu_tanh, tm_proj)
        p_gate = _inproj_call(h, w_gate, l, _sigmoid, tm_proj)
        q, k, vt = _mla_prep_call(p_lin, small, cosf, sins,
                                  mla_cq_norm[l].reshape(1, -1), mla_ckv_norm[l].reshape(1, -1),
                                  wuq, wuk, wuv, qg, kg, tq)
        ya = _attn_call(q, k, vt, p_silu, B, S, tq)
        yb = _sgu_call(p_gelu, p_silu, sgu_v_norm[l].reshape(1, -1), sgu_w_s[l], sgu_b_s[l].T, tm_row)
        yc = _gla_call(p_lin, p_silu, small, wa_pad, gla_b_a[l].reshape(1, -1),
                       gla_o_norm[l].reshape(1, -1), B, S, cpb)
        x2 = _merge_call(x2, ya, yb, yc, p_gate, wb, wo, l, tm_merge, S, tq)
    return x2.reshape(B, S, D)
```

```python
import functools
import math

import jax
import jax.numpy as jnp
from jax import lax
from jax.experimental import pallas as pl
from jax.experimental.pallas import tpu as pltpu

F32 = jnp.float32
BF16 = jnp.bfloat16

MLA_HEADS = 8
MLA_NOPE = 128
MLA_ROPE = 64
MLA_QK = MLA_NOPE + MLA_ROPE
MLA_V = 128
MLA_Q_RANK = 512
MLA_KV_RANK = 512
MLA_WIDTH = MLA_HEADS * MLA_V
MLA_QK_PAD = 256
MLA_VT_ROWS = MLA_V + 16
SUBLANE = 8
ORDER_LAG = 3
ROPE_THETA = 10000.0
SGU_GROUPS = 8
SGU_CHUNK = 128
SGU_WIDTH = 1024
GLA_HEADS = 4
GLA_DK = 128
GLA_DV = 256
GLA_GATE_RANK = 16
GLA_TAU = 16.0
GLA_CHUNK = 128
GLA_WIDTH = GLA_HEADS * GLA_DV
D_MODEL = 2048
N_BRANCH = 3
BRANCH_WIDTH = 1024
EPS = 1e-6
LANE = 128
LOG2_E = math.log2(math.e)
MASK_VALUE = -1e30
SMALL_W = LANE

PROJ_TN = 1024
LIN_CQKV, LIN_QK_C, LIN_V_C = range(3)
SILU_Z_A, SILU_Z_B, SILU_Z_C = range(3)
GELU_U_B, GELU_V_B = range(2)

VMEM_LIMIT = 56 * 1024 * 1024


def _params(*sem):
    return pltpu.CompilerParams(dimension_semantics=sem, vmem_limit_bytes=VMEM_LIMIT)


def _dot(a, b):
    return jnp.dot(a, b, preferred_element_type=F32)


def _dot_nt(a, b):
    return lax.dot_general(a, b, (((1,), (1,)), ((), ())), preferred_element_type=F32)


def _dot_tn(a, b):
    return lax.dot_general(a, b, (((0,), (0,)), ((), ())), preferred_element_type=F32)


def _sigmoid(x):
    return 0.5 * jnp.tanh(0.5 * x) + 0.5


def _gelu_tanh(x):
    c = math.sqrt(2.0 / math.pi)
    hx = 0.5 * x
    return hx * jnp.tanh(x * (c + (c * 0.044715) * (x * x))) + hx


def _silu(x):
    hx = 0.5 * x
    return hx * jnp.tanh(hx) + hx


def _identity(x):
    return x


def _resident(shape, layer=None):
    nd = len(shape)
    if layer is None:
        return pl.BlockSpec(shape, lambda *_: (0,) * nd, pipeline_mode=pl.Buffered(1))
    return pl.BlockSpec((None,) + tuple(shape), lambda *_: (layer,) + (0,) * nd,
                        pipeline_mode=pl.Buffered(1))


def _norm_kernel(x_ref, g_ref, ws_ref, h_ref, small_ref):
    x = x_ref[...]
    rstd = lax.rsqrt(jnp.mean(x * x, axis=-1, keepdims=True) + EPS)
    h = (x * rstd * g_ref[...]).astype(BF16)
    h_ref[...] = h
    small_ref[...] = _dot(h, ws_ref[...])


def _norm_call(x2, g, w_small, layer, tm):
    T, D = x2.shape
    return pl.pallas_call(
        _norm_kernel,
        grid=(T // tm,),
        in_specs=[pl.BlockSpec((tm, D), lambda i: (i, 0)),
                  _resident((1, D)),
                  _resident((D, SMALL_W), layer)],
        out_specs=[pl.BlockSpec((tm, D), lambda i: (i, 0)),
                   pl.BlockSpec((tm, SMALL_W), lambda i: (i, 0))],
        out_shape=[jax.ShapeDtypeStruct((T, D), BF16),
                   jax.ShapeDtypeStruct((T, SMALL_W), F32)],
        compiler_params=_params("parallel"),
    )(x2, g, w_small)


def _inproj_kernel(h_ref, w_ref, o_ref, *, act):
    o_ref[...] = act(_dot(h_ref[...], w_ref[...])).astype(o_ref.dtype)


def _inproj_call(h, w, layer, act, tm):
    T, D = h.shape
    N = w.shape[2]
    return pl.pallas_call(
        functools.partial(_inproj_kernel, act=act),
        grid=(T // tm, N // PROJ_TN),
        in_specs=[pl.BlockSpec((tm, D), lambda i, j: (i, 0)),
                  pl.BlockSpec((None, D, PROJ_TN), lambda i, j: (layer, 0, j))],
        out_specs=pl.BlockSpec((tm, PROJ_TN), lambda i, j: (i, j)),
        out_shape=jax.ShapeDtypeStruct((T, N), BF16),
        compiler_params=_params("parallel", "arbitrary"),
    )(h, w)


def _rope128(r, cosf, sins):
    lane = lax.broadcasted_iota(jnp.int32, r.shape, 1)
    partner = jnp.where(lane < MLA_ROPE // 2,
                        pltpu.roll(r, LANE - MLA_ROPE // 2, axis=1),
                        pltpu.roll(r, MLA_ROPE // 2, axis=1))
    return r * cosf + partner * sins


def _mla_prep_kernel(cq_ref, ckv_ref, small_ref, cos_ref, sin_ref,
                     cqg_ref, ckvg_ref, wuq_ref, wuk_ref, wuv_ref, qg_ref, kg_ref,
                     q_ref, k_ref, vt_ref):
    cosf = cos_ref[...]
    sins = sin_ref[...]
    scale = MLA_QK ** -0.5 * LOG2_E

    cq = cq_ref[...].astype(F32)
    nq = cq * lax.rsqrt(jnp.mean(cq * cq, axis=-1, keepdims=True) + EPS) * cqg_ref[...]
    q_all = _dot(nq.astype(BF16), wuq_ref[...])

    ckv = ckv_ref[...].astype(F32)
    nkv = ckv * lax.rsqrt(jnp.mean(ckv * ckv, axis=-1, keepdims=True) + EPS) * ckvg_ref[...]
    nkv = nkv.astype(BF16)
    k_all = _dot(nkv, wuk_ref[...])
    vt = _dot_nt(wuv_ref[...], nkv)
    ones_row = (lax.broadcasted_iota(jnp.int32, (MLA_VT_ROWS - MLA_V, vt.shape[1]), 0) == 0)
    for hd in range(MLA_HEADS):
        vt_ref[hd, :MLA_V, :] = vt[hd * MLA_V:(hd + 1) * MLA_V, :].astype(vt_ref.dtype)
        vt_ref[hd, MLA_V:, :] = ones_row.astype(vt_ref.dtype)

    qg = qg_ref[...]
    kg = kg_ref[...]
    lane = lax.broadcasted_iota(jnp.int32, (1, LANE), 1)
    kr = jnp.where(lane < MLA_ROPE, small_ref[...], 0.0)
    kr_ss = jnp.sum(kr * kr, axis=-1, keepdims=True)
    kr_rot = _rope128(kr * kg[:, MLA_NOPE:], cosf, sins)

    for hd in range(MLA_HEADS):
        qb = q_all[:, hd * MLA_QK_PAD:(hd + 1) * MLA_QK_PAD]
        rq = lax.rsqrt(jnp.sum(qb * qb, axis=-1, keepdims=True) * (1.0 / MLA_QK) + EPS) * scale
        qn = qb * rq * qg
        q_ref[:, hd * MLA_QK_PAD:hd * MLA_QK_PAD + MLA_NOPE] = qn[:, :MLA_NOPE].astype(q_ref.dtype)
        q_ref[:, hd * MLA_QK_PAD + MLA_NOPE:(hd + 1) * MLA_QK_PAD] = _rope128(
            qn[:, MLA_NOPE:], cosf, sins).astype(q_ref.dtype)

        kb = k_all[:, hd * MLA_NOPE:(hd + 1) * MLA_NOPE]
        rk = lax.rsqrt((jnp.sum(kb * kb, axis=-1, keepdims=True) + kr_ss) * (1.0 / MLA_QK) + EPS)
        k_ref[:, hd * MLA_QK_PAD:hd * MLA_QK_PAD + MLA_NOPE] = (
            kb * rk * kg[:, :MLA_NOPE]).astype(k_ref.dtype)
        k_ref[:, hd * MLA_QK_PAD + MLA_NOPE:(hd + 1) * MLA_QK_PAD] = (kr_rot * rk).astype(k_ref.dtype)


def _mla_prep_call(proj, small, cosf, sins, cqg, ckvg, wuq, wuk, wuv, qg, kg, tm):
    T = proj.shape[0]
    HQ = MLA_HEADS * MLA_QK_PAD
    return pl.pallas_call(
        _mla_prep_kernel,
        grid=(T // tm,),
        in_specs=[pl.BlockSpec((tm, MLA_Q_RANK), lambda i: (i, 0)),
                  pl.BlockSpec((tm, MLA_KV_RANK), lambda i: (i, 1)),
                  pl.BlockSpec((tm, SMALL_W), lambda i: (i, 0)),
                  pl.BlockSpec((tm, LANE), lambda i: (i, 0)),
                  pl.BlockSpec((tm, LANE), lambda i: (i, 0)),
                  _resident((1, MLA_Q_RANK)),
                  _resident((1, MLA_KV_RANK)),
                  _resident((MLA_Q_RANK, HQ)),
                  _resident((MLA_KV_RANK, MLA_HEADS * MLA_NOPE)),
                  _resident((MLA_WIDTH, MLA_KV_RANK)),
                  _resident((1, MLA_QK_PAD)),
                  _resident((1, MLA_QK_PAD))],
        out_specs=[pl.BlockSpec((tm, HQ), lambda i: (i, 0)),
                   pl.BlockSpec((tm, HQ), lambda i: (i, 0)),
                   pl.BlockSpec((None, MLA_HEADS, MLA_VT_ROWS, tm), lambda i: (i, 0, 0, 0))],
        out_shape=[jax.ShapeDtypeStruct((T, HQ), BF16),
                   jax.ShapeDtypeStruct((T, HQ), BF16),
                   jax.ShapeDtypeStruct((T // tm, MLA_HEADS, MLA_VT_ROWS, tm), BF16)],
        compiler_params=_params("parallel"),
    )(proj, proj, small, cosf, sins, cqg, ckvg, wuq, wuk, wuv, qg, kg)


def _attn_kernel(qlo_ref, qhi_ref, k_ref, vt_ref, szlo_ref, szhi_ref, o_ref,
                 q2_ref, s_ref, mf_ref, pv_ref, *, tq, nq, n_steps):
    g = pl.program_id(0)
    half = nq // 2
    nt = nq + 1
    i1 = jnp.minimum(g, n_steps - 1) % half
    i2 = jnp.maximum(g - 1, 0) % half

    @pl.when(g == 0)
    def _():
        for t in range(nt):
            s_ref[t] = jnp.zeros((tq, tq), F32)
            mf_ref[t] = jnp.zeros((SUBLANE, tq), F32)

    q2_ref[0] = qlo_ref[...]
    q2_ref[1] = qhi_ref[...]

    def is_lo(p, i):
        return True if p == 0 else (False if p >= half else p <= i)

    def pick(p, i, lo, hi):
        c = is_lo(p, i)
        if c is True:
            return lo
        if c is False:
            return hi
        return jnp.where(c, lo, hi)

    def kv_tile(p, i):
        if p == 0:
            return i
        if p == nq:
            return nq - 1 - i
        return pick(p, i, p - 1, p - i - 1)

    def combine(ref, shape, i, op, init):
        lo = jnp.full(shape, init, F32)
        hi = jnp.full(shape, init, F32)
        for p in range(nt):
            c = is_lo(p, i)
            x = ref[p]
            if c is True:
                lo = op(lo, x)
            elif c is False:
                hi = op(hi, x)
            else:
                lo = op(lo, jnp.where(c, x, init))
                hi = op(hi, jnp.where(c, init, x))
        return lo, hi

    m_lo, m_hi = combine(mf_ref, (SUBLANE, tq), i2, jnp.maximum, MASK_VALUE)
    m_lo = jnp.max(m_lo, axis=0, keepdims=True)
    m_hi = jnp.max(m_hi, axis=0, keepdims=True)

    causal = (lax.broadcasted_iota(jnp.int32, (tq, tq), 0)
              <= lax.broadcasted_iota(jnp.int32, (tq, tq), 1))
    order_zeros = [jnp.zeros((1, tq), F32)] * ORDER_LAG
    for p in range(nt):
        e = jnp.exp2(s_ref[p] - (pick(p, i2, m_lo, m_hi) + order_zeros[0]))
        pv_ref[p] = _dot(vt_ref[kv_tile(p, i2)], e.astype(BF16))
        j = kv_tile(p, i1)
        s = _dot_nt(k_ref[pl.ds(pl.multiple_of(j * tq, tq), tq), :], q2_ref[pick(p, i1, 0, 1)])
        if p == 0 or p == nq:
            s = jnp.where(causal, s, MASK_VALUE)
        s_ref[p] = s
        mf = jnp.max(s.reshape(tq // SUBLANE, SUBLANE, tq), axis=0)
        mf_ref[p] = mf
        bits = lax.bitcast_convert_type(mf[:1, :], jnp.uint32)
        order_zeros = order_zeros[1:] + [lax.bitcast_convert_type((bits >> 16) >> 16, F32)]

    a_lo, a_hi = combine(pv_ref, (MLA_VT_ROWS, tq), i2, jnp.add, 0.0)
    for a, sz_ref, rows in ((a_lo, szlo_ref, slice(0, tq)), (a_hi, szhi_ref, slice(tq, 2 * tq))):
        ot = a[:MLA_V, :] / a[MLA_V:MLA_V + 1, :]
        o_ref[rows, :] = (ot.T * _silu(sz_ref[...].astype(F32))).astype(o_ref.dtype)


def _attn_call(q, k, vt, p_silu, B, S, tq):
    T = q.shape[0]
    nq = S // tq
    assert nq % 2 == 0 and MLA_V == LANE
    half = nq // 2
    n_steps = B * MLA_HEADS * half
    sz_col0 = SILU_Z_A * PROJ_TN // MLA_V

    def bhi(g):
        bh, i = g // half, g % half
        return bh // MLA_HEADS, bh % MLA_HEADS, i

    def first(g):
        return bhi(jnp.minimum(g, n_steps - 1))

    def second(g):
        return bhi(jnp.maximum(g - 1, 0))

    def lo_blk(f, col0=0):
        def index(g):
            b, h, i = f(g)
            return b * nq + i, col0 + h
        return index

    def hi_blk(f, col0=0):
        def index(g):
            b, h, i = f(g)
            return b * nq + nq - 1 - i, col0 + h
        return index

    def seq_blk(f):
        def index(g):
            b, h, _ = f(g)
            return b, h
        return index

    def seq_blk4(f):
        def index(g):
            b, h, _ = f(g)
            return b, h, 0, 0
        return index

    def out_blk(g):
        b, h, i = second(g)
        return b * half + i, h

    nt = nq + 1
    return pl.pallas_call(
        functools.partial(_attn_kernel, tq=tq, nq=nq, n_steps=n_steps),
        grid=(n_steps + 1,),
        in_specs=[pl.BlockSpec((tq, MLA_QK_PAD), lo_blk(first)),
                  pl.BlockSpec((tq, MLA_QK_PAD), hi_blk(first)),
                  pl.BlockSpec((S, MLA_QK_PAD), seq_blk(first)),
                  pl.BlockSpec((nq, None, MLA_VT_ROWS, tq), seq_blk4(second)),
                  pl.BlockSpec((tq, MLA_V), lo_blk(second, sz_col0)),
                  pl.BlockSpec((tq, MLA_V), hi_blk(second, sz_col0))],
        out_specs=pl.BlockSpec((2 * tq, MLA_V), out_blk),
        out_shape=jax.ShapeDtypeStruct((T, MLA_WIDTH), BF16),
        scratch_shapes=[pltpu.VMEM((2, tq, MLA_QK_PAD), BF16),
                        pltpu.VMEM((nt, tq, tq), F32),
                        pltpu.VMEM((nt, SUBLANE, tq), F32),
                        pltpu.VMEM((nt, MLA_VT_ROWS, tq), F32)],
        compiler_params=_params("arbitrary"),
    )(q, q, k, vt, p_silu, p_silu)


def _paired_block(r, S, tq, tm):
    per_seq, per_tile, nq = S // tm, tq // tm, S // tq
    b, w = r // per_seq, r % per_seq
    u, sub = w // per_tile, w % per_tile
    pos = jnp.where(u < nq // 2, 2 * u, 2 * (nq - 1 - u) + 1)
    return b * per_seq + pos * per_tile + sub


def _sgu_kernel(u_ref, v_ref, sz_ref, vg_ref, ws_ref, bs_ref, o_ref, *, tm):
    v = v_ref[...].astype(F32)
    vn = (v * lax.rsqrt(jnp.mean(v * v, axis=-1, keepdims=True) + EPS) * vg_ref[...]).astype(BF16)
    row = lax.broadcasted_iota(jnp.int32, (SGU_CHUNK, SGU_CHUNK), 0)
    col = lax.broadcasted_iota(jnp.int32, (SGU_CHUNK, SGU_CHUNK), 1)
    gd = SGU_WIDTH // SGU_GROUPS
    for g in range(SGU_GROUPS):
        w = jnp.where(row >= col, ws_ref[g], 0.0).astype(BF16)
        bias = bs_ref[:, g:g + 1]
        for c in range(tm // SGU_CHUNK):
            rs = slice(c * SGU_CHUNK, (c + 1) * SGU_CHUNK)
            cs = slice(g * gd, (g + 1) * gd)
            mix = _dot(w, vn[rs, cs]) + bias
            o_ref[rs, cs] = (u_ref[rs, cs].astype(F32) * mix
                             * _silu(sz_ref[rs, cs].astype(F32))).astype(o_ref.dtype)


def _sgu_call(p_gelu, p_silu, vg, ws, bs_t, tm):
    T = p_gelu.shape[0]
    return pl.pallas_call(
        functools.partial(_sgu_kernel, tm=tm),
        grid=(T // tm,),
        in_specs=[pl.BlockSpec((tm, SGU_WIDTH), lambda i: (i, GELU_U_B)),
                  pl.BlockSpec((tm, SGU_WIDTH), lambda i: (i, GELU_V_B)),
                  pl.BlockSpec((tm, SGU_WIDTH), lambda i: (i, SILU_Z_B)),
                  _resident((1, SGU_WIDTH)),
                  _resident((SGU_GROUPS, SGU_CHUNK, SGU_CHUNK)),
                  _resident((SGU_CHUNK, SGU_GROUPS))],
        out_specs=pl.BlockSpec((tm, SGU_WIDTH), lambda i: (i, 0)),
        out_shape=jax.ShapeDtypeStruct((T, SGU_WIDTH), BF16),
        compiler_params=_params("parallel"),
    )(p_gelu, p_gelu, p_silu, vg, ws, bs_t)


def _split_bf16(x):
    hi = x.astype(BF16)
    lo = (x - hi.astype(F32)).astype(BF16)
    return hi, lo


def _gla_kernel(qk_ref, v_ref, sz_ref, small_ref, wa_ref, ba_ref, og_ref, o_ref, st_ref, *, cpb):
    L = GLA_CHUNK

    @pl.when(pl.program_id(1) == 0)
    def _():
        st_ref[...] = jnp.zeros(st_ref.shape, F32)

    a_hi, a_lo = _split_bf16(small_ref[...])
    w_hi, w_lo = _split_bf16(wa_ref[...])
    xg = _dot(a_hi, w_hi) + _dot(a_lo, w_hi) + _dot(a_hi, w_lo) + ba_ref[...]
    log_a = (jnp.minimum(xg, 0.0) - jnp.log(1.0 + jnp.exp(-jnp.abs(xg)))) * (1.0 / GLA_TAU)
    row = lax.broadcasted_iota(jnp.int32, (L, L), 0)
    col = lax.broadcasted_iota(jnp.int32, (L, L), 1)
    causal = row >= col
    ones_tril = causal.astype(BF16)
    og = og_ref[...]
    states = [st_ref[hd] for hd in range(GLA_HEADS)]

    for c in range(cpb):
        rs = slice(c * L, (c + 1) * L)
        la_hi, la_lo = _split_bf16(log_a[rs, :])
        bcum = _dot(ones_tril, la_hi) + _dot(ones_tril, la_lo)
        for hd in range(GLA_HEADS):
            ks = slice(hd * GLA_DK, (hd + 1) * GLA_DK)
            k2 = slice(GLA_HEADS * GLA_DK + hd * GLA_DK, GLA_HEADS * GLA_DK + (hd + 1) * GLA_DK)
            vs = slice(hd * GLA_DV, (hd + 1) * GLA_DV)
            b = bcum[:, ks]
            b_mid = b[L // 2:L // 2 + 1, :]
            b_last = b[L - 1:L, :]
            q = qk_ref[rs, ks].astype(F32) * (GLA_DK ** -0.5)
            k = qk_ref[rs, k2].astype(F32)
            v = v_ref[rs, vs]
            q_t = (q * jnp.exp(b - b_mid)).astype(BF16)
            k_t = (k * jnp.exp(b_mid - b)).astype(BF16)
            attn = jnp.where(causal, _dot_nt(q_t, k_t), 0.0).astype(BF16)
            st = states[hd]
            o = _dot_nt((q * jnp.exp(b)).astype(BF16), st.astype(BF16)) + _dot(attn, v)
            k_s = (k * jnp.exp(b_last - b)).astype(BF16)
            states[hd] = jnp.exp(b_last) * st + _dot_tn(v, k_s)
            on = o * lax.rsqrt(jnp.mean(o * o, axis=-1, keepdims=True) + EPS) * og
            o_ref[rs, vs] = (on * _silu(sz_ref[rs, vs].astype(F32))).astype(o_ref.dtype)

    for hd in range(GLA_HEADS):
        st_ref[hd] = states[hd]


def _gla_call(p_lin, p_silu, small, wa_pad, ba, og, B, S, cpb):
    T = p_lin.shape[0]
    R = cpb * GLA_CHUNK
    ns = S // R
    return pl.pallas_call(
        functools.partial(_gla_kernel, cpb=cpb),
        grid=(B, ns),
        in_specs=[pl.BlockSpec((R, PROJ_TN), lambda b, c: (b * ns + c, LIN_QK_C)),
                  pl.BlockSpec((R, GLA_WIDTH), lambda b, c: (b * ns + c, LIN_V_C)),
                  pl.BlockSpec((R, GLA_WIDTH), lambda b, c: (b * ns + c, SILU_Z_C)),
                  pl.BlockSpec((R, SMALL_W), lambda b, c: (b * ns + c, 0)),
                  _resident((SMALL_W, GLA_HEADS * GLA_DK)),
                  _resident((1, GLA_HEADS * GLA_DK)),
                  _resident((1, GLA_DV))],
        out_specs=pl.BlockSpec((R, GLA_WIDTH), lambda b, c: (b * ns + c, 0)),
        out_shape=jax.ShapeDtypeStruct((T, GLA_WIDTH), BF16),
        scratch_shapes=[pltpu.VMEM((GLA_HEADS, GLA_DV, GLA_DK), F32)],
        compiler_params=_params("parallel", "arbitrary"),
    )(p_lin, p_lin, p_silu, small, wa_pad, ba, og)


def _merge_kernel(ya_ref, yb_ref, yc_ref, ga_ref, gb_ref, gc_ref, wb_ref, o_ref):
    merged = _sigmoid(ga_ref[...].astype(F32)) * _dot(ya_ref[...], wb_ref[0])
    merged += _sigmoid(gb_ref[...].astype(F32)) * _dot(yb_ref[...], wb_ref[1])
    merged += _sigmoid(gc_ref[...].astype(F32)) * _dot(yc_ref[...], wb_ref[2])
    o_ref[...] = merged.astype(o_ref.dtype)


def _merge_call(ya, yb, yc, p_gate, wb, layer, tm, S, tq):
    T = ya.shape[0]
    D = wb.shape[-1]
    yspec = pl.BlockSpec((tm, BRANCH_WIDTH), lambda i: (i, 0))
    ya_spec = pl.BlockSpec((tm, BRANCH_WIDTH), lambda i: (_paired_block(i, S, tq, tm), 0))
    return pl.pallas_call(
        _merge_kernel,
        grid=(T // tm,),
        in_specs=[ya_spec, yspec, yspec,
                  pl.BlockSpec((tm, D), lambda i: (i, 0)),
                  pl.BlockSpec((tm, D), lambda i: (i, 1)),
                  pl.BlockSpec((tm, D), lambda i: (i, 2)),
                  _resident((N_BRANCH, BRANCH_WIDTH, D), layer)],
        out_specs=pl.BlockSpec((tm, D), lambda i: (i, 0)),
        out_shape=jax.ShapeDtypeStruct((T, D), BF16),
        compiler_params=_params("parallel"),
    )(ya, yb, yc, p_gate, p_gate, p_gate, wb)


def _out_kernel(x_ref, m_ref, wo_ref, o_ref):
    o_ref[...] = x_ref[...] + _dot(m_ref[...], wo_ref[...])


def _out_call(x2, merged, wo, layer, tm):
    T, D = x2.shape
    return pl.pallas_call(
        _out_kernel,
        grid=(T // tm,),
        in_specs=[pl.BlockSpec((tm, D), lambda i: (i, 0)),
                  pl.BlockSpec((tm, D), lambda i: (i, 0)),
                  _resident((D, D), layer)],
        out_specs=pl.BlockSpec((tm, D), lambda i: (i, 0)),
        out_shape=jax.ShapeDtypeStruct((T, D), F32),
        compiler_params=_params("parallel"),
    )(x2, merged, wo)


_IN_SIZES = (("c_q", MLA_Q_RANK), ("c_kv", MLA_KV_RANK), ("k_rope", MLA_ROPE), ("z_a", MLA_WIDTH),
             ("u_b", SGU_WIDTH), ("v_b", SGU_WIDTH), ("z_b", SGU_WIDTH),
             ("q_c", GLA_HEADS * GLA_DK), ("k_c", GLA_HEADS * GLA_DK), ("v_c", GLA_WIDTH),
             ("a_r", GLA_GATE_RANK), ("z_c", GLA_WIDTH), ("gates", N_BRANCH * D_MODEL))
_IN_SEG = {}
_off = 0
for _name, _size in _IN_SIZES:
    _IN_SEG[_name] = (_off, _size)
    _off += _size
IN_COLS = _off
_W_GROUPS = (("c_q", "c_kv", "q_c", "k_c", "v_c"), ("z_a", "z_b", "z_c"), ("u_b", "v_b"),
             ("gates",), ("k_rope", "a_r"))


def _wprep_kernel(w_ref, lin_ref, silu_ref, gelu_ref, gate_ref, small_ref):
    small_ref[...] = jnp.zeros(small_ref.shape, small_ref.dtype)
    for o_ref, names in zip((lin_ref, silu_ref, gelu_ref, gate_ref, small_ref), _W_GROUPS):
        dst = 0
        for name in names:
            src, size = _IN_SEG[name]
            o_ref[:, dst:dst + size] = w_ref[:, src:src + size].astype(o_ref.dtype)
            dst += size


def _split_w_in(w_in, tr):
    depth, D, cols = w_in.shape
    assert cols == IN_COLS
    widths = [sum(_IN_SEG[n][1] for n in names) for names in _W_GROUPS[:-1]] + [SMALL_W]
    nr = D // tr
    return pl.pallas_call(
        _wprep_kernel,
        grid=(depth, nr),
        in_specs=[pl.BlockSpec((tr, cols), lambda l, r: (l * nr + r, 0))],
        out_specs=[pl.BlockSpec((None, tr, n), lambda l, r: (l, r, 0)) for n in widths],
        out_shape=[jax.ShapeDtypeStruct((depth, D, n), BF16) for n in widths],
        compiler_params=_params("parallel", "parallel"),
    )(w_in.reshape(depth * D, cols))


def _pad_heads(w, per_head, lo, hi, width):
    r = w.shape[0]
    w = w.reshape(r, MLA_HEADS, per_head)[:, :, lo:hi]
    w = jnp.pad(w, ((0, 0), (0, 0), (0, width - (hi - lo))))
    return w.reshape(r, MLA_HEADS * width)


def _rope_tables(positions):
    half = MLA_ROPE // 2
    inv_freq = 1.0 / (ROPE_THETA ** (jnp.arange(0, MLA_ROPE, 2, dtype=F32) / MLA_ROPE))
    ang = positions.astype(F32).reshape(-1, 1) * inv_freq
    cos, sin = jnp.cos(ang), jnp.sin(ang)
    z = jnp.zeros((ang.shape[0], LANE - 2 * half), F32)
    return (jnp.concatenate([cos, cos, z], axis=1), jnp.concatenate([-sin, sin, z], axis=1))


def _pick_tile(n, want):
    t = min(n, want)
    while n % t:
        t //= 2
    return t


def kernel(x, positions, norm_g, w_in, mla_cq_norm, mla_ckv_norm, mla_w_uq, mla_w_ukv,
           mla_q_norm, mla_k_norm, sgu_v_norm, sgu_w_s, sgu_b_s, gla_w_a2, gla_b_a,
           gla_o_norm, w_branch, w_out):
    B, S, D = x.shape
    T = B * S
    depth = w_in.shape[0]
    tm_proj = _pick_tile(T, 2048)
    tm_row = _pick_tile(T, 512)
    tm_merge = _pick_tile(T, 256)
    tq = _pick_tile(S, 512)
    cpb = _pick_tile(S // GLA_CHUNK, 8)

    cosf, sins = _rope_tables(positions)
    x2 = x.reshape(T, D)
    w_lin, w_silu, w_gelu, w_gate, w_small = _split_w_in(w_in, _pick_tile(D, 128))
    wb, wo = w_branch.astype(BF16), w_out.astype(BF16)
    for l in range(depth):
        wuq = _pad_heads(mla_w_uq[l], MLA_QK, 0, MLA_QK, MLA_QK_PAD).astype(BF16)
        wuk = _pad_heads(mla_w_ukv[l], MLA_NOPE + MLA_V, 0, MLA_NOPE, MLA_NOPE).astype(BF16)
        wuv = _pad_heads(mla_w_ukv[l], MLA_NOPE + MLA_V, MLA_NOPE, MLA_NOPE + MLA_V, MLA_V).T.astype(BF16)
        qg = jnp.pad(mla_q_norm[l], (0, MLA_QK_PAD - MLA_QK)).reshape(1, MLA_QK_PAD)
        kg = jnp.pad(mla_k_norm[l], (0, MLA_QK_PAD - MLA_QK)).reshape(1, MLA_QK_PAD)
        wa_pad = jnp.zeros((SMALL_W, GLA_HEADS * GLA_DK), F32).at[
            MLA_ROPE:MLA_ROPE + GLA_GATE_RANK].set(gla_w_a2[l])

        h, small = _norm_call(x2, norm_g[l].reshape(1, D), w_small, l, tm_row)
        p_lin = _inproj_call(h, w_lin, l, _identity, tm_proj)
        p_silu = _inproj_call(h, w_silu, l, _identity, tm_proj)
        p_gelu = _inproj_call(h, w_gelu, l, _gelu_tanh, tm_proj)
        p_gate = _inproj_call(h, w_gate, l, _identity, tm_proj)
        q, k, vt = _mla_prep_call(p_lin, small, cosf, sins,
                                  mla_cq_norm[l].reshape(1, -1), mla_ckv_norm[l].reshape(1, -1),
                                  wuq, wuk, wuv, qg, kg, tq)
        ya = _attn_call(q, k, vt, p_silu, B, S, tq)
        yb = _sgu_call(p_gelu, p_silu, sgu_v_norm[l].reshape(1, -1), sgu_w_s[l], sgu_b_s[l].T, tm_row)
        yc = _gla_call(p_lin, p_silu, small, wa_pad, gla_b_a[l].reshape(1, -1),
                       gla_o_norm[l].reshape(1, -1), B, S, cpb)
        merged = _merge_call(ya, yb, yc, p_gate, wb, l, tm_row, S, tq)
        x2 = _out_call(x2, merged, wo, l, tm_row)
    return x2.reshape(B, S, D)
```

```python
import functools
import math

import jax
import jax.numpy as jnp
from jax import lax
from jax.experimental import pallas as pl
from jax.experimental.pallas import tpu as pltpu

F32 = jnp.float32
BF16 = jnp.bfloat16

MLA_HEADS = 8
MLA_NOPE = 128
MLA_ROPE = 64
MLA_QK = MLA_NOPE + MLA_ROPE
MLA_V = 128
MLA_Q_RANK = 512
MLA_KV_RANK = 512
MLA_WIDTH = MLA_HEADS * MLA_V
MLA_QK_PAD = 256
MLA_VT_ROWS = MLA_V + 16
SUBLANE = 8
ORDER_LAG = 3
ROPE_THETA = 10000.0
SGU_GROUPS = 8
SGU_CHUNK = 128
SGU_WIDTH = 1024
GLA_HEADS = 4
GLA_DK = 128
GLA_DV = 256
GLA_GATE_RANK = 16
GLA_TAU = 16.0
GLA_CHUNK = 128
GLA_WIDTH = GLA_HEADS * GLA_DV
D_MODEL = 2048
N_BRANCH = 3
BRANCH_WIDTH = 1024
EPS = 1e-6
LANE = 128
LOG2_E = math.log2(math.e)
MASK_VALUE = -1e30
SMALL_W = LANE

PROJ_TN = 1024
LIN_CQKV, LIN_QK_C, LIN_V_C = range(3)
SILU_Z_A, SILU_Z_B, SILU_Z_C = range(3)
GELU_U_B, GELU_V_B = range(2)

VMEM_LIMIT = 56 * 1024 * 1024


def _params(*sem):
    return pltpu.CompilerParams(dimension_semantics=sem, vmem_limit_bytes=VMEM_LIMIT)


def _dot(a, b):
    return jnp.dot(a, b, preferred_element_type=F32)


def _dot_nt(a, b):
    return lax.dot_general(a, b, (((1,), (1,)), ((), ())), preferred_element_type=F32)


def _dot_tn(a, b):
    return lax.dot_general(a, b, (((0,), (0,)), ((), ())), preferred_element_type=F32)


def _sigmoid(x):
    return 0.5 * jnp.tanh(0.5 * x) + 0.5


def _gelu_tanh(x):
    c = math.sqrt(2.0 / math.pi)
    hx = 0.5 * x
    return hx * jnp.tanh(x * (c + (c * 0.044715) * (x * x))) + hx


def _silu(x):
    hx = 0.5 * x
    return hx * jnp.tanh(hx) + hx


def _identity(x):
    return x


def _resident(shape, layer=None):
    nd = len(shape)
    if layer is None:
        return pl.BlockSpec(shape, lambda *_: (0,) * nd, pipeline_mode=pl.Buffered(1))
    return pl.BlockSpec((None,) + tuple(shape), lambda *_: (layer,) + (0,) * nd,
                        pipeline_mode=pl.Buffered(1))


def _norm_kernel(x_ref, g_ref, ws_ref, h_ref, small_ref):
    x = x_ref[...]
    rstd = lax.rsqrt(jnp.mean(x * x, axis=-1, keepdims=True) + EPS)
    h = (x * rstd * g_ref[...]).astype(BF16)
    h_ref[...] = h
    small_ref[...] = _dot(h, ws_ref[...])


def _norm_call(x2, g, w_small, layer, tm):
    T, D = x2.shape
    return pl.pallas_call(
        _norm_kernel,
        grid=(T // tm,),
        in_specs=[pl.BlockSpec((tm, D), lambda i: (i, 0)),
                  _resident((1, D)),
                  _resident((D, SMALL_W), layer)],
        out_specs=[pl.BlockSpec((tm, D), lambda i: (i, 0)),
                   pl.BlockSpec((tm, SMALL_W), lambda i: (i, 0))],
        out_shape=[jax.ShapeDtypeStruct((T, D), BF16),
                   jax.ShapeDtypeStruct((T, SMALL_W), F32)],
        compiler_params=_params("parallel"),
    )(x2, g, w_small)


def _inproj_kernel(h_ref, w_ref, o_ref, *, act):
    o_ref[...] = act(_dot(h_ref[...], w_ref[...])).astype(o_ref.dtype)


def _inproj_call(h, w, layer, act, tm):
    T, D = h.shape
    N = w.shape[2]
    return pl.pallas_call(
        functools.partial(_inproj_kernel, act=act),
        grid=(T // tm, N // PROJ_TN),
        in_specs=[pl.BlockSpec((tm, D), lambda i, j: (i, 0)),
                  pl.BlockSpec((None, D, PROJ_TN), lambda i, j: (layer, 0, j))],
        out_specs=pl.BlockSpec((tm, PROJ_TN), lambda i, j: (i, j)),
        out_shape=jax.ShapeDtypeStruct((T, N), BF16),
        compiler_params=_params("parallel", "arbitrary"),
    )(h, w)


def _rope128(r, cosf, sins):
    lane = lax.broadcasted_iota(jnp.int32, r.shape, 1)
    partner = jnp.where(lane < MLA_ROPE // 2,
                        pltpu.roll(r, LANE - MLA_ROPE // 2, axis=1),
                        pltpu.roll(r, MLA_ROPE // 2, axis=1))
    return r * cosf + partner * sins


def _mla_prep_kernel(cq_ref, ckv_ref, small_ref, cos_ref, sin_ref,
                     cqg_ref, ckvg_ref, wuq_ref, wuk_ref, wuv_ref, qg_ref, kg_ref,
                     q_ref, k_ref, vt_ref):
    cosf = cos_ref[...]
    sins = sin_ref[...]
    scale = MLA_QK ** -0.5 * LOG2_E

    cq = cq_ref[...].astype(F32)
    nq = cq * lax.rsqrt(jnp.mean(cq * cq, axis=-1, keepdims=True) + EPS) * cqg_ref[...]
    q_all = _dot(nq.astype(BF16), wuq_ref[...])

    ckv = ckv_ref[...].astype(F32)
    nkv = ckv * lax.rsqrt(jnp.mean(ckv * ckv, axis=-1, keepdims=True) + EPS) * ckvg_ref[...]
    nkv = nkv.astype(BF16)
    k_all = _dot(nkv, wuk_ref[...])
    vt = _dot_nt(wuv_ref[...], nkv)
    ones_row = (lax.broadcasted_iota(jnp.int32, (MLA_VT_ROWS - MLA_V, vt.shape[1]), 0) == 0)
    for hd in range(MLA_HEADS):
        vt_ref[hd, :MLA_V, :] = vt[hd * MLA_V:(hd + 1) * MLA_V, :].astype(vt_ref.dtype)
        vt_ref[hd, MLA_V:, :] = ones_row.astype(vt_ref.dtype)

    qg = qg_ref[...]
    kg = kg_ref[...]
    lane = lax.broadcasted_iota(jnp.int32, (1, LANE), 1)
    kr = jnp.where(lane < MLA_ROPE, small_ref[...], 0.0)
    kr_ss = jnp.sum(kr * kr, axis=-1, keepdims=True)
    kr_rot = _rope128(kr * kg[:, MLA_NOPE:], cosf, sins)

    for hd in range(MLA_HEADS):
        qb = q_all[:, hd * MLA_QK_PAD:(hd + 1) * MLA_QK_PAD]
        rq = lax.rsqrt(jnp.sum(qb * qb, axis=-1, keepdims=True) * (1.0 / MLA_QK) + EPS) * scale
        qn = qb * rq * qg
        q_ref[:, hd * MLA_QK_PAD:hd * MLA_QK_PAD + MLA_NOPE] = qn[:, :MLA_NOPE].astype(q_ref.dtype)
        q_ref[:, hd * MLA_QK_PAD + MLA_NOPE:(hd + 1) * MLA_QK_PAD] = _rope128(
            qn[:, MLA_NOPE:], cosf, sins).astype(q_ref.dtype)

        kb = k_all[:, hd * MLA_NOPE:(hd + 1) * MLA_NOPE]
        rk = lax.rsqrt((jnp.sum(kb * kb, axis=-1, keepdims=True) + kr_ss) * (1.0 / MLA_QK) + EPS)
        k_ref[:, hd * MLA_QK_PAD:hd * MLA_QK_PAD + MLA_NOPE] = (
            kb * rk * kg[:, :MLA_NOPE]).astype(k_ref.dtype)
        k_ref[:, hd * MLA_QK_PAD + MLA_NOPE:(hd + 1) * MLA_QK_PAD] = (kr_rot * rk).astype(k_ref.dtype)


def _mla_prep_call(proj, small, cosf, sins, cqg, ckvg, wuq, wuk, wuv, qg, kg, tm):
    T = proj.shape[0]
    HQ = MLA_HEADS * MLA_QK_PAD
    return pl.pallas_call(
        _mla_prep_kernel,
        grid=(T // tm,),
        in_specs=[pl.BlockSpec((tm, MLA_Q_RANK), lambda i: (i, 0)),
                  pl.BlockSpec((tm, MLA_KV_RANK), lambda i: (i, 1)),
                  pl.BlockSpec((tm, SMALL_W), lambda i: (i, 0)),
                  pl.BlockSpec((tm, LANE), lambda i: (i, 0)),
                  pl.BlockSpec((tm, LANE), lambda i: (i, 0)),
                  _resident((1, MLA_Q_RANK)),
                  _resident((1, MLA_KV_RANK)),
                  _resident((MLA_Q_RANK, HQ)),
                  _resident((MLA_KV_RANK, MLA_HEADS * MLA_NOPE)),
                  _resident((MLA_WIDTH, MLA_KV_RANK)),
                  _resident((1, MLA_QK_PAD)),
                  _resident((1, MLA_QK_PAD))],
        out_specs=[pl.BlockSpec((tm, HQ), lambda i: (i, 0)),
                   pl.BlockSpec((tm, HQ), lambda i: (i, 0)),
                   pl.BlockSpec((None, MLA_HEADS, MLA_VT_ROWS, tm), lambda i: (i, 0, 0, 0))],
        out_shape=[jax.ShapeDtypeStruct((T, HQ), BF16),
                   jax.ShapeDtypeStruct((T, HQ), BF16),
                   jax.ShapeDtypeStruct((T // tm, MLA_HEADS, MLA_VT_ROWS, tm), BF16)],
        compiler_params=_params("parallel"),
    )(proj, proj, small, cosf, sins, cqg, ckvg, wuq, wuk, wuv, qg, kg)


def _attn_kernel(qlo_ref, qhi_ref, k_ref, vt_ref, szlo_ref, szhi_ref, o_ref,
                 q2_ref, s_ref, mf_ref, pv_ref, *, tq, nq, n_steps):
    g = pl.program_id(0)
    half = nq // 2
    nt = nq + 1
    i1 = jnp.minimum(g, n_steps - 1) % half
    i2 = jnp.maximum(g - 1, 0) % half

    @pl.when(g == 0)
    def _():
        for t in range(nt):
            s_ref[t] = jnp.zeros((tq, tq), F32)
            mf_ref[t] = jnp.zeros((SUBLANE, tq), F32)

    q2_ref[0] = qlo_ref[...]
    q2_ref[1] = qhi_ref[...]

    def is_lo(p, i):
        return True if p == 0 else (False if p >= half else p <= i)

    def pick(p, i, lo, hi):
        c = is_lo(p, i)
        if c is True:
            return lo
        if c is False:
            return hi
        return jnp.where(c, lo, hi)

    def kv_tile(p, i):
        if p == 0:
            return i
        if p == nq:
            return nq - 1 - i
        return pick(p, i, p - 1, p - i - 1)

    def combine(ref, shape, i, op, init):
        lo = jnp.full(shape, init, F32)
        hi = jnp.full(shape, init, F32)
        for p in range(nt):
            c = is_lo(p, i)
            x = ref[p]
            if c is True:
                lo = op(lo, x)
            elif c is False:
                hi = op(hi, x)
            else:
                lo = op(lo, jnp.where(c, x, init))
                hi = op(hi, jnp.where(c, init, x))
        return lo, hi

    m_lo, m_hi = combine(mf_ref, (SUBLANE, tq), i2, jnp.maximum, MASK_VALUE)
    m_lo = jnp.max(m_lo, axis=0, keepdims=True)
    m_hi = jnp.max(m_hi, axis=0, keepdims=True)

    causal = (lax.broadcasted_iota(jnp.int32, (tq, tq), 0)
              <= lax.broadcasted_iota(jnp.int32, (tq, tq), 1))
    order_zeros = [jnp.zeros((1, tq), F32)] * ORDER_LAG
    for p in range(nt):
        e = jnp.exp2(s_ref[p] - (pick(p, i2, m_lo, m_hi) + order_zeros[0]))
        pv_ref[p] = _dot(vt_ref[kv_tile(p, i2)], e.astype(BF16))
        j = kv_tile(p, i1)
        s = _dot_nt(k_ref[pl.ds(pl.multiple_of(j * tq, tq), tq), :], q2_ref[pick(p, i1, 0, 1)])
        if p == 0 or p == nq:
            s = jnp.where(causal, s, MASK_VALUE)
        s_ref[p] = s
        mf = jnp.max(s.reshape(tq // SUBLANE, SUBLANE, tq), axis=0)
        mf_ref[p] = mf
        bits = lax.bitcast_convert_type(mf[:1, :], jnp.uint32)
        order_zeros = order_zeros[1:] + [lax.bitcast_convert_type((bits >> 16) >> 16, F32)]

    a_lo, a_hi = combine(pv_ref, (MLA_VT_ROWS, tq), i2, jnp.add, 0.0)
    for a, sz_ref, rows in ((a_lo, szlo_ref, slice(0, tq)), (a_hi, szhi_ref, slice(tq, 2 * tq))):
        ot = a[:MLA_V, :] / a[MLA_V:MLA_V + 1, :]
        o_ref[rows, :] = (ot.T * _silu(sz_ref[...].astype(F32))).astype(o_ref.dtype)


def _attn_call(q, k, vt, p_silu, B, S, tq):
    T = q.shape[0]
    nq = S // tq
    assert nq % 2 == 0 and MLA_V == LANE
    half = nq // 2
    n_steps = B * MLA_HEADS * half
    sz_col0 = SILU_Z_A * PROJ_TN // MLA_V

    def bhi(g):
        bh, i = g // half, g % half
        return bh // MLA_HEADS, bh % MLA_HEADS, i

    def first(g):
        return bhi(jnp.minimum(g, n_steps - 1))

    def second(g):
        return bhi(jnp.maximum(g - 1, 0))

    def lo_blk(f, col0=0):
        def index(g):
            b, h, i = f(g)
            return b * nq + i, col0 + h
        return index

    def hi_blk(f, col0=0):
        def index(g):
            b, h, i = f(g)
            return b * nq + nq - 1 - i, col0 + h
        return index

    def seq_blk(f):
        def index(g):
            b, h, _ = f(g)
            return b, h
        return index

    def seq_blk4(f):
        def index(g):
            b, h, _ = f(g)
            return b, h, 0, 0
        return index

    def out_blk(g):
        b, h, i = second(g)
        return b * half + i, h

    nt = nq + 1
    return pl.pallas_call(
        functools.partial(_attn_kernel, tq=tq, nq=nq, n_steps=n_steps),
        grid=(n_steps + 1,),
        in_specs=[pl.BlockSpec((tq, MLA_QK_PAD), lo_blk(first)),
                  pl.BlockSpec((tq, MLA_QK_PAD), hi_blk(first)),
                  pl.BlockSpec((S, MLA_QK_PAD), seq_blk(first)),
                  pl.BlockSpec((nq, None, MLA_VT_ROWS, tq), seq_blk4(second)),
                  pl.BlockSpec((tq, MLA_V), lo_blk(second, sz_col0)),
                  pl.BlockSpec((tq, MLA_V), hi_blk(second, sz_col0))],
        out_specs=pl.BlockSpec((2 * tq, MLA_V), out_blk),
        out_shape=jax.ShapeDtypeStruct((T, MLA_WIDTH), BF16),
        scratch_shapes=[pltpu.VMEM((2, tq, MLA_QK_PAD), BF16),
                        pltpu.VMEM((nt, tq, tq), F32),
                        pltpu.VMEM((nt, SUBLANE, tq), F32),
                        pltpu.VMEM((nt, MLA_VT_ROWS, tq), F32)],
        compiler_params=_params("arbitrary"),
    )(q, q, k, vt, p_silu, p_silu)


def _paired_block(r, S, tq, tm):
    per_seq, per_tile, nq = S // tm, tq // tm, S // tq
    b, w = r // per_seq, r % per_seq
    u, sub = w // per_tile, w % per_tile
    pos = jnp.where(u < nq // 2, 2 * u, 2 * (nq - 1 - u) + 1)
    return b * per_seq + pos * per_tile + sub


def _sgu_tile(u_ref, v_ref, sz_ref, vg_ref, ws_ref, bs_ref, o_ref, tm):
    v = v_ref[...].astype(F32)
    vn = (v * lax.rsqrt(jnp.mean(v * v, axis=-1, keepdims=True) + EPS) * vg_ref[...]).astype(BF16)
    row = lax.broadcasted_iota(jnp.int32, (SGU_CHUNK, SGU_CHUNK), 0)
    col = lax.broadcasted_iota(jnp.int32, (SGU_CHUNK, SGU_CHUNK), 1)
    gd = SGU_WIDTH // SGU_GROUPS
    for g in range(SGU_GROUPS):
        w = jnp.where(row >= col, ws_ref[g], 0.0).astype(BF16)
        bias = bs_ref[:, g:g + 1]
        for c in range(tm // SGU_CHUNK):
            rs = slice(c * SGU_CHUNK, (c + 1) * SGU_CHUNK)
            cs = slice(g * gd, (g + 1) * gd)
            mix = _dot(w, vn[rs, cs]) + bias
            o_ref[rs, cs] = (u_ref[rs, cs].astype(F32) * mix
                             * _silu(sz_ref[rs, cs].astype(F32))).astype(o_ref.dtype)


def _split_bf16(x):
    hi = x.astype(BF16)
    lo = (x - hi.astype(F32)).astype(BF16)
    return hi, lo


def _gla_kernel(qk_ref, v_ref, sz_ref, small_ref, wa_ref, ba_ref, og_ref, o_ref, st_ref, *, cpb):
    L = GLA_CHUNK

    @pl.when(pl.program_id(1) == 0)
    def _():
        st_ref[...] = jnp.zeros(st_ref.shape, F32)

    a_hi, a_lo = _split_bf16(small_ref[...])
    w_hi, w_lo = _split_bf16(wa_ref[...])
    xg = _dot(a_hi, w_hi) + _dot(a_lo, w_hi) + _dot(a_hi, w_lo) + ba_ref[...]
    log_a = (jnp.minimum(xg, 0.0) - jnp.log(1.0 + jnp.exp(-jnp.abs(xg)))) * (1.0 / GLA_TAU)
    row = lax.broadcasted_iota(jnp.int32, (L, L), 0)
    col = lax.broadcasted_iota(jnp.int32, (L, L), 1)
    causal = row >= col
    ones_tril = causal.astype(BF16)
    og = og_ref[...]
    states = [st_ref[hd] for hd in range(GLA_HEADS)]

    for c in range(cpb):
        rs = slice(c * L, (c + 1) * L)
        la_hi, la_lo = _split_bf16(log_a[rs, :])
        bcum = _dot(ones_tril, la_hi) + _dot(ones_tril, la_lo)
        for hd in range(GLA_HEADS):
            ks = slice(hd * GLA_DK, (hd + 1) * GLA_DK)
            k2 = slice(GLA_HEADS * GLA_DK + hd * GLA_DK, GLA_HEADS * GLA_DK + (hd + 1) * GLA_DK)
            vs = slice(hd * GLA_DV, (hd + 1) * GLA_DV)
            b = bcum[:, ks]
            b_mid = b[L // 2:L // 2 + 1, :]
            b_last = b[L - 1:L, :]
            q = qk_ref[rs, ks].astype(F32) * (GLA_DK ** -0.5)
            k = qk_ref[rs, k2].astype(F32)
            v = v_ref[rs, vs]
            q_t = (q * jnp.exp(b - b_mid)).astype(BF16)
            k_t = (k * jnp.exp(b_mid - b)).astype(BF16)
            attn = jnp.where(causal, _dot_nt(q_t, k_t), 0.0).astype(BF16)
            st = states[hd]
            o = _dot_nt((q * jnp.exp(b)).astype(BF16), st.astype(BF16)) + _dot(attn, v)
            k_s = (k * jnp.exp(b_last - b)).astype(BF16)
            states[hd] = jnp.exp(b_last) * st + _dot_tn(v, k_s)
            on = o * lax.rsqrt(jnp.mean(o * o, axis=-1, keepdims=True) + EPS) * og
            o_ref[rs, vs] = (on * _silu(sz_ref[rs, vs].astype(F32))).astype(o_ref.dtype)

    for hd in range(GLA_HEADS):
        st_ref[hd] = states[hd]


def _gla_call(p_lin, p_silu, small, wa_pad, ba, og, B, S, cpb):
    T = p_lin.shape[0]
    R = cpb * GLA_CHUNK
    ns = S // R
    return pl.pallas_call(
        functools.partial(_gla_kernel, cpb=cpb),
        grid=(B, ns),
        in_specs=[pl.BlockSpec((R, PROJ_TN), lambda b, c: (b * ns + c, LIN_QK_C)),
                  pl.BlockSpec((R, GLA_WIDTH), lambda b, c: (b * ns + c, LIN_V_C)),
                  pl.BlockSpec((R, GLA_WIDTH), lambda b, c: (b * ns + c, SILU_Z_C)),
                  pl.BlockSpec((R, SMALL_W), lambda b, c: (b * ns + c, 0)),
                  _resident((SMALL_W, GLA_HEADS * GLA_DK)),
                  _resident((1, GLA_HEADS * GLA_DK)),
                  _resident((1, GLA_DV))],
        out_specs=pl.BlockSpec((R, GLA_WIDTH), lambda b, c: (b * ns + c, 0)),
        out_shape=jax.ShapeDtypeStruct((T, GLA_WIDTH), BF16),
        scratch_shapes=[pltpu.VMEM((GLA_HEADS, GLA_DV, GLA_DK), F32)],
        compiler_params=_params("parallel", "arbitrary"),
    )(p_lin, p_lin, p_silu, small, wa_pad, ba, og)


def _merge_kernel(ya_ref, u_ref, v_ref, szb_ref, yc_ref, ga_ref, gb_ref, gc_ref,
                  vg_ref, ws_ref, bs_ref, wb_ref, o_ref, yb_ref, *, tm):
    _sgu_tile(u_ref, v_ref, szb_ref, vg_ref, ws_ref, bs_ref, yb_ref, tm)
    merged = _sigmoid(ga_ref[...].astype(F32)) * _dot(ya_ref[...], wb_ref[0])
    merged += _sigmoid(gb_ref[...].astype(F32)) * _dot(yb_ref[...], wb_ref[1])
    merged += _sigmoid(gc_ref[...].astype(F32)) * _dot(yc_ref[...], wb_ref[2])
    o_ref[...] = merged.astype(o_ref.dtype)


def _merge_call(ya, p_gelu, p_silu, yc, p_gate, vg, ws, bs_t, wb, layer, tm, S, tq):
    T = ya.shape[0]
    D = wb.shape[-1]
    assert SGU_WIDTH == BRANCH_WIDTH
    yspec = pl.BlockSpec((tm, BRANCH_WIDTH), lambda i: (i, 0))
    ya_spec = pl.BlockSpec((tm, BRANCH_WIDTH), lambda i: (_paired_block(i, S, tq, tm), 0))
    return pl.pallas_call(
        functools.partial(_merge_kernel, tm=tm),
        grid=(T // tm,),
        in_specs=[ya_spec,
                  pl.BlockSpec((tm, SGU_WIDTH), lambda i: (i, GELU_U_B)),
                  pl.BlockSpec((tm, SGU_WIDTH), lambda i: (i, GELU_V_B)),
                  pl.BlockSpec((tm, SGU_WIDTH), lambda i: (i, SILU_Z_B)),
                  yspec,
                  pl.BlockSpec((tm, D), lambda i: (i, 0)),
                  pl.BlockSpec((tm, D), lambda i: (i, 1)),
                  pl.BlockSpec((tm, D), lambda i: (i, 2)),
                  _resident((1, SGU_WIDTH)),
                  _resident((SGU_GROUPS, SGU_CHUNK, SGU_CHUNK)),
                  _resident((SGU_CHUNK, SGU_GROUPS)),
                  _resident((N_BRANCH, BRANCH_WIDTH, D), layer)],
        out_specs=pl.BlockSpec((tm, D), lambda i: (i, 0)),
        out_shape=jax.ShapeDtypeStruct((T, D), BF16),
        scratch_shapes=[pltpu.VMEM((tm, SGU_WIDTH), BF16)],
        compiler_params=_params("parallel"),
    )(ya, p_gelu, p_gelu, p_silu, yc, p_gate, p_gate, p_gate, vg, ws, bs_t, wb)


def _out_kernel(x_ref, m_ref, wo_ref, o_ref):
    o_ref[...] = x_ref[...] + _dot(m_ref[...], wo_ref[...])


def _out_call(x2, merged, wo, layer, tm):
    T, D = x2.shape
    return pl.pallas_call(
        _out_kernel,
        grid=(T // tm,),
        in_specs=[pl.BlockSpec((tm, D), lambda i: (i, 0)),
                  pl.BlockSpec((tm, D), lambda i: (i, 0)),
                  _resident((D, D), layer)],
        out_specs=pl.BlockSpec((tm, D), lambda i: (i, 0)),
        out_shape=jax.ShapeDtypeStruct((T, D), F32),
        compiler_params=_params("parallel"),
    )(x2, merged, wo)


_IN_SIZES = (("c_q", MLA_Q_RANK), ("c_kv", MLA_KV_RANK), ("k_rope", MLA_ROPE), ("z_a", MLA_WIDTH),
             ("u_b", SGU_WIDTH), ("v_b", SGU_WIDTH), ("z_b", SGU_WIDTH),
             ("q_c", GLA_HEADS * GLA_DK), ("k_c", GLA_HEADS * GLA_DK), ("v_c", GLA_WIDTH),
             ("a_r", GLA_GATE_RANK), ("z_c", GLA_WIDTH), ("gates", N_BRANCH * D_MODEL))
_IN_SEG = {}
_off = 0
for _name, _size in _IN_SIZES:
    _IN_SEG[_name] = (_off, _size)
    _off += _size
IN_COLS = _off
_W_GROUPS = (("c_q", "c_kv", "q_c", "k_c", "v_c"), ("z_a", "z_b", "z_c"), ("u_b", "v_b"),
             ("gates",), ("k_rope", "a_r"))


def _wprep_kernel(w_ref, lin_ref, silu_ref, gelu_ref, gate_ref, small_ref):
    small_ref[...] = jnp.zeros(small_ref.shape, small_ref.dtype)
    for o_ref, names in zip((lin_ref, silu_ref, gelu_ref, gate_ref, small_ref), _W_GROUPS):
        dst = 0
        for name in names:
            src, size = _IN_SEG[name]
            o_ref[:, dst:dst + size] = w_ref[:, src:src + size].astype(o_ref.dtype)
            dst += size


def _split_w_in(w_in, tr):
    depth, D, cols = w_in.shape
    assert cols == IN_COLS
    widths = [sum(_IN_SEG[n][1] for n in names) for names in _W_GROUPS[:-1]] + [SMALL_W]
    nr = D // tr
    return pl.pallas_call(
        _wprep_kernel,
        grid=(depth, nr),
        in_specs=[pl.BlockSpec((tr, cols), lambda l, r: (l * nr + r, 0))],
        out_specs=[pl.BlockSpec((None, tr, n), lambda l, r: (l, r, 0)) for n in widths],
        out_shape=[jax.ShapeDtypeStruct((depth, D, n), BF16) for n in widths],
        compiler_params=_params("parallel", "parallel"),
    )(w_in.reshape(depth * D, cols))


def _pad_heads(w, per_head, lo, hi, width):
    r = w.shape[0]
    w = w.reshape(r, MLA_HEADS, per_head)[:, :, lo:hi]
    w = jnp.pad(w, ((0, 0), (0, 0), (0, width - (hi - lo))))
    return w.reshape(r, MLA_HEADS * width)


def _rope_tables(positions):
    half = MLA_ROPE // 2
    inv_freq = 1.0 / (ROPE_THETA ** (jnp.arange(0, MLA_ROPE, 2, dtype=F32) / MLA_ROPE))
    ang = positions.astype(F32).reshape(-1, 1) * inv_freq
    cos, sin = jnp.cos(ang), jnp.sin(ang)
    z = jnp.zeros((ang.shape[0], LANE - 2 * half), F32)
    return (jnp.concatenate([cos, cos, z], axis=1), jnp.concatenate([-sin, sin, z], axis=1))


def _pick_tile(n, want):
    t = min(n, want)
    while n % t:
        t //= 2
    return t


def kernel(x, positions, norm_g, w_in, mla_cq_norm, mla_ckv_norm, mla_w_uq, mla_w_ukv,
           mla_q_norm, mla_k_norm, sgu_v_norm, sgu_w_s, sgu_b_s, gla_w_a2, gla_b_a,
           gla_o_norm, w_branch, w_out):
    B, S, D = x.shape
    T = B * S
    depth = w_in.shape[0]
    tm_proj = _pick_tile(T, 2048)
    tm_row = _pick_tile(T, 512)
    tq = _pick_tile(S, 512)
    cpb = _pick_tile(S // GLA_CHUNK, 8)

    cosf, sins = _rope_tables(positions)
    x2 = x.reshape(T, D)
    w_lin, w_silu, w_gelu, w_gate, w_small = _split_w_in(w_in, _pick_tile(D, 128))
    wb, wo = w_branch.astype(BF16), w_out.astype(BF16)
    for l in range(depth):
        wuq = _pad_heads(mla_w_uq[l], MLA_QK, 0, MLA_QK, MLA_QK_PAD).astype(BF16)
        wuk = _pad_heads(mla_w_ukv[l], MLA_NOPE + MLA_V, 0, MLA_NOPE, MLA_NOPE).astype(BF16)
        wuv = _pad_heads(mla_w_ukv[l], MLA_NOPE + MLA_V, MLA_NOPE, MLA_NOPE + MLA_V, MLA_V).T.astype(BF16)
        qg = jnp.pad(mla_q_norm[l], (0, MLA_QK_PAD - MLA_QK)).reshape(1, MLA_QK_PAD)
        kg = jnp.pad(mla_k_norm[l], (0, MLA_QK_PAD - MLA_QK)).reshape(1, MLA_QK_PAD)
        wa_pad = jnp.zeros((SMALL_W, GLA_HEADS * GLA_DK), F32).at[
            MLA_ROPE:MLA_ROPE + GLA_GATE_RANK].set(gla_w_a2[l])

        h, small = _norm_call(x2, norm_g[l].reshape(1, D), w_small, l, tm_row)
        p_lin = _inproj_call(h, w_lin, l, _identity, tm_proj)
        p_silu = _inproj_call(h, w_silu, l, _identity, tm_proj)
        p_gelu = _inproj_call(h, w_gelu, l, _gelu_tanh, tm_proj)
        p_gate = _inproj_call(h, w_gate, l, _identity, tm_proj)
        q, k, vt = _mla_prep_call(p_lin, small, cosf, sins,
                                  mla_cq_norm[l].reshape(1, -1), mla_ckv_norm[l].reshape(1, -1),
                                  wuq, wuk, wuv, qg, kg, tq)
        ya = _attn_call(q, k, vt, p_silu, B, S, tq)
        yc = _gla_call(p_lin, p_silu, small, wa_pad, gla_b_a[l].reshape(1, -1),
                       gla_o_norm[l].reshape(1, -1), B, S, cpb)
        merged = _merge_call(ya, p_gelu, p_silu, yc, p_gate, sgu_v_norm[l].reshape(1, -1), sgu_w_s[l],
                             sgu_b_s[l].T, wb, l, tm_row, S, tq)
        x2 = _out_call(x2, merged, wo, l, tm_row)
    return x2.reshape(B, S, D)
```

```python
import functools
import math

import jax
import jax.numpy as jnp
from jax import lax
from jax.experimental import pallas as pl
from jax.experimental.pallas import tpu as pltpu

F32 = jnp.float32
BF16 = jnp.bfloat16

MLA_HEADS = 8
MLA_NOPE = 128
MLA_ROPE = 64
MLA_QK = MLA_NOPE + MLA_ROPE
MLA_V = 128
MLA_Q_RANK = 512
MLA_KV_RANK = 512
MLA_WIDTH = MLA_HEADS * MLA_V
MLA_QK_PAD = 256
MLA_VT_ROWS = MLA_V + 16
SUBLANE = 8
ORDER_LAG = 3
ROPE_THETA = 10000.0
SGU_GROUPS = 8
SGU_CHUNK = 128
SGU_WIDTH = 1024
GLA_HEADS = 4
GLA_DK = 128
GLA_DV = 256
GLA_GATE_RANK = 16
GLA_TAU = 16.0
GLA_CHUNK = 128
GLA_WIDTH = GLA_HEADS * GLA_DV
D_MODEL = 2048
N_BRANCH = 3
BRANCH_WIDTH = 1024
EPS = 1e-6
LANE = 128
LOG2_E = math.log2(math.e)
MASK_VALUE = -1e30
SMALL_W = LANE

PROJ_TN = 1024
LIN_CQKV, LIN_QK_C, LIN_V_C = range(3)
SILU_Z_A, SILU_Z_B, SILU_Z_C = range(3)
GELU_U_B, GELU_V_B = range(2)

VMEM_LIMIT = 56 * 1024 * 1024


def _params(*sem):
    return pltpu.CompilerParams(dimension_semantics=sem, vmem_limit_bytes=VMEM_LIMIT)


def _dot(a, b):
    return jnp.dot(a, b, preferred_element_type=F32)


def _dot_nt(a, b):
    return lax.dot_general(a, b, (((1,), (1,)), ((), ())), preferred_element_type=F32)


def _dot_tn(a, b):
    return lax.dot_general(a, b, (((0,), (0,)), ((), ())), preferred_element_type=F32)


def _sigmoid(x):
    return 0.5 * jnp.tanh(0.5 * x) + 0.5


def _gelu_tanh(x):
    c = math.sqrt(2.0 / math.pi)
    hx = 0.5 * x
    return hx * jnp.tanh(x * (c + (c * 0.044715) * (x * x))) + hx


def _silu(x):
    hx = 0.5 * x
    return hx * jnp.tanh(hx) + hx


def _identity(x):
    return x


def _resident(shape, layer=None):
    nd = len(shape)
    if layer is None:
        return pl.BlockSpec(shape, lambda *_: (0,) * nd, pipeline_mode=pl.Buffered(1))
    return pl.BlockSpec((None,) + tuple(shape), lambda *_: (layer,) + (0,) * nd,
                        pipeline_mode=pl.Buffered(1))


def _norm_kernel(x_ref, g_ref, ws_ref, h_ref, small_ref):
    x = x_ref[...]
    rstd = lax.rsqrt(jnp.mean(x * x, axis=-1, keepdims=True) + EPS)
    h = (x * rstd * g_ref[...]).astype(BF16)
    h_ref[...] = h
    small_ref[...] = _dot_nt(h, ws_ref[...])


def _norm_call(x2, g, w_small, layer, tm):
    T, D = x2.shape
    return pl.pallas_call(
        _norm_kernel,
        grid=(T // tm,),
        in_specs=[pl.BlockSpec((tm, D), lambda i: (i, 0)),
                  _resident((1, D)),
                  _resident((SMALL_W, D), layer)],
        out_specs=[pl.BlockSpec((tm, D), lambda i: (i, 0)),
                   pl.BlockSpec((tm, SMALL_W), lambda i: (i, 0))],
        out_shape=[jax.ShapeDtypeStruct((T, D), BF16),
                   jax.ShapeDtypeStruct((T, SMALL_W), F32)],
        compiler_params=_params("parallel"),
    )(x2, g, w_small)


def _inproj_kernel(h_ref, w_ref, o_ref, *, act):
    o_ref[...] = act(_dot_nt(h_ref[...], w_ref[...])).astype(o_ref.dtype)


def _inproj_call(h, w, layer, act, tm):
    T, D = h.shape
    N = w.shape[1]
    return pl.pallas_call(
        functools.partial(_inproj_kernel, act=act),
        grid=(T // tm, N // PROJ_TN),
        in_specs=[pl.BlockSpec((tm, D), lambda i, j: (i, 0)),
                  pl.BlockSpec((None, PROJ_TN, D), lambda i, j: (layer, j, 0))],
        out_specs=pl.BlockSpec((tm, PROJ_TN), lambda i, j: (i, j)),
        out_shape=jax.ShapeDtypeStruct((T, N), BF16),
        compiler_params=_params("parallel", "arbitrary"),
    )(h, w)


def _rope128(r, cosf, sins):
    lane = lax.broadcasted_iota(jnp.int32, r.shape, 1)
    partner = jnp.where(lane < MLA_ROPE // 2,
                        pltpu.roll(r, LANE - MLA_ROPE // 2, axis=1),
                        pltpu.roll(r, MLA_ROPE // 2, axis=1))
    return r * cosf + partner * sins


def _mla_prep_kernel(cq_ref, ckv_ref, small_ref, cos_ref, sin_ref,
                     cqg_ref, ckvg_ref, wuq_ref, wuk_ref, wuv_ref, qg_ref, kg_ref,
                     q_ref, k_ref, vt_ref):
    cosf = cos_ref[...]
    sins = sin_ref[...]
    scale = MLA_QK ** -0.5 * LOG2_E

    cq = cq_ref[...].astype(F32)
    nq = cq * lax.rsqrt(jnp.mean(cq * cq, axis=-1, keepdims=True) + EPS) * cqg_ref[...]
    q_all = _dot(nq.astype(BF16), wuq_ref[...])

    ckv = ckv_ref[...].astype(F32)
    nkv = ckv * lax.rsqrt(jnp.mean(ckv * ckv, axis=-1, keepdims=True) + EPS) * ckvg_ref[...]
    nkv = nkv.astype(BF16)
    k_all = _dot(nkv, wuk_ref[...])
    vt = _dot_nt(wuv_ref[...], nkv)
    ones_row = (lax.broadcasted_iota(jnp.int32, (MLA_VT_ROWS - MLA_V, vt.shape[1]), 0) == 0)
    for hd in range(MLA_HEADS):
        vt_ref[hd, :MLA_V, :] = vt[hd * MLA_V:(hd + 1) * MLA_V, :].astype(vt_ref.dtype)
        vt_ref[hd, MLA_V:, :] = ones_row.astype(vt_ref.dtype)

    qg = qg_ref[...]
    kg = kg_ref[...]
    lane = lax.broadcasted_iota(jnp.int32, (1, LANE), 1)
    kr = jnp.where(lane < MLA_ROPE, small_ref[...], 0.0)
    kr_ss = jnp.sum(kr * kr, axis=-1, keepdims=True)
    kr_rot = _rope128(kr * kg[:, MLA_NOPE:], cosf, sins)

    for hd in range(MLA_HEADS):
        qb = q_all[:, hd * MLA_QK_PAD:(hd + 1) * MLA_QK_PAD]
        rq = lax.rsqrt(jnp.sum(qb * qb, axis=-1, keepdims=True) * (1.0 / MLA_QK) + EPS) * scale
        qn = qb * rq * qg
        q_ref[:, hd * MLA_QK_PAD:hd * MLA_QK_PAD + MLA_NOPE] = qn[:, :MLA_NOPE].astype(q_ref.dtype)
        q_ref[:, hd * MLA_QK_PAD + MLA_NOPE:(hd + 1) * MLA_QK_PAD] = _rope128(
            qn[:, MLA_NOPE:], cosf, sins).astype(q_ref.dtype)

        kb = k_all[:, hd * MLA_NOPE:(hd + 1) * MLA_NOPE]
        rk = lax.rsqrt((jnp.sum(kb * kb, axis=-1, keepdims=True) + kr_ss) * (1.0 / MLA_QK) + EPS)
        k_ref[:, hd * MLA_QK_PAD:hd * MLA_QK_PAD + MLA_NOPE] = (
            kb * rk * kg[:, :MLA_NOPE]).astype(k_ref.dtype)
        k_ref[:, hd * MLA_QK_PAD + MLA_NOPE:(hd + 1) * MLA_QK_PAD] = (kr_rot * rk).astype(k_ref.dtype)


def _mla_prep_call(proj, small, cosf, sins, cqg, ckvg, wuq, wuk, wuv, qg, kg, tm):
    T = proj.shape[0]
    HQ = MLA_HEADS * MLA_QK_PAD
    return pl.pallas_call(
        _mla_prep_kernel,
        grid=(T // tm,),
        in_specs=[pl.BlockSpec((tm, MLA_Q_RANK), lambda i: (i, 0)),
                  pl.BlockSpec((tm, MLA_KV_RANK), lambda i: (i, 1)),
                  pl.BlockSpec((tm, SMALL_W), lambda i: (i, 0)),
                  pl.BlockSpec((tm, LANE), lambda i: (i, 0)),
                  pl.BlockSpec((tm, LANE), lambda i: (i, 0)),
                  _resident((1, MLA_Q_RANK)),
                  _resident((1, MLA_KV_RANK)),
                  _resident((MLA_Q_RANK, HQ)),
                  _resident((MLA_KV_RANK, MLA_HEADS * MLA_NOPE)),
                  _resident((MLA_WIDTH, MLA_KV_RANK)),
                  _resident((1, MLA_QK_PAD)),
                  _resident((1, MLA_QK_PAD))],
        out_specs=[pl.BlockSpec((tm, HQ), lambda i: (i, 0)),
                   pl.BlockSpec((tm, HQ), lambda i: (i, 0)),
                   pl.BlockSpec((None, MLA_HEADS, MLA_VT_ROWS, tm), lambda i: (i, 0, 0, 0))],
        out_shape=[jax.ShapeDtypeStruct((T, HQ), BF16),
                   jax.ShapeDtypeStruct((T, HQ), BF16),
                   jax.ShapeDtypeStruct((T // tm, MLA_HEADS, MLA_VT_ROWS, tm), BF16)],
        compiler_params=_params("parallel"),
    )(proj, proj, small, cosf, sins, cqg, ckvg, wuq, wuk, wuv, qg, kg)


def _attn_kernel(qlo_ref, qhi_ref, k_ref, vt_ref, szlo_ref, szhi_ref, o_ref,
                 q2_ref, s_ref, mf_ref, pv_ref, *, tq, nq, n_steps):
    g = pl.program_id(0)
    half = nq // 2
    nt = nq + 1
    i1 = jnp.minimum(g, n_steps - 1) % half
    i2 = jnp.maximum(g - 1, 0) % half

    @pl.when(g == 0)
    def _():
        for t in range(nt):
            s_ref[t] = jnp.zeros((tq, tq), F32)
            mf_ref[t] = jnp.zeros((SUBLANE, tq), F32)

    q2_ref[0] = qlo_ref[...]
    q2_ref[1] = qhi_ref[...]

    def is_lo(p, i):
        return True if p == 0 else (False if p >= half else p <= i)

    def pick(p, i, lo, hi):
        c = is_lo(p, i)
        if c is True:
            return lo
        if c is False:
            return hi
        return jnp.where(c, lo, hi)

    def kv_tile(p, i):
        if p == 0:
            return i
        if p == nq:
            return nq - 1 - i
        return pick(p, i, p - 1, p - i - 1)

    def combine(ref, shape, i, op, init):
        lo = jnp.full(shape, init, F32)
        hi = jnp.full(shape, init, F32)
        for p in range(nt):
            c = is_lo(p, i)
            x = ref[p]
            if c is True:
                lo = op(lo, x)
            elif c is False:
                hi = op(hi, x)
            else:
                lo = op(lo, jnp.where(c, x, init))
                hi = op(hi, jnp.where(c, init, x))
        return lo, hi

    m_lo, m_hi = combine(mf_ref, (SUBLANE, tq), i2, jnp.maximum, MASK_VALUE)
    m_lo = jnp.max(m_lo, axis=0, keepdims=True)
    m_hi = jnp.max(m_hi, axis=0, keepdims=True)

    causal = (lax.broadcasted_iota(jnp.int32, (tq, tq), 0)
              <= lax.broadcasted_iota(jnp.int32, (tq, tq), 1))
    order_zeros = [jnp.zeros((1, tq), F32)] * ORDER_LAG
    for p in range(nt):
        e = jnp.exp2(s_ref[p] - (pick(p, i2, m_lo, m_hi) + order_zeros[0]))
        pv_ref[p] = _dot(vt_ref[kv_tile(p, i2)], e.astype(BF16))
        j = kv_tile(p, i1)
        s = _dot_nt(k_ref[pl.ds(pl.multiple_of(j * tq, tq), tq), :], q2_ref[pick(p, i1, 0, 1)])
        if p == 0 or p == nq:
            s = jnp.where(causal, s, MASK_VALUE)
        s_ref[p] = s
        mf = jnp.max(s.reshape(tq // SUBLANE, SUBLANE, tq), axis=0)
        mf_ref[p] = mf
        bits = lax.bitcast_convert_type(mf[:1, :], jnp.uint32)
        order_zeros = order_zeros[1:] + [lax.bitcast_convert_type((bits >> 16) >> 16, F32)]

    a_lo, a_hi = combine(pv_ref, (MLA_VT_ROWS, tq), i2, jnp.add, 0.0)
    for a, sz_ref, rows in ((a_lo, szlo_ref, slice(0, tq)), (a_hi, szhi_ref, slice(tq, 2 * tq))):
        ot = a[:MLA_V, :] / a[MLA_V:MLA_V + 1, :]
        o_ref[rows, :] = (ot.T * _silu(sz_ref[...].astype(F32))).astype(o_ref.dtype)


def _attn_call(q, k, vt, p_silu, B, S, tq):
    T = q.shape[0]
    nq = S // tq
    assert nq % 2 == 0 and MLA_V == LANE
    half = nq // 2
    n_steps = B * MLA_HEADS * half
    sz_col0 = SILU_Z_A * PROJ_TN // MLA_V

    def bhi(g):
        bh, i = g // half, g % half
        return bh // MLA_HEADS, bh % MLA_HEADS, i

    def first(g):
        return bhi(jnp.minimum(g, n_steps - 1))

    def second(g):
        return bhi(jnp.maximum(g - 1, 0))

    def lo_blk(f, col0=0):
        def index(g):
            b, h, i = f(g)
            return b * nq + i, col0 + h
        return index

    def hi_blk(f, col0=0):
        def index(g):
            b, h, i = f(g)
            return b * nq + nq - 1 - i, col0 + h
        return index

    def seq_blk(f):
        def index(g):
            b, h, _ = f(g)
            return b, h
        return index

    def seq_blk4(f):
        def index(g):
            b, h, _ = f(g)
            return b, h, 0, 0
        return index

    def out_blk(g):
        b, h, i = second(g)
        return b * half + i, h

    nt = nq + 1
    return pl.pallas_call(
        functools.partial(_attn_kernel, tq=tq, nq=nq, n_steps=n_steps),
        grid=(n_steps + 1,),
        in_specs=[pl.BlockSpec((tq, MLA_QK_PAD), lo_blk(first)),
                  pl.BlockSpec((tq, MLA_QK_PAD), hi_blk(first)),
                  pl.BlockSpec((S, MLA_QK_PAD), seq_blk(first)),
                  pl.BlockSpec((nq, None, MLA_VT_ROWS, tq), seq_blk4(second)),
                  pl.BlockSpec((tq, MLA_V), lo_blk(second, sz_col0)),
                  pl.BlockSpec((tq, MLA_V), hi_blk(second, sz_col0))],
        out_specs=pl.BlockSpec((2 * tq, MLA_V), out_blk),
        out_shape=jax.ShapeDtypeStruct((T, MLA_WIDTH), BF16),
        scratch_shapes=[pltpu.VMEM((2, tq, MLA_QK_PAD), BF16),
                        pltpu.VMEM((nt, tq, tq), F32),
                        pltpu.VMEM((nt, SUBLANE, tq), F32),
                        pltpu.VMEM((nt, MLA_VT_ROWS, tq), F32)],
        compiler_params=_params("arbitrary"),
    )(q, q, k, vt, p_silu, p_silu)


def _paired_block(r, S, tq, tm):
    per_seq, per_tile, nq = S // tm, tq // tm, S // tq
    b, w = r // per_seq, r % per_seq
    u, sub = w // per_tile, w % per_tile
    pos = jnp.where(u < nq // 2, 2 * u, 2 * (nq - 1 - u) + 1)
    return b * per_seq + pos * per_tile + sub


def _sgu_tile(u_ref, v_ref, sz_ref, vg_ref, ws_ref, bs_ref, o_ref, tm):
    v = v_ref[...].astype(F32)
    vn = (v * lax.rsqrt(jnp.mean(v * v, axis=-1, keepdims=True) + EPS) * vg_ref[...]).astype(BF16)
    row = lax.broadcasted_iota(jnp.int32, (SGU_CHUNK, SGU_CHUNK), 0)
    col = lax.broadcasted_iota(jnp.int32, (SGU_CHUNK, SGU_CHUNK), 1)
    gd = SGU_WIDTH // SGU_GROUPS
    for g in range(SGU_GROUPS):
        w = jnp.where(row >= col, ws_ref[g], 0.0).astype(BF16)
        bias = bs_ref[:, g:g + 1]
        for c in range(tm // SGU_CHUNK):
            rs = slice(c * SGU_CHUNK, (c + 1) * SGU_CHUNK)
            cs = slice(g * gd, (g + 1) * gd)
            mix = _dot(w, vn[rs, cs]) + bias
            o_ref[rs, cs] = (u_ref[rs, cs].astype(F32) * mix
                             * _silu(sz_ref[rs, cs].astype(F32))).astype(o_ref.dtype)


def _split_bf16(x):
    hi = x.astype(BF16)
    lo = (x - hi.astype(F32)).astype(BF16)
    return hi, lo


def _gla_kernel(qk_ref, v_ref, sz_ref, small_ref, wa_ref, ba_ref, og_ref, o_ref, st_ref, *, cpb):
    L = GLA_CHUNK

    @pl.when(pl.program_id(1) == 0)
    def _():
        st_ref[...] = jnp.zeros(st_ref.shape, F32)

    a_hi, a_lo = _split_bf16(small_ref[...])
    w_hi, w_lo = _split_bf16(wa_ref[...])
    xg = _dot(a_hi, w_hi) + _dot(a_lo, w_hi) + _dot(a_hi, w_lo) + ba_ref[...]
    log_a = (jnp.minimum(xg, 0.0) - jnp.log(1.0 + jnp.exp(-jnp.abs(xg)))) * (1.0 / GLA_TAU)
    row = lax.broadcasted_iota(jnp.int32, (L, L), 0)
    col = lax.broadcasted_iota(jnp.int32, (L, L), 1)
    causal = row >= col
    ones_tril = causal.astype(BF16)
    og = og_ref[...]
    states = [st_ref[hd] for hd in range(GLA_HEADS)]

    for c in range(cpb):
        rs = slice(c * L, (c + 1) * L)
        la_hi, la_lo = _split_bf16(log_a[rs, :])
        bcum = _dot(ones_tril, la_hi) + _dot(ones_tril, la_lo)
        for hd in range(GLA_HEADS):
            ks = slice(hd * GLA_DK, (hd + 1) * GLA_DK)
            k2 = slice(GLA_HEADS * GLA_DK + hd * GLA_DK, GLA_HEADS * GLA_DK + (hd + 1) * GLA_DK)
            vs = slice(hd * GLA_DV, (hd + 1) * GLA_DV)
            b = bcum[:, ks]
            b_mid = b[L // 2:L // 2 + 1, :]
            b_last = b[L - 1:L, :]
            q = qk_ref[rs, ks].astype(F32) * (GLA_DK ** -0.5)
            k = qk_ref[rs, k2].astype(F32)
            v = v_ref[rs, vs]
            q_t = (q * jnp.exp(b - b_mid)).astype(BF16)
            k_t = (k * jnp.exp(b_mid - b)).astype(BF16)
            attn = jnp.where(causal, _dot_nt(q_t, k_t), 0.0).astype(BF16)
            st = states[hd]
            o = _dot_nt((q * jnp.exp(b)).astype(BF16), st.astype(BF16)) + _dot(attn, v)
            k_s = (k * jnp.exp(b_last - b)).astype(BF16)
            states[hd] = jnp.exp(b_last) * st + _dot_tn(v, k_s)
            on = o * lax.rsqrt(jnp.mean(o * o, axis=-1, keepdims=True) + EPS) * og
            o_ref[rs, vs] = (on * _silu(sz_ref[rs, vs].astype(F32))).astype(o_ref.dtype)

    for hd in range(GLA_HEADS):
        st_ref[hd] = states[hd]


def _gla_call(p_lin, p_silu, small, wa_pad, ba, og, B, S, cpb):
    T = p_lin.shape[0]
    R = cpb * GLA_CHUNK
    ns = S // R
    return pl.pallas_call(
        functools.partial(_gla_kernel, cpb=cpb),
        grid=(B, ns),
        in_specs=[pl.BlockSpec((R, PROJ_TN), lambda b, c: (b * ns + c, LIN_QK_C)),
                  pl.BlockSpec((R, GLA_WIDTH), lambda b, c: (b * ns + c, LIN_V_C)),
                  pl.BlockSpec((R, GLA_WIDTH), lambda b, c: (b * ns + c, SILU_Z_C)),
                  pl.BlockSpec((R, SMALL_W), lambda b, c: (b * ns + c, 0)),
                  _resident((SMALL_W, GLA_HEADS * GLA_DK)),
                  _resident((1, GLA_HEADS * GLA_DK)),
                  _resident((1, GLA_DV))],
        out_specs=pl.BlockSpec((R, GLA_WIDTH), lambda b, c: (b * ns + c, 0)),
        out_shape=jax.ShapeDtypeStruct((T, GLA_WIDTH), BF16),
        scratch_shapes=[pltpu.VMEM((GLA_HEADS, GLA_DV, GLA_DK), F32)],
        compiler_params=_params("parallel", "arbitrary"),
    )(p_lin, p_lin, p_silu, small, wa_pad, ba, og)


def _merge_kernel(ya_ref, u_ref, v_ref, szb_ref, yc_ref, ga_ref, gb_ref, gc_ref,
                  vg_ref, ws_ref, bs_ref, wb_ref, o_ref, yb_ref, *, tm):
    _sgu_tile(u_ref, v_ref, szb_ref, vg_ref, ws_ref, bs_ref, yb_ref, tm)
    merged = _sigmoid(ga_ref[...].astype(F32)) * _dot(ya_ref[...], wb_ref[0])
    merged += _sigmoid(gb_ref[...].astype(F32)) * _dot(yb_ref[...], wb_ref[1])
    merged += _sigmoid(gc_ref[...].astype(F32)) * _dot(yc_ref[...], wb_ref[2])
    o_ref[...] = merged.astype(o_ref.dtype)


def _merge_call(ya, p_gelu, p_silu, yc, p_gate, vg, ws, bs_t, wb, layer, tm, S, tq):
    T = ya.shape[0]
    D = wb.shape[-1]
    assert SGU_WIDTH == BRANCH_WIDTH
    yspec = pl.BlockSpec((tm, BRANCH_WIDTH), lambda i: (i, 0))
    ya_spec = pl.BlockSpec((tm, BRANCH_WIDTH), lambda i: (_paired_block(i, S, tq, tm), 0))
    return pl.pallas_call(
        functools.partial(_merge_kernel, tm=tm),
        grid=(T // tm,),
        in_specs=[ya_spec,
                  pl.BlockSpec((tm, SGU_WIDTH), lambda i: (i, GELU_U_B)),
                  pl.BlockSpec((tm, SGU_WIDTH), lambda i: (i, GELU_V_B)),
                  pl.BlockSpec((tm, SGU_WIDTH), lambda i: (i, SILU_Z_B)),
                  yspec,
                  pl.BlockSpec((tm, D), lambda i: (i, 0)),
                  pl.BlockSpec((tm, D), lambda i: (i, 1)),
                  pl.BlockSpec((tm, D), lambda i: (i, 2)),
                  _resident((1, SGU_WIDTH)),
                  _resident((SGU_GROUPS, SGU_CHUNK, SGU_CHUNK)),
                  _resident((SGU_CHUNK, SGU_GROUPS)),
                  _resident((N_BRANCH, BRANCH_WIDTH, D), layer)],
        out_specs=pl.BlockSpec((tm, D), lambda i: (i, 0)),
        out_shape=jax.ShapeDtypeStruct((T, D), BF16),
        scratch_shapes=[pltpu.VMEM((tm, SGU_WIDTH), BF16)],
        compiler_params=_params("parallel"),
    )(ya, p_gelu, p_gelu, p_silu, yc, p_gate, p_gate, p_gate, vg, ws, bs_t, wb)


def _out_kernel(x_ref, m_ref, wo_ref, o_ref):
    o_ref[...] = x_ref[...] + _dot(m_ref[...], wo_ref[...])


def _out_call(x2, merged, wo, layer, tm):
    T, D = x2.shape
    return pl.pallas_call(
        _out_kernel,
        grid=(T // tm,),
        in_specs=[pl.BlockSpec((tm, D), lambda i: (i, 0)),
                  pl.BlockSpec((tm, D), lambda i: (i, 0)),
                  _resident((D, D), layer)],
        out_specs=pl.BlockSpec((tm, D), lambda i: (i, 0)),
        out_shape=jax.ShapeDtypeStruct((T, D), F32),
        compiler_params=_params("parallel"),
    )(x2, merged, wo)


_IN_SIZES = (("c_q", MLA_Q_RANK), ("c_kv", MLA_KV_RANK), ("k_rope", MLA_ROPE), ("z_a", MLA_WIDTH),
             ("u_b", SGU_WIDTH), ("v_b", SGU_WIDTH), ("z_b", SGU_WIDTH),
             ("q_c", GLA_HEADS * GLA_DK), ("k_c", GLA_HEADS * GLA_DK), ("v_c", GLA_WIDTH),
             ("a_r", GLA_GATE_RANK), ("z_c", GLA_WIDTH), ("gates", N_BRANCH * D_MODEL))
_IN_SEG = {}
_off = 0
for _name, _size in _IN_SIZES:
    _IN_SEG[_name] = (_off, _size)
    _off += _size
IN_COLS = _off
_W_GROUPS = (("c_q", "c_kv", "q_c", "k_c", "v_c"), ("z_a", "z_b", "z_c"), ("u_b", "v_b"),
             ("gates",), ("k_rope", "a_r"))


def _wprep_kernel(w_ref, lin_ref, silu_ref, gelu_ref, gate_ref, small_ref):
    small_ref[...] = jnp.zeros(small_ref.shape, small_ref.dtype)
    for o_ref, names in zip((lin_ref, silu_ref, gelu_ref, gate_ref, small_ref), _W_GROUPS):
        dst = 0
        for name in names:
            src, size = _IN_SEG[name]
            o_ref[dst:dst + size, :] = w_ref[src:src + size, :].astype(o_ref.dtype)
            dst += size


def _split_w_in(w_in, tk):
    depth, D, cols = w_in.shape
    assert cols == IN_COLS
    widths = [sum(_IN_SEG[n][1] for n in names) for names in _W_GROUPS[:-1]] + [SMALL_W]
    return pl.pallas_call(
        _wprep_kernel,
        grid=(depth, D // tk),
        in_specs=[pl.BlockSpec((None, cols, tk), lambda l, r: (l, 0, r))],
        out_specs=[pl.BlockSpec((None, n, tk), lambda l, r: (l, 0, r)) for n in widths],
        out_shape=[jax.ShapeDtypeStruct((depth, n, D), BF16) for n in widths],
        compiler_params=_params("parallel", "parallel"),
    )(jnp.swapaxes(w_in, 1, 2))


def _pad_heads(w, per_head, lo, hi, width):
    r = w.shape[0]
    w = w.reshape(r, MLA_HEADS, per_head)[:, :, lo:hi]
    w = jnp.pad(w, ((0, 0), (0, 0), (0, width - (hi - lo))))
    return w.reshape(r, MLA_HEADS * width)


def _rope_tables(positions):
    half = MLA_ROPE // 2
    inv_freq = 1.0 / (ROPE_THETA ** (jnp.arange(0, MLA_ROPE, 2, dtype=F32) / MLA_ROPE))
    ang = positions.astype(F32).reshape(-1, 1) * inv_freq
    cos, sin = jnp.cos(ang), jnp.sin(ang)
    z = jnp.zeros((ang.shape[0], LANE - 2 * half), F32)
    return (jnp.concatenate([cos, cos, z], axis=1), jnp.concatenate([-sin, sin, z], axis=1))


def _pick_tile(n, want):
    t = min(n, want)
    while n % t:
        t //= 2
    return t


def kernel(x, positions, norm_g, w_in, mla_cq_norm, mla_ckv_norm, mla_w_uq, mla_w_ukv,
           mla_q_norm, mla_k_norm, sgu_v_norm, sgu_w_s, sgu_b_s, gla_w_a2, gla_b_a,
           gla_o_norm, w_branch, w_out):
    B, S, D = x.shape
    T = B * S
    depth = w_in.shape[0]
    tm_proj = _pick_tile(T, 2048)
    tm_row = _pick_tile(T, 512)
    tq = _pick_tile(S, 512)
    cpb = _pick_tile(S // GLA_CHUNK, 8)

    cosf, sins = _rope_tables(positions)
    x2 = x.reshape(T, D)
    w_lin, w_silu, w_gelu, w_gate, w_small = _split_w_in(w_in, _pick_tile(D, 256))
    wb, wo = w_branch.astype(BF16), w_out.astype(BF16)
    for l in range(depth):
        wuq = _pad_heads(mla_w_uq[l], MLA_QK, 0, MLA_QK, MLA_QK_PAD).astype(BF16)
        wuk = _pad_heads(mla_w_ukv[l], MLA_NOPE + MLA_V, 0, MLA_NOPE, MLA_NOPE).astype(BF16)
        wuv = _pad_heads(mla_w_ukv[l], MLA_NOPE + MLA_V, MLA_NOPE, MLA_NOPE + MLA_V, MLA_V).T.astype(BF16)
        qg = jnp.pad(mla_q_norm[l], (0, MLA_QK_PAD - MLA_QK)).reshape(1, MLA_QK_PAD)
        kg = jnp.pad(mla_k_norm[l], (0, MLA_QK_PAD - MLA_QK)).reshape(1, MLA_QK_PAD)
        wa_pad = jnp.zeros((SMALL_W, GLA_HEADS * GLA_DK), F32).at[
            MLA_ROPE:MLA_ROPE + GLA_GATE_RANK].set(gla_w_a2[l])

        h, small = _norm_call(x2, norm_g[l].reshape(1, D), w_small, l, tm_row)
        p_lin = _inproj_call(h, w_lin, l, _identity, tm_proj)
        p_silu = _inproj_call(h, w_silu, l, _identity, tm_proj)
        p_gelu = _inproj_call(h, w_gelu, l, _gelu_tanh, tm_proj)
        p_gate = _inproj_call(h, w_gate, l, _identity, tm_proj)
        q, k, vt = _mla_prep_call(p_lin, small, cosf, sins,
                                  mla_cq_norm[l].reshape(1, -1), mla_ckv_norm[l].reshape(1, -1),
                                  wuq, wuk, wuv, qg, kg, tq)
        ya = _attn_call(q, k, vt, p_silu, B, S, tq)
        yc = _gla_call(p_lin, p_silu, small, wa_pad, gla_b_a[l].reshape(1, -1),
                       gla_o_norm[l].reshape(1, -1), B, S, cpb)
        merged = _merge_call(ya, p_gelu, p_silu, yc, p_gate, sgu_v_norm[l].reshape(1, -1), sgu_w_s[l],
                             sgu_b_s[l].T, wb, l, tm_row, S, tq)
        x2 = _out_call(x2, merged, wo, l, tm_row)
    return x2.reshape(B, S, D)
```

```python
import functools
import math

import jax
import jax.numpy as jnp
from jax import lax
from jax.experimental import pallas as pl
from jax.experimental.pallas import tpu as pltpu

F32 = jnp.float32
BF16 = jnp.bfloat16

MLA_HEADS = 8
MLA_NOPE = 128
MLA_ROPE = 64
MLA_QK = MLA_NOPE + MLA_ROPE
MLA_V = 128
MLA_Q_RANK = 512
MLA_KV_RANK = 512
MLA_WIDTH = MLA_HEADS * MLA_V
MLA_QK_PAD = 256
MLA_VT_ROWS = MLA_V + 16
SUBLANE = 8
ORDER_LAG = 3
ROPE_THETA = 10000.0
SGU_GROUPS = 8
SGU_CHUNK = 128
SGU_WIDTH = 1024
GLA_HEADS = 4
GLA_DK = 128
GLA_DV = 256
GLA_GATE_RANK = 16
GLA_TAU = 16.0
GLA_CHUNK = 128
GLA_WIDTH = GLA_HEADS * GLA_DV
D_MODEL = 2048
N_BRANCH = 3
BRANCH_WIDTH = 1024
EPS = 1e-6
LANE = 128
LOG2_E = math.log2(math.e)
MASK_VALUE = -1e30
SMALL_W = LANE

PROJ_TN = 1024
LIN_CQKV, LIN_QK_C, LIN_V_C = range(3)
SILU_Z_A, SILU_Z_B, SILU_Z_C = range(3)
GELU_U_B, GELU_V_B = range(2)

VMEM_LIMIT = 56 * 1024 * 1024


def _params(*sem):
    return pltpu.CompilerParams(dimension_semantics=sem, vmem_limit_bytes=VMEM_LIMIT)


def _dot(a, b):
    return jnp.dot(a, b, preferred_element_type=F32)


def _dot_nt(a, b):
    return lax.dot_general(a, b, (((1,), (1,)), ((), ())), preferred_element_type=F32)


def _dot_tn(a, b):
    return lax.dot_general(a, b, (((0,), (0,)), ((), ())), preferred_element_type=F32)


def _sigmoid(x):
    return 0.5 * jnp.tanh(0.5 * x) + 0.5


def _gelu_tanh(x):
    c = math.sqrt(2.0 / math.pi)
    hx = 0.5 * x
    return hx * jnp.tanh(x * (c + (c * 0.044715) * (x * x))) + hx


def _silu(x):
    hx = 0.5 * x
    return hx * jnp.tanh(hx) + hx


def _identity(x):
    return x


def _resident(shape, layer=None):
    nd = len(shape)
    if layer is None:
        return pl.BlockSpec(shape, lambda *_: (0,) * nd, pipeline_mode=pl.Buffered(1))
    return pl.BlockSpec((None,) + tuple(shape), lambda *_: (layer,) + (0,) * nd,
                        pipeline_mode=pl.Buffered(1))


def _norm_kernel(x_ref, g_ref, ws_ref, h_ref, small_ref):
    x = x_ref[...]
    rstd = lax.rsqrt(jnp.mean(x * x, axis=-1, keepdims=True) + EPS)
    h = (x * rstd * g_ref[...]).astype(BF16)
    h_ref[...] = h
    small_ref[...] = _dot_nt(h, ws_ref[...])


def _norm_call(x2, g, w_small, layer, tm):
    T, D = x2.shape
    return pl.pallas_call(
        _norm_kernel,
        grid=(T // tm,),
        in_specs=[pl.BlockSpec((tm, D), lambda i: (i, 0)),
                  _resident((1, D)),
                  _resident((SMALL_W, D), layer)],
        out_specs=[pl.BlockSpec((tm, D), lambda i: (i, 0)),
                   pl.BlockSpec((tm, SMALL_W), lambda i: (i, 0))],
        out_shape=[jax.ShapeDtypeStruct((T, D), BF16),
                   jax.ShapeDtypeStruct((T, SMALL_W), F32)],
        compiler_params=_params("parallel"),
    )(x2, g, w_small)


def _inproj_kernel(h_ref, w_ref, o_ref, *, act):
    o_ref[...] = act(_dot_nt(h_ref[...], w_ref[...])).astype(o_ref.dtype)


def _inproj_call(h, w, layer, act, tm):
    T, D = h.shape
    N = w.shape[1]
    return pl.pallas_call(
        functools.partial(_inproj_kernel, act=act),
        grid=(T // tm, N // PROJ_TN),
        in_specs=[pl.BlockSpec((tm, D), lambda i, j: (i, 0)),
                  pl.BlockSpec((None, PROJ_TN, D), lambda i, j: (layer, j, 0))],
        out_specs=pl.BlockSpec((tm, PROJ_TN), lambda i, j: (i, j)),
        out_shape=jax.ShapeDtypeStruct((T, N), BF16),
        compiler_params=_params("parallel", "arbitrary"),
    )(h, w)


def _rope128(r, cosf, sins):
    lane = lax.broadcasted_iota(jnp.int32, r.shape, 1)
    partner = jnp.where(lane < MLA_ROPE // 2,
                        pltpu.roll(r, LANE - MLA_ROPE // 2, axis=1),
                        pltpu.roll(r, MLA_ROPE // 2, axis=1))
    return r * cosf + partner * sins


def _mla_prep_kernel(cq_ref, ckv_ref, small_ref, cos_ref, sin_ref,
                     cqg_ref, ckvg_ref, wuq_ref, wuk_ref, wuv_ref, qg_ref, kg_ref,
                     q_ref, k_ref, vt_ref):
    cosf = cos_ref[...]
    sins = sin_ref[...]
    scale = MLA_QK ** -0.5 * LOG2_E

    cq = cq_ref[...].astype(F32)
    nq = cq * lax.rsqrt(jnp.mean(cq * cq, axis=-1, keepdims=True) + EPS) * cqg_ref[...]
    q_all = _dot(nq.astype(BF16), wuq_ref[...])

    ckv = ckv_ref[...].astype(F32)
    nkv = ckv * lax.rsqrt(jnp.mean(ckv * ckv, axis=-1, keepdims=True) + EPS) * ckvg_ref[...]
    nkv = nkv.astype(BF16)
    k_all = _dot(nkv, wuk_ref[...])
    vt = _dot_nt(wuv_ref[...], nkv)
    ones_row = (lax.broadcasted_iota(jnp.int32, (MLA_VT_ROWS - MLA_V, vt.shape[1]), 0) == 0)
    for hd in range(MLA_HEADS):
        vt_ref[hd, :MLA_V, :] = vt[hd * MLA_V:(hd + 1) * MLA_V, :].astype(vt_ref.dtype)
        vt_ref[hd, MLA_V:, :] = ones_row.astype(vt_ref.dtype)

    qg = qg_ref[...]
    kg = kg_ref[...]
    lane = lax.broadcasted_iota(jnp.int32, (1, LANE), 1)
    kr = jnp.where(lane < MLA_ROPE, small_ref[...], 0.0)
    kr_ss = jnp.sum(kr * kr, axis=-1, keepdims=True)
    kr_rot = _rope128(kr * kg[:, MLA_NOPE:], cosf, sins)

    for hd in range(MLA_HEADS):
        qb = q_all[:, hd * MLA_QK_PAD:(hd + 1) * MLA_QK_PAD]
        rq = lax.rsqrt(jnp.sum(qb * qb, axis=-1, keepdims=True) * (1.0 / MLA_QK) + EPS) * scale
        qn = qb * rq * qg
        q_ref[:, hd * MLA_QK_PAD:hd * MLA_QK_PAD + MLA_NOPE] = qn[:, :MLA_NOPE].astype(q_ref.dtype)
        q_ref[:, hd * MLA_QK_PAD + MLA_NOPE:(hd + 1) * MLA_QK_PAD] = _rope128(
            qn[:, MLA_NOPE:], cosf, sins).astype(q_ref.dtype)

        kb = k_all[:, hd * MLA_NOPE:(hd + 1) * MLA_NOPE]
        rk = lax.rsqrt((jnp.sum(kb * kb, axis=-1, keepdims=True) + kr_ss) * (1.0 / MLA_QK) + EPS)
        k_ref[:, hd * MLA_QK_PAD:hd * MLA_QK_PAD + MLA_NOPE] = (
            kb * rk * kg[:, :MLA_NOPE]).astype(k_ref.dtype)
        k_ref[:, hd * MLA_QK_PAD + MLA_NOPE:(hd + 1) * MLA_QK_PAD] = (kr_rot * rk).astype(k_ref.dtype)


def _mla_prep_call(proj, small, cosf, sins, cqg, ckvg, wuq, wuk, wuv, qg, kg, tm):
    T = proj.shape[0]
    HQ = MLA_HEADS * MLA_QK_PAD
    return pl.pallas_call(
        _mla_prep_kernel,
        grid=(T // tm,),
        in_specs=[pl.BlockSpec((tm, MLA_Q_RANK), lambda i: (i, 0)),
                  pl.BlockSpec((tm, MLA_KV_RANK), lambda i: (i, 1)),
                  pl.BlockSpec((tm, SMALL_W), lambda i: (i, 0)),
                  pl.BlockSpec((tm, LANE), lambda i: (i, 0)),
                  pl.BlockSpec((tm, LANE), lambda i: (i, 0)),
                  _resident((1, MLA_Q_RANK)),
                  _resident((1, MLA_KV_RANK)),
                  _resident((MLA_Q_RANK, HQ)),
                  _resident((MLA_KV_RANK, MLA_HEADS * MLA_NOPE)),
                  _resident((MLA_WIDTH, MLA_KV_RANK)),
                  _resident((1, MLA_QK_PAD)),
                  _resident((1, MLA_QK_PAD))],
        out_specs=[pl.BlockSpec((tm, HQ), lambda i: (i, 0)),
                   pl.BlockSpec((tm, HQ), lambda i: (i, 0)),
                   pl.BlockSpec((None, MLA_HEADS, MLA_VT_ROWS, tm), lambda i: (i, 0, 0, 0))],
        out_shape=[jax.ShapeDtypeStruct((T, HQ), BF16),
                   jax.ShapeDtypeStruct((T, HQ), BF16),
                   jax.ShapeDtypeStruct((T // tm, MLA_HEADS, MLA_VT_ROWS, tm), BF16)],
        compiler_params=_params("parallel"),
    )(proj, proj, small, cosf, sins, cqg, ckvg, wuq, wuk, wuv, qg, kg)


def _attn_kernel(qlo_ref, qhi_ref, k_ref, vt_ref, szlo_ref, szhi_ref, o_ref,
                 q2_ref, s_ref, mf_ref, pv_ref, *, tq, nq, n_steps):
    g = pl.program_id(0)
    half = nq // 2
    nt = nq + 1
    i1 = jnp.minimum(g, n_steps - 1) % half
    i2 = jnp.maximum(g - 1, 0) % half

    @pl.when(g == 0)
    def _():
        for t in range(nt):
            s_ref[t] = jnp.zeros((tq, tq), F32)
            mf_ref[t] = jnp.zeros((SUBLANE, tq), F32)

    q2_ref[0] = qlo_ref[...]
    q2_ref[1] = qhi_ref[...]

    def is_lo(p, i):
        return True if p == 0 else (False if p >= half else p <= i)

    def pick(p, i, lo, hi):
        c = is_lo(p, i)
        if c is True:
            return lo
        if c is False:
            return hi
        return jnp.where(c, lo, hi)

    def kv_tile(p, i):
        if p == 0:
            return i
        if p == nq:
            return nq - 1 - i
        return pick(p, i, p - 1, p - i - 1)

    def combine(ref, shape, i, op, init):
        lo = jnp.full(shape, init, F32)
        hi = jnp.full(shape, init, F32)
        for p in range(nt):
            c = is_lo(p, i)
            x = ref[p]
            if c is True:
                lo = op(lo, x)
            elif c is False:
                hi = op(hi, x)
            else:
                lo = op(lo, jnp.where(c, x, init))
                hi = op(hi, jnp.where(c, init, x))
        return lo, hi

    m_lo, m_hi = combine(mf_ref, (SUBLANE, tq), i2, jnp.maximum, MASK_VALUE)
    m_lo = jnp.max(m_lo, axis=0, keepdims=True)
    m_hi = jnp.max(m_hi, axis=0, keepdims=True)

    causal = (lax.broadcasted_iota(jnp.int32, (tq, tq), 0)
              <= lax.broadcasted_iota(jnp.int32, (tq, tq), 1))
    order_zeros = [jnp.zeros((1, tq), F32)] * ORDER_LAG
    for p in range(nt):
        e = jnp.exp2(s_ref[p] - (pick(p, i2, m_lo, m_hi) + order_zeros[0]))
        pv_ref[p] = _dot(vt_ref[kv_tile(p, i2)], e.astype(BF16))
        j = kv_tile(p, i1)
        s = _dot_nt(k_ref[pl.ds(pl.multiple_of(j * tq, tq), tq), :], q2_ref[pick(p, i1, 0, 1)])
        if p == 0 or p == nq:
            s = jnp.where(causal, s, MASK_VALUE)
        s_ref[p] = s
        mf = jnp.max(s.reshape(tq // SUBLANE, SUBLANE, tq), axis=0)
        mf_ref[p] = mf
        bits = lax.bitcast_convert_type(mf[:1, :], jnp.uint32)
        order_zeros = order_zeros[1:] + [lax.bitcast_convert_type((bits >> 16) >> 16, F32)]

    a_lo, a_hi = combine(pv_ref, (MLA_VT_ROWS, tq), i2, jnp.add, 0.0)
    for a, sz_ref, rows in ((a_lo, szlo_ref, slice(0, tq)), (a_hi, szhi_ref, slice(tq, 2 * tq))):
        ot = a[:MLA_V, :] / a[MLA_V:MLA_V + 1, :]
        o_ref[rows, :] = (ot.T * _silu(sz_ref[...].astype(F32))).astype(o_ref.dtype)


def _attn_call(q, k, vt, p_silu, B, S, tq):
    T = q.shape[0]
    nq = S // tq
    assert nq % 2 == 0 and MLA_V == LANE
    half = nq // 2
    n_steps = B * MLA_HEADS * half
    sz_col0 = SILU_Z_A * PROJ_TN // MLA_V

    def bhi(g):
        bh, i = g // half, g % half
        return bh // MLA_HEADS, bh % MLA_HEADS, i

    def first(g):
        return bhi(jnp.minimum(g, n_steps - 1))

    def second(g):
        return bhi(jnp.maximum(g - 1, 0))

    def lo_blk(f, col0=0):
        def index(g):
            b, h, i = f(g)
            return b * nq + i, col0 + h
        return index

    def hi_blk(f, col0=0):
        def index(g):
            b, h, i = f(g)
            return b * nq + nq - 1 - i, col0 + h
        return index

    def seq_blk(f):
        def index(g):
            b, h, _ = f(g)
            return b, h
        return index

    def seq_blk4(f):
        def index(g):
            b, h, _ = f(g)
            return b, h, 0, 0
        return index

    def out_blk(g):
        b, h, i = second(g)
        return b * half + i, h

    nt = nq + 1
    return pl.pallas_call(
        functools.partial(_attn_kernel, tq=tq, nq=nq, n_steps=n_steps),
        grid=(n_steps + 1,),
        in_specs=[pl.BlockSpec((tq, MLA_QK_PAD), lo_blk(first)),
                  pl.BlockSpec((tq, MLA_QK_PAD), hi_blk(first)),
                  pl.BlockSpec((S, MLA_QK_PAD), seq_blk(first)),
                  pl.BlockSpec((nq, None, MLA_VT_ROWS, tq), seq_blk4(second)),
                  pl.BlockSpec((tq, MLA_V), lo_blk(second, sz_col0)),
                  pl.BlockSpec((tq, MLA_V), hi_blk(second, sz_col0))],
        out_specs=pl.BlockSpec((2 * tq, MLA_V), out_blk),
        out_shape=jax.ShapeDtypeStruct((T, MLA_WIDTH), BF16),
        scratch_shapes=[pltpu.VMEM((2, tq, MLA_QK_PAD), BF16),
                        pltpu.VMEM((nt, tq, tq), F32),
                        pltpu.VMEM((nt, SUBLANE, tq), F32),
                        pltpu.VMEM((nt, MLA_VT_ROWS, tq), F32)],
        compiler_params=_params("arbitrary"),
    )(q, q, k, vt, p_silu, p_silu)


def _paired_block(r, S, tq, tm):
    per_seq, per_tile, nq = S // tm, tq // tm, S // tq
    b, w = r // per_seq, r % per_seq
    u, sub = w // per_tile, w % per_tile
    pos = jnp.where(u < nq // 2, 2 * u, 2 * (nq - 1 - u) + 1)
    return b * per_seq + pos * per_tile + sub


def _sgu_tile(u_ref, v_ref, sz_ref, vg_ref, ws_ref, bs_ref, o_ref, tm):
    v = v_ref[...].astype(F32)
    vn = (v * lax.rsqrt(jnp.mean(v * v, axis=-1, keepdims=True) + EPS) * vg_ref[...]).astype(BF16)
    row = lax.broadcasted_iota(jnp.int32, (SGU_CHUNK, SGU_CHUNK), 0)
    col = lax.broadcasted_iota(jnp.int32, (SGU_CHUNK, SGU_CHUNK), 1)
    gd = SGU_WIDTH // SGU_GROUPS
    for g in range(SGU_GROUPS):
        w = jnp.where(row >= col, ws_ref[g], 0.0).astype(BF16)
        bias = bs_ref[:, g:g + 1]
        for c in range(tm // SGU_CHUNK):
            rs = slice(c * SGU_CHUNK, (c + 1) * SGU_CHUNK)
            cs = slice(g * gd, (g + 1) * gd)
            mix = _dot(w, vn[rs, cs]) + bias
            o_ref[rs, cs] = (u_ref[rs, cs].astype(F32) * mix
                             * _silu(sz_ref[rs, cs].astype(F32))).astype(o_ref.dtype)


def _split_bf16(x):
    hi = x.astype(BF16)
    lo = (x - hi.astype(F32)).astype(BF16)
    return hi, lo


def _gla_kernel(qk_ref, v_ref, sz_ref, small_ref, wa_ref, ba_ref, og_ref, o_ref, st_ref, *, cpb):
    L = GLA_CHUNK

    @pl.when(pl.program_id(1) == 0)
    def _():
        st_ref[...] = jnp.zeros(st_ref.shape, F32)

    a_hi, a_lo = _split_bf16(small_ref[...])
    w_hi, w_lo = _split_bf16(wa_ref[...])
    xg = _dot(a_hi, w_hi) + _dot(a_lo, w_hi) + _dot(a_hi, w_lo) + ba_ref[...]
    log_a = (jnp.minimum(xg, 0.0) - jnp.log(1.0 + jnp.exp(-jnp.abs(xg)))) * (1.0 / GLA_TAU)
    row = lax.broadcasted_iota(jnp.int32, (L, L), 0)
    col = lax.broadcasted_iota(jnp.int32, (L, L), 1)
    causal = row >= col
    ones_tril = causal.astype(BF16)
    og = og_ref[...]
    states = [st_ref[hd] for hd in range(GLA_HEADS)]

    for c in range(cpb):
        rs = slice(c * L, (c + 1) * L)
        la_hi, la_lo = _split_bf16(log_a[rs, :])
        bcum = _dot(ones_tril, la_hi) + _dot(ones_tril, la_lo)
        for hd in range(GLA_HEADS):
            ks = slice(hd * GLA_DK, (hd + 1) * GLA_DK)
            k2 = slice(GLA_HEADS * GLA_DK + hd * GLA_DK, GLA_HEADS * GLA_DK + (hd + 1) * GLA_DK)
            vs = slice(hd * GLA_DV, (hd + 1) * GLA_DV)
            b = bcum[:, ks]
            b_mid = b[L // 2:L // 2 + 1, :]
            b_last = b[L - 1:L, :]
            q = qk_ref[rs, ks].astype(F32) * (GLA_DK ** -0.5)
            k = qk_ref[rs, k2].astype(F32)
            v = v_ref[rs, vs]
            q_t = (q * jnp.exp(b - b_mid)).astype(BF16)
            k_t = (k * jnp.exp(b_mid - b)).astype(BF16)
            attn = jnp.where(causal, _dot_nt(q_t, k_t), 0.0).astype(BF16)
            st = states[hd]
            o = _dot_nt((q * jnp.exp(b)).astype(BF16), st.astype(BF16)) + _dot(attn, v)
            k_s = (k * jnp.exp(b_last - b)).astype(BF16)
            states[hd] = jnp.exp(b_last) * st + _dot_tn(v, k_s)
            on = o * lax.rsqrt(jnp.mean(o * o, axis=-1, keepdims=True) + EPS) * og
            o_ref[rs, vs] = (on * _silu(sz_ref[rs, vs].astype(F32))).astype(o_ref.dtype)

    for hd in range(GLA_HEADS):
        st_ref[hd] = states[hd]


def _gla_call(p_lin, p_silu, small, wa_pad, ba, og, B, S, cpb):
    T = p_lin.shape[0]
    R = cpb * GLA_CHUNK
    ns = S // R
    return pl.pallas_call(
        functools.partial(_gla_kernel, cpb=cpb),
        grid=(B, ns),
        in_specs=[pl.BlockSpec((R, PROJ_TN), lambda b, c: (b * ns + c, LIN_QK_C)),
                  pl.BlockSpec((R, GLA_WIDTH), lambda b, c: (b * ns + c, LIN_V_C)),
                  pl.BlockSpec((R, GLA_WIDTH), lambda b, c: (b * ns + c, SILU_Z_C)),
                  pl.BlockSpec((R, SMALL_W), lambda b, c: (b * ns + c, 0)),
                  _resident((SMALL_W, GLA_HEADS * GLA_DK)),
                  _resident((1, GLA_HEADS * GLA_DK)),
                  _resident((1, GLA_DV))],
        out_specs=pl.BlockSpec((R, GLA_WIDTH), lambda b, c: (b * ns + c, 0)),
        out_shape=jax.ShapeDtypeStruct((T, GLA_WIDTH), BF16),
        scratch_shapes=[pltpu.VMEM((GLA_HEADS, GLA_DV, GLA_DK), F32)],
        compiler_params=_params("parallel", "arbitrary"),
    )(p_lin, p_lin, p_silu, small, wa_pad, ba, og)


def _merge_kernel(ya_ref, u_ref, v_ref, szb_ref, yc_ref, ga_ref, gb_ref, gc_ref,
                  vg_ref, ws_ref, bs_ref, wb_ref, o_ref, yb_ref, *, tm):
    _sgu_tile(u_ref, v_ref, szb_ref, vg_ref, ws_ref, bs_ref, yb_ref, tm)
    merged = _sigmoid(ga_ref[...].astype(F32)) * _dot(ya_ref[...], wb_ref[0])
    merged += _sigmoid(gb_ref[...].astype(F32)) * _dot(yb_ref[...], wb_ref[1])
    merged += _sigmoid(gc_ref[...].astype(F32)) * _dot(yc_ref[...], wb_ref[2])
    o_ref[...] = merged.astype(o_ref.dtype)


def _merge_call(ya, p_gelu, p_silu, yc, p_gate, vg, ws, bs_t, wb, layer, tm, S, tq):
    T = ya.shape[0]
    D = wb.shape[-1]
    assert SGU_WIDTH == BRANCH_WIDTH
    yspec = pl.BlockSpec((tm, BRANCH_WIDTH), lambda i: (i, 0))
    ya_spec = pl.BlockSpec((tm, BRANCH_WIDTH), lambda i: (_paired_block(i, S, tq, tm), 0))
    return pl.pallas_call(
        functools.partial(_merge_kernel, tm=tm),
        grid=(T // tm,),
        in_specs=[ya_spec,
                  pl.BlockSpec((tm, SGU_WIDTH), lambda i: (i, GELU_U_B)),
                  pl.BlockSpec((tm, SGU_WIDTH), lambda i: (i, GELU_V_B)),
                  pl.BlockSpec((tm, SGU_WIDTH), lambda i: (i, SILU_Z_B)),
                  yspec,
                  pl.BlockSpec((tm, D), lambda i: (i, 0)),
                  pl.BlockSpec((tm, D), lambda i: (i, 1)),
                  pl.BlockSpec((tm, D), lambda i: (i, 2)),
                  _resident((1, SGU_WIDTH)),
                  _resident((SGU_GROUPS, SGU_CHUNK, SGU_CHUNK)),
                  _resident((SGU_CHUNK, SGU_GROUPS)),
                  _resident((N_BRANCH, BRANCH_WIDTH, D), layer)],
        out_specs=pl.BlockSpec((tm, D), lambda i: (i, 0)),
        out_shape=jax.ShapeDtypeStruct((T, D), BF16),
        scratch_shapes=[pltpu.VMEM((tm, SGU_WIDTH), BF16)],
        compiler_params=_params("parallel"),
    )(ya, p_gelu, p_gelu, p_silu, yc, p_gate, p_gate, p_gate, vg, ws, bs_t, wb)


def _out_kernel(x_ref, m_ref, wo_ref, o_ref):
    o_ref[...] = x_ref[...] + _dot(m_ref[...], wo_ref[...])


def _out_call(x2, merged, wo, layer, tm):
    T, D = x2.shape
    return pl.pallas_call(
        _out_kernel,
        grid=(T // tm,),
        in_specs=[pl.BlockSpec((tm, D), lambda i: (i, 0)),
                  pl.BlockSpec((tm, D), lambda i: (i, 0)),
                  _resident((D, D), layer)],
        out_specs=pl.BlockSpec((tm, D), lambda i: (i, 0)),
        out_shape=jax.ShapeDtypeStruct((T, D), F32),
        compiler_params=_params("parallel"),
    )(x2, merged, wo)


_IN_SIZES = (("c_q", MLA_Q_RANK), ("c_kv", MLA_KV_RANK), ("k_rope", MLA_ROPE), ("z_a", MLA_WIDTH),
             ("u_b", SGU_WIDTH), ("v_b", SGU_WIDTH), ("z_b", SGU_WIDTH),
             ("q_c", GLA_HEADS * GLA_DK), ("k_c", GLA_HEADS * GLA_DK), ("v_c", GLA_WIDTH),
             ("a_r", GLA_GATE_RANK), ("z_c", GLA_WIDTH), ("gates", N_BRANCH * D_MODEL))
_IN_SEG = {}
_off = 0
for _name, _size in _IN_SIZES:
    _IN_SEG[_name] = (_off, _size)
    _off += _size
IN_COLS = _off
_W_GROUPS = (("c_q", "c_kv", "q_c", "k_c", "v_c"), ("z_a", "z_b", "z_c"), ("u_b", "v_b"),
             ("gates",), ("k_rope", "a_r"))


def _wprep_kernel(w_ref, lin_ref, silu_ref, gelu_ref, gate_ref, small_ref):
    small_ref[...] = jnp.zeros(small_ref.shape, small_ref.dtype)
    for o_ref, names in zip((lin_ref, silu_ref, gelu_ref, gate_ref, small_ref), _W_GROUPS):
        dst = 0
        for name in names:
            src, size = _IN_SEG[name]
            o_ref[dst:dst + size, :] = w_ref[src:src + size, :].astype(o_ref.dtype)
            dst += size


def _split_w_in(w_in, tk):
    depth, D, cols = w_in.shape
    assert cols == IN_COLS
    widths = [sum(_IN_SEG[n][1] for n in names) for names in _W_GROUPS[:-1]] + [SMALL_W]
    return pl.pallas_call(
        _wprep_kernel,
        grid=(depth, D // tk),
        in_specs=[pl.BlockSpec((None, cols, tk), lambda l, r: (l, 0, r))],
        out_specs=[pl.BlockSpec((None, n, tk), lambda l, r: (l, 0, r)) for n in widths],
        out_shape=[jax.ShapeDtypeStruct((depth, n, D), BF16) for n in widths],
        compiler_params=_params("parallel", "parallel"),
    )(jnp.swapaxes(w_in, 1, 2))


def _pad_heads(w, per_head, lo, hi, width):
    r = w.shape[0]
    w = w.reshape(r, MLA_HEADS, per_head)[:, :, lo:hi]
    w = jnp.pad(w, ((0, 0), (0, 0), (0, width - (hi - lo))))
    return w.reshape(r, MLA_HEADS * width)


def _rope_tables(positions):
    half = MLA_ROPE // 2
    inv_freq = 1.0 / (ROPE_THETA ** (jnp.arange(0, MLA_ROPE, 2, dtype=F32) / MLA_ROPE))
    inv_freq = jnp.concatenate([inv_freq, inv_freq, jnp.zeros((LANE - 2 * half,), F32)])
    ang = positions.astype(F32).reshape(-1, 1) * inv_freq
    lane = jnp.arange(LANE)
    cos, sin = jnp.cos(ang), jnp.sin(ang)
    return (jnp.where(lane < 2 * half, cos, 0.0),
            jnp.where(lane < half, -sin, jnp.where(lane < 2 * half, sin, 0.0)))


def _pick_tile(n, want):
    t = min(n, want)
    while n % t:
        t //= 2
    return t


def kernel(x, positions, norm_g, w_in, mla_cq_norm, mla_ckv_norm, mla_w_uq, mla_w_ukv,
           mla_q_norm, mla_k_norm, sgu_v_norm, sgu_w_s, sgu_b_s, gla_w_a2, gla_b_a,
           gla_o_norm, w_branch, w_out):
    B, S, D = x.shape
    T = B * S
    depth = w_in.shape[0]
    tm_proj = _pick_tile(T, 2048)
    tm_row = _pick_tile(T, 512)
    tm_norm = _pick_tile(T, 1024)
    tq = _pick_tile(S, 512)
    cpb = _pick_tile(S // GLA_CHUNK, 8)

    cosf, sins = _rope_tables(positions)
    x2 = x.reshape(T, D)
    w_lin, w_silu, w_gelu, w_gate, w_small = _split_w_in(w_in, _pick_tile(D, 256))
    wb, wo = w_branch.astype(BF16), w_out.astype(BF16)
    for l in range(depth):
        wuq = _pad_heads(mla_w_uq[l], MLA_QK, 0, MLA_QK, MLA_QK_PAD).astype(BF16)
        wuk = _pad_heads(mla_w_ukv[l], MLA_NOPE + MLA_V, 0, MLA_NOPE, MLA_NOPE).astype(BF16)
        wuv = _pad_heads(mla_w_ukv[l], MLA_NOPE + MLA_V, MLA_NOPE, MLA_NOPE + MLA_V, MLA_V).T.astype(BF16)
        qg = jnp.pad(mla_q_norm[l], (0, MLA_QK_PAD - MLA_QK)).reshape(1, MLA_QK_PAD)
        kg = jnp.pad(mla_k_norm[l], (0, MLA_QK_PAD - MLA_QK)).reshape(1, MLA_QK_PAD)
        wa_pad = jnp.zeros((SMALL_W, GLA_HEADS * GLA_DK), F32).at[
            MLA_ROPE:MLA_ROPE + GLA_GATE_RANK].set(gla_w_a2[l])

        h, small = _norm_call(x2, norm_g[l].reshape(1, D), w_small, l, tm_norm)
        p_lin = _inproj_call(h, w_lin, l, _identity, tm_proj)
        p_silu = _inproj_call(h, w_silu, l, _identity, tm_proj)
        p_gelu = _inproj_call(h, w_gelu, l, _gelu_tanh, tm_proj)
        p_gate = _inproj_call(h, w_gate, l, _identity, tm_proj)
        q, k, vt = _mla_prep_call(p_lin, small, cosf, sins,
                                  mla_cq_norm[l].reshape(1, -1), mla_ckv_norm[l].reshape(1, -1),
                                  wuq, wuk, wuv, qg, kg, tq)
        ya = _attn_call(q, k, vt, p_silu, B, S, tq)
        yc = _gla_call(p_lin, p_silu, small, wa_pad, gla_b_a[l].reshape(1, -1),
                       gla_o_norm[l].reshape(1, -1), B, S, cpb)
        merged = _merge_call(ya, p_gelu, p_silu, yc, p_gate, sgu_v_norm[l].reshape(1, -1), sgu_w_s[l],
                             sgu_b_s[l].T, wb, l, tm_row, S, tq)
        x2 = _out_call(x2, merged, wo, l, tm_row)
    return x2.reshape(B, S, D)
```

```python
import functools
import math

import jax
import jax.numpy as jnp
from jax import lax
from jax.experimental import pallas as pl
from jax.experimental.pallas import tpu as pltpu

F32 = jnp.float32
BF16 = jnp.bfloat16

MLA_HEADS = 8
MLA_NOPE = 128
MLA_ROPE = 64
MLA_QK = MLA_NOPE + MLA_ROPE
MLA_V = 128
MLA_Q_RANK = 512
MLA_KV_RANK = 512
MLA_WIDTH = MLA_HEADS * MLA_V
MLA_QK_PAD = 256
MLA_VT_ROWS = MLA_V + 16
SUBLANE = 8
ORDER_LAG = 3
ROPE_THETA = 10000.0
SGU_GROUPS = 8
SGU_CHUNK = 128
SGU_WIDTH = 1024
GLA_HEADS = 4
GLA_DK = 128
GLA_DV = 256
GLA_GATE_RANK = 16
GLA_TAU = 16.0
GLA_CHUNK = 128
GLA_WIDTH = GLA_HEADS * GLA_DV
D_MODEL = 2048
N_BRANCH = 3
BRANCH_WIDTH = 1024
EPS = 1e-6
LANE = 128
LOG2_E = math.log2(math.e)
MASK_VALUE = -1e30
SMALL_W = LANE

PROJ_TN = 1024
LIN_CQKV, LIN_QK_C, LIN_V_C = range(3)
SILU_Z_A, SILU_Z_B, SILU_Z_C = range(3)
GELU_U_B, GELU_V_B = range(2)

VMEM_LIMIT = 56 * 1024 * 1024


def _params(*sem):
    return pltpu.CompilerParams(dimension_semantics=sem, vmem_limit_bytes=VMEM_LIMIT)


def _dot(a, b):
    return jnp.dot(a, b, preferred_element_type=F32)


def _dot_nt(a, b):
    return lax.dot_general(a, b, (((1,), (1,)), ((), ())), preferred_element_type=F32)


def _dot_tn(a, b):
    return lax.dot_general(a, b, (((0,), (0,)), ((), ())), preferred_element_type=F32)


def _sigmoid(x):
    return 0.5 * jnp.tanh(0.5 * x) + 0.5


def _gelu_tanh(x):
    c = math.sqrt(2.0 / math.pi)
    hx = 0.5 * x
    return hx * jnp.tanh(x * (c + (c * 0.044715) * (x * x))) + hx


def _silu(x):
    hx = 0.5 * x
    return hx * jnp.tanh(hx) + hx


def _identity(x):
    return x


def _resident(shape, layer=None):
    nd = len(shape)
    if layer is None:
        return pl.BlockSpec(shape, lambda *_: (0,) * nd, pipeline_mode=pl.Buffered(1))
    return pl.BlockSpec((None,) + tuple(shape), lambda *_: (layer,) + (0,) * nd,
                        pipeline_mode=pl.Buffered(1))


def _norm_kernel(x_ref, g_ref, ws_ref, h_ref, small_ref):
    x = x_ref[...]
    rstd = lax.rsqrt(jnp.mean(x * x, axis=-1, keepdims=True) + EPS)
    h = (x * rstd * g_ref[...]).astype(BF16)
    h_ref[...] = h
    small_ref[...] = _dot_nt(h, ws_ref[...])


def _norm_call(x2, g, w_small, layer, tm):
    T, D = x2.shape
    return pl.pallas_call(
        _norm_kernel,
        grid=(T // tm,),
        in_specs=[pl.BlockSpec((tm, D), lambda i: (i, 0)),
                  _resident((1, D)),
                  _resident((SMALL_W, D), layer)],
        out_specs=[pl.BlockSpec((tm, D), lambda i: (i, 0)),
                   pl.BlockSpec((tm, SMALL_W), lambda i: (i, 0))],
        out_shape=[jax.ShapeDtypeStruct((T, D), BF16),
                   jax.ShapeDtypeStruct((T, SMALL_W), F32)],
        compiler_params=_params("parallel"),
    )(x2, g, w_small)


def _inproj_kernel(h_ref, w_ref, o_ref, *, act):
    o_ref[...] = act(_dot_nt(h_ref[...], w_ref[...])).astype(o_ref.dtype)


def _inproj_call(h, w, layer, act, tm):
    T, D = h.shape
    N = w.shape[1]
    return pl.pallas_call(
        functools.partial(_inproj_kernel, act=act),
        grid=(T // tm, N // PROJ_TN),
        in_specs=[pl.BlockSpec((tm, D), lambda i, j: (i, 0)),
                  pl.BlockSpec((None, PROJ_TN, D), lambda i, j: (layer, j, 0))],
        out_specs=pl.BlockSpec((tm, PROJ_TN), lambda i, j: (i, j)),
        out_shape=jax.ShapeDtypeStruct((T, N), BF16),
        compiler_params=_params("parallel", "arbitrary"),
    )(h, w)


def _rope128(r, cosf, sins):
    lane = lax.broadcasted_iota(jnp.int32, r.shape, 1)
    partner = jnp.where(lane < MLA_ROPE // 2,
                        pltpu.roll(r, LANE - MLA_ROPE // 2, axis=1),
                        pltpu.roll(r, MLA_ROPE // 2, axis=1))
    return r * cosf + partner * sins


def _mla_prep_kernel(cq_ref, ckv_ref, small_ref, cos_ref, sin_ref,
                     cqg_ref, ckvg_ref, wuq_ref, wuk_ref, wuv_ref, qg_ref, kg_ref,
                     q_ref, k_ref, vt_ref, *, tq):
    cosf = cos_ref[...]
    sins = sin_ref[...]
    scale = MLA_QK ** -0.5 * LOG2_E

    cq = cq_ref[...].astype(F32)
    nq = cq * lax.rsqrt(jnp.mean(cq * cq, axis=-1, keepdims=True) + EPS) * cqg_ref[...]
    q_all = _dot(nq.astype(BF16), wuq_ref[...])

    ckv = ckv_ref[...].astype(F32)
    nkv = ckv * lax.rsqrt(jnp.mean(ckv * ckv, axis=-1, keepdims=True) + EPS) * ckvg_ref[...]
    nkv = nkv.astype(BF16)
    k_all = _dot(nkv, wuk_ref[...])
    vt = _dot_nt(wuv_ref[...], nkv)
    ones_row = (lax.broadcasted_iota(jnp.int32, (MLA_VT_ROWS - MLA_V, tq), 0) == 0)
    for sub in range(vt.shape[1] // tq):
        for hd in range(MLA_HEADS):
            vt_ref[sub, hd, :MLA_V, :] = vt[hd * MLA_V:(hd + 1) * MLA_V,
                                            sub * tq:(sub + 1) * tq].astype(vt_ref.dtype)
            vt_ref[sub, hd, MLA_V:, :] = ones_row.astype(vt_ref.dtype)

    qg = qg_ref[...]
    kg = kg_ref[...]
    lane = lax.broadcasted_iota(jnp.int32, (1, LANE), 1)
    kr = jnp.where(lane < MLA_ROPE, small_ref[...], 0.0)
    kr_ss = jnp.sum(kr * kr, axis=-1, keepdims=True)
    kr_rot = _rope128(kr * kg[:, MLA_NOPE:], cosf, sins)

    for hd in range(MLA_HEADS):
        qb = q_all[:, hd * MLA_QK_PAD:(hd + 1) * MLA_QK_PAD]
        rq = lax.rsqrt(jnp.sum(qb * qb, axis=-1, keepdims=True) * (1.0 / MLA_QK) + EPS) * scale
        qn = qb * rq * qg
        q_ref[:, hd * MLA_QK_PAD:hd * MLA_QK_PAD + MLA_NOPE] = qn[:, :MLA_NOPE].astype(q_ref.dtype)
        q_ref[:, hd * MLA_QK_PAD + MLA_NOPE:(hd + 1) * MLA_QK_PAD] = _rope128(
            qn[:, MLA_NOPE:], cosf, sins).astype(q_ref.dtype)

        kb = k_all[:, hd * MLA_NOPE:(hd + 1) * MLA_NOPE]
        rk = lax.rsqrt((jnp.sum(kb * kb, axis=-1, keepdims=True) + kr_ss) * (1.0 / MLA_QK) + EPS)
        k_ref[:, hd * MLA_QK_PAD:hd * MLA_QK_PAD + MLA_NOPE] = (
            kb * rk * kg[:, :MLA_NOPE]).astype(k_ref.dtype)
        k_ref[:, hd * MLA_QK_PAD + MLA_NOPE:(hd + 1) * MLA_QK_PAD] = (kr_rot * rk).astype(k_ref.dtype)


def _mla_prep_call(proj, small, cosf, sins, cqg, ckvg, wuq, wuk, wuv, qg, kg, tm, tq):
    T = proj.shape[0]
    assert tm % tq == 0
    HQ = MLA_HEADS * MLA_QK_PAD
    return pl.pallas_call(
        functools.partial(_mla_prep_kernel, tq=tq),
        grid=(T // tm,),
        in_specs=[pl.BlockSpec((tm, MLA_Q_RANK), lambda i: (i, 0)),
                  pl.BlockSpec((tm, MLA_KV_RANK), lambda i: (i, 1)),
                  pl.BlockSpec((tm, SMALL_W), lambda i: (i, 0)),
                  pl.BlockSpec((tm, LANE), lambda i: (i, 0)),
                  pl.BlockSpec((tm, LANE), lambda i: (i, 0)),
                  _resident((1, MLA_Q_RANK)),
                  _resident((1, MLA_KV_RANK)),
                  _resident((MLA_Q_RANK, HQ)),
                  _resident((MLA_KV_RANK, MLA_HEADS * MLA_NOPE)),
                  _resident((MLA_WIDTH, MLA_KV_RANK)),
                  _resident((1, MLA_QK_PAD)),
                  _resident((1, MLA_QK_PAD))],
        out_specs=[pl.BlockSpec((tm, HQ), lambda i: (i, 0)),
                   pl.BlockSpec((tm, HQ), lambda i: (i, 0)),
                   pl.BlockSpec((tm // tq, MLA_HEADS, MLA_VT_ROWS, tq), lambda i: (i, 0, 0, 0))],
        out_shape=[jax.ShapeDtypeStruct((T, HQ), BF16),
                   jax.ShapeDtypeStruct((T, HQ), BF16),
                   jax.ShapeDtypeStruct((T // tq, MLA_HEADS, MLA_VT_ROWS, tq), BF16)],
        compiler_params=_params("parallel"),
    )(proj, proj, small, cosf, sins, cqg, ckvg, wuq, wuk, wuv, qg, kg)


def _attn_kernel(qlo_ref, qhi_ref, k_ref, vt_ref, szlo_ref, szhi_ref, o_ref,
                 q2_ref, s_ref, mf_ref, pv_ref, *, tq, nq, n_steps):
    g = pl.program_id(0)
    half = nq // 2
    nt = nq + 1
    i1 = jnp.minimum(g, n_steps - 1) % half
    i2 = jnp.maximum(g - 1, 0) % half

    @pl.when(g == 0)
    def _():
        for t in range(nt):
            s_ref[t] = jnp.zeros((tq, tq), F32)
            mf_ref[t] = jnp.zeros((SUBLANE, tq), F32)

    q2_ref[0] = qlo_ref[...]
    q2_ref[1] = qhi_ref[...]

    def is_lo(p, i):
        return True if p == 0 else (False if p >= half else p <= i)

    def pick(p, i, lo, hi):
        c = is_lo(p, i)
        if c is True:
            return lo
        if c is False:
            return hi
        return jnp.where(c, lo, hi)

    def kv_tile(p, i):
        if p == 0:
            return i
        if p == nq:
            return nq - 1 - i
        return pick(p, i, p - 1, p - i - 1)

    def combine(ref, shape, i, op, init):
        lo = jnp.full(shape, init, F32)
        hi = jnp.full(shape, init, F32)
        for p in range(nt):
            c = is_lo(p, i)
            x = ref[p]
            if c is True:
                lo = op(lo, x)
            elif c is False:
                hi = op(hi, x)
            else:
                lo = op(lo, jnp.where(c, x, init))
                hi = op(hi, jnp.where(c, init, x))
        return lo, hi

    m_lo, m_hi = combine(mf_ref, (SUBLANE, tq), i2, jnp.maximum, MASK_VALUE)
    m_lo = jnp.max(m_lo, axis=0, keepdims=True)
    m_hi = jnp.max(m_hi, axis=0, keepdims=True)

    causal = (lax.broadcasted_iota(jnp.int32, (tq, tq), 0)
              <= lax.broadcasted_iota(jnp.int32, (tq, tq), 1))
    order_zeros = [jnp.zeros((1, tq), F32)] * ORDER_LAG
    for p in range(nt):
        e = jnp.exp2(s_ref[p] - (pick(p, i2, m_lo, m_hi) + order_zeros[0]))
        pv_ref[p] = _dot(vt_ref[kv_tile(p, i2)], e.astype(BF16))
        j = kv_tile(p, i1)
        s = _dot_nt(k_ref[pl.ds(pl.multiple_of(j * tq, tq), tq), :], q2_ref[pick(p, i1, 0, 1)])
        if p == 0 or p == nq:
            s = jnp.where(causal, s, MASK_VALUE)
        s_ref[p] = s
        mf = jnp.max(s.reshape(tq // SUBLANE, SUBLANE, tq), axis=0)
        mf_ref[p] = mf
        bits = lax.bitcast_convert_type(mf[:1, :], jnp.uint32)
        order_zeros = order_zeros[1:] + [lax.bitcast_convert_type((bits >> 16) >> 16, F32)]

    a_lo, a_hi = combine(pv_ref, (MLA_VT_ROWS, tq), i2, jnp.add, 0.0)
    for a, sz_ref, rows in ((a_lo, szlo_ref, slice(0, tq)), (a_hi, szhi_ref, slice(tq, 2 * tq))):
        ot = a[:MLA_V, :] / a[MLA_V:MLA_V + 1, :]
        o_ref[rows, :] = (ot.T * _silu(sz_ref[...].astype(F32))).astype(o_ref.dtype)


def _attn_call(q, k, vt, p_silu, B, S, tq):
    T = q.shape[0]
    nq = S // tq
    assert nq % 2 == 0 and MLA_V == LANE
    half = nq // 2
    n_steps = B * MLA_HEADS * half
    sz_col0 = SILU_Z_A * PROJ_TN // MLA_V

    def bhi(g):
        bh, i = g // half, g % half
        return bh // MLA_HEADS, bh % MLA_HEADS, i

    def first(g):
        return bhi(jnp.minimum(g, n_steps - 1))

    def second(g):
        return bhi(jnp.maximum(g - 1, 0))

    def lo_blk(f, col0=0):
        def index(g):
            b, h, i = f(g)
            return b * nq + i, col0 + h
        return index

    def hi_blk(f, col0=0):
        def index(g):
            b, h, i = f(g)
            return b * nq + nq - 1 - i, col0 + h
        return index

    def seq_blk(f):
        def index(g):
            b, h, _ = f(g)
            return b, h
        return index

    def seq_blk4(f):
        def index(g):
            b, h, _ = f(g)
            return b, h, 0, 0
        return index

    def out_blk(g):
        b, h, i = second(g)
        return b * half + i, h

    nt = nq + 1
    return pl.pallas_call(
        functools.partial(_attn_kernel, tq=tq, nq=nq, n_steps=n_steps),
        grid=(n_steps + 1,),
        in_specs=[pl.BlockSpec((tq, MLA_QK_PAD), lo_blk(first)),
                  pl.BlockSpec((tq, MLA_QK_PAD), hi_blk(first)),
                  pl.BlockSpec((S, MLA_QK_PAD), seq_blk(first)),
                  pl.BlockSpec((nq, None, MLA_VT_ROWS, tq), seq_blk4(second)),
                  pl.BlockSpec((tq, MLA_V), lo_blk(second, sz_col0)),
                  pl.BlockSpec((tq, MLA_V), hi_blk(second, sz_col0))],
        out_specs=pl.BlockSpec((2 * tq, MLA_V), out_blk),
        out_shape=jax.ShapeDtypeStruct((T, MLA_WIDTH), BF16),
        scratch_shapes=[pltpu.VMEM((2, tq, MLA_QK_PAD), BF16),
                        pltpu.VMEM((nt, tq, tq), F32),
                        pltpu.VMEM((nt, SUBLANE, tq), F32),
                        pltpu.VMEM((nt, MLA_VT_ROWS, tq), F32)],
        compiler_params=_params("arbitrary"),
    )(q, q, k, vt, p_silu, p_silu)


def _paired_block(r, S, tq, tm):
    per_seq, per_tile, nq = S // tm, tq // tm, S // tq
    b, w = r // per_seq, r % per_seq
    u, sub = w // per_tile, w % per_tile
    pos = jnp.where(u < nq // 2, 2 * u, 2 * (nq - 1 - u) + 1)
    return b * per_seq + pos * per_tile + sub


def _sgu_tile(u_ref, v_ref, sz_ref, vg_ref, ws_ref, bs_ref, o_ref, tm):
    v = v_ref[...].astype(F32)
    vn = (v * lax.rsqrt(jnp.mean(v * v, axis=-1, keepdims=True) + EPS) * vg_ref[...]).astype(BF16)
    row = lax.broadcasted_iota(jnp.int32, (SGU_CHUNK, SGU_CHUNK), 0)
    col = lax.broadcasted_iota(jnp.int32, (SGU_CHUNK, SGU_CHUNK), 1)
    gd = SGU_WIDTH // SGU_GROUPS
    for g in range(SGU_GROUPS):
        w = jnp.where(row >= col, ws_ref[g], 0.0).astype(BF16)
        bias = bs_ref[:, g:g + 1]
        for c in range(tm // SGU_CHUNK):
            rs = slice(c * SGU_CHUNK, (c + 1) * SGU_CHUNK)
            cs = slice(g * gd, (g + 1) * gd)
            mix = _dot(w, vn[rs, cs]) + bias
            o_ref[rs, cs] = (u_ref[rs, cs].astype(F32) * mix
                             * _silu(sz_ref[rs, cs].astype(F32))).astype(o_ref.dtype)


def _split_bf16(x):
    hi = x.astype(BF16)
    lo = (x - hi.astype(F32)).astype(BF16)
    return hi, lo


def _gla_kernel(qk_ref, v_ref, sz_ref, small_ref, wa_ref, ba_ref, og_ref, o_ref, st_ref, *, cpb):
    L = GLA_CHUNK

    @pl.when(pl.program_id(1) == 0)
    def _():
        st_ref[...] = jnp.zeros(st_ref.shape, F32)

    a_hi, a_lo = _split_bf16(small_ref[...])
    w_hi, w_lo = _split_bf16(wa_ref[...])
    xg = _dot(a_hi, w_hi) + _dot(a_lo, w_hi) + _dot(a_hi, w_lo) + ba_ref[...]
    log_a = (jnp.minimum(xg, 0.0) - jnp.log(1.0 + jnp.exp(-jnp.abs(xg)))) * (1.0 / GLA_TAU)
    row = lax.broadcasted_iota(jnp.int32, (L, L), 0)
    col = lax.broadcasted_iota(jnp.int32, (L, L), 1)
    causal = row >= col
    ones_tril = causal.astype(BF16)
    og = og_ref[...]
    states = [st_ref[hd] for hd in range(GLA_HEADS)]

    for c in range(cpb):
        rs = slice(c * L, (c + 1) * L)
        la_hi, la_lo = _split_bf16(log_a[rs, :])
        bcum = _dot(ones_tril, la_hi) + _dot(ones_tril, la_lo)
        for hd in range(GLA_HEADS):
            ks = slice(hd * GLA_DK, (hd + 1) * GLA_DK)
            k2 = slice(GLA_HEADS * GLA_DK + hd * GLA_DK, GLA_HEADS * GLA_DK + (hd + 1) * GLA_DK)
            vs = slice(hd * GLA_DV, (hd + 1) * GLA_DV)
            b = bcum[:, ks]
            b_mid = b[L // 2:L // 2 + 1, :]
            b_last = b[L - 1:L, :]
            q = qk_ref[rs, ks].astype(F32) * (GLA_DK ** -0.5)
            k = qk_ref[rs, k2].astype(F32)
            v = v_ref[rs, vs]
            q_t = (q * jnp.exp(b - b_mid)).astype(BF16)
            k_t = (k * jnp.exp(b_mid - b)).astype(BF16)
            attn = jnp.where(causal, _dot_nt(q_t, k_t), 0.0).astype(BF16)
            st = states[hd]
            o = _dot_nt((q * jnp.exp(b)).astype(BF16), st.astype(BF16)) + _dot(attn, v)
            k_s = (k * jnp.exp(b_last - b)).astype(BF16)
            states[hd] = jnp.exp(b_last) * st + _dot_tn(v, k_s)
            on = o * lax.rsqrt(jnp.mean(o * o, axis=-1, keepdims=True) + EPS) * og
            o_ref[rs, vs] = (on * _silu(sz_ref[rs, vs].astype(F32))).astype(o_ref.dtype)

    for hd in range(GLA_HEADS):
        st_ref[hd] = states[hd]


def _gla_call(p_lin, p_silu, small, wa_pad, ba, og, B, S, cpb):
    T = p_lin.shape[0]
    R = cpb * GLA_CHUNK
    ns = S // R
    return pl.pallas_call(
        functools.partial(_gla_kernel, cpb=cpb),
        grid=(B, ns),
        in_specs=[pl.BlockSpec((R, PROJ_TN), lambda b, c: (b * ns + c, LIN_QK_C)),
                  pl.BlockSpec((R, GLA_WIDTH), lambda b, c: (b * ns + c, LIN_V_C)),
                  pl.BlockSpec((R, GLA_WIDTH), lambda b, c: (b * ns + c, SILU_Z_C)),
                  pl.BlockSpec((R, SMALL_W), lambda b, c: (b * ns + c, 0)),
                  _resident((SMALL_W, GLA_HEADS * GLA_DK)),
                  _resident((1, GLA_HEADS * GLA_DK)),
                  _resident((1, GLA_DV))],
        out_specs=pl.BlockSpec((R, GLA_WIDTH), lambda b, c: (b * ns + c, 0)),
        out_shape=jax.ShapeDtypeStruct((T, GLA_WIDTH), BF16),
        scratch_shapes=[pltpu.VMEM((GLA_HEADS, GLA_DV, GLA_DK), F32)],
        compiler_params=_params("parallel", "arbitrary"),
    )(p_lin, p_lin, p_silu, small, wa_pad, ba, og)


def _merge_kernel(ya_ref, u_ref, v_ref, szb_ref, yc_ref, ga_ref, gb_ref, gc_ref,
                  vg_ref, ws_ref, bs_ref, wb_ref, o_ref, yb_ref, *, tm):
    _sgu_tile(u_ref, v_ref, szb_ref, vg_ref, ws_ref, bs_ref, yb_ref, tm)
    merged = _sigmoid(ga_ref[...].astype(F32)) * _dot(ya_ref[...], wb_ref[0])
    merged += _sigmoid(gb_ref[...].astype(F32)) * _dot(yb_ref[...], wb_ref[1])
    merged += _sigmoid(gc_ref[...].astype(F32)) * _dot(yc_ref[...], wb_ref[2])
    o_ref[...] = merged.astype(o_ref.dtype)


def _merge_call(ya, p_gelu, p_silu, yc, p_gate, vg, ws, bs_t, wb, layer, tm, S, tq):
    T = ya.shape[0]
    D = wb.shape[-1]
    assert SGU_WIDTH == BRANCH_WIDTH
    yspec = pl.BlockSpec((tm, BRANCH_WIDTH), lambda i: (i, 0))
    ya_spec = pl.BlockSpec((tm, BRANCH_WIDTH), lambda i: (_paired_block(i, S, tq, tm), 0))
    return pl.pallas_call(
        functools.partial(_merge_kernel, tm=tm),
        grid=(T // tm,),
        in_specs=[ya_spec,
                  pl.BlockSpec((tm, SGU_WIDTH), lambda i: (i, GELU_U_B)),
                  pl.BlockSpec((tm, SGU_WIDTH), lambda i: (i, GELU_V_B)),
                  pl.BlockSpec((tm, SGU_WIDTH), lambda i: (i, SILU_Z_B)),
                  yspec,
                  pl.BlockSpec((tm, D), lambda i: (i, 0)),
                  pl.BlockSpec((tm, D), lambda i: (i, 1)),
                  pl.BlockSpec((tm, D), lambda i: (i, 2)),
                  _resident((1, SGU_WIDTH)),
                  _resident((SGU_GROUPS, SGU_CHUNK, SGU_CHUNK)),
                  _resident((SGU_CHUNK, SGU_GROUPS)),
                  _resident((N_BRANCH, BRANCH_WIDTH, D), layer)],
        out_specs=pl.BlockSpec((tm, D), lambda i: (i, 0)),
        out_shape=jax.ShapeDtypeStruct((T, D), BF16),
        scratch_shapes=[pltpu.VMEM((tm, SGU_WIDTH), BF16)],
        compiler_params=_params("parallel"),
    )(ya, p_gelu, p_gelu, p_silu, yc, p_gate, p_gate, p_gate, vg, ws, bs_t, wb)


def _out_kernel(x_ref, m_ref, wo_ref, o_ref):
    o_ref[...] = x_ref[...] + _dot(m_ref[...], wo_ref[...])


def _out_call(x2, merged, wo, layer, tm):
    T, D = x2.shape
    return pl.pallas_call(
        _out_kernel,
        grid=(T // tm,),
        in_specs=[pl.BlockSpec((tm, D), lambda i: (i, 0)),
                  pl.BlockSpec((tm, D), lambda i: (i, 0)),
                  _resident((D, D), layer)],
        out_specs=pl.BlockSpec((tm, D), lambda i: (i, 0)),
        out_shape=jax.ShapeDtypeStruct((T, D), F32),
        compiler_params=_params("parallel"),
    )(x2, merged, wo)


_IN_SIZES = (("c_q", MLA_Q_RANK), ("c_kv", MLA_KV_RANK), ("k_rope", MLA_ROPE), ("z_a", MLA_WIDTH),
             ("u_b", SGU_WIDTH), ("v_b", SGU_WIDTH), ("z_b", SGU_WIDTH),
             ("q_c", GLA_HEADS * GLA_DK), ("k_c", GLA_HEADS * GLA_DK), ("v_c", GLA_WIDTH),
             ("a_r", GLA_GATE_RANK), ("z_c", GLA_WIDTH), ("gates", N_BRANCH * D_MODEL))
_IN_SEG = {}
_off = 0
for _name, _size in _IN_SIZES:
    _IN_SEG[_name] = (_off, _size)
    _off += _size
IN_COLS = _off
_W_GROUPS = (("c_q", "c_kv", "q_c", "k_c", "v_c"), ("z_a", "z_b", "z_c"), ("u_b", "v_b"),
             ("gates",), ("k_rope", "a_r"))


def _wprep_kernel(w_ref, lin_ref, silu_ref, gelu_ref, gate_ref, small_ref):
    small_ref[...] = jnp.zeros(small_ref.shape, small_ref.dtype)
    for o_ref, names in zip((lin_ref, silu_ref, gelu_ref, gate_ref, small_ref), _W_GROUPS):
        dst = 0
        for name in names:
            src, size = _IN_SEG[name]
            o_ref[dst:dst + size, :] = w_ref[src:src + size, :].astype(o_ref.dtype)
            dst += size


def _split_w_in(w_in, tk):
    depth, D, cols = w_in.shape
    assert cols == IN_COLS
    widths = [sum(_IN_SEG[n][1] for n in names) for names in _W_GROUPS[:-1]] + [SMALL_W]
    return pl.pallas_call(
        _wprep_kernel,
        grid=(depth, D // tk),
        in_specs=[pl.BlockSpec((None, cols, tk), lambda l, r: (l, 0, r))],
        out_specs=[pl.BlockSpec((None, n, tk), lambda l, r: (l, 0, r)) for n in widths],
        out_shape=[jax.ShapeDtypeStruct((depth, n, D), BF16) for n in widths],
        compiler_params=_params("parallel", "parallel"),
    )(jnp.swapaxes(w_in, 1, 2))


def _pad_heads(w, per_head, lo, hi, width):
    r = w.shape[0]
    w = w.reshape(r, MLA_HEADS, per_head)[:, :, lo:hi]
    w = jnp.pad(w, ((0, 0), (0, 0), (0, width - (hi - lo))))
    return w.reshape(r, MLA_HEADS * width)


def _rope_tables(positions):
    half = MLA_ROPE // 2
    inv_freq = 1.0 / (ROPE_THETA ** (jnp.arange(0, MLA_ROPE, 2, dtype=F32) / MLA_ROPE))
    inv_freq = jnp.concatenate([inv_freq, inv_freq, jnp.zeros((LANE - 2 * half,), F32)])
    ang = positions.astype(F32).reshape(-1, 1) * inv_freq
    lane = jnp.arange(LANE)
    cos, sin = jnp.cos(ang), jnp.sin(ang)
    return (jnp.where(lane < 2 * half, cos, 0.0),
            jnp.where(lane < half, -sin, jnp.where(lane < 2 * half, sin, 0.0)))


def _pick_tile(n, want):
    t = min(n, want)
    while n % t:
        t //= 2
    return t


def kernel(x, positions, norm_g, w_in, mla_cq_norm, mla_ckv_norm, mla_w_uq, mla_w_ukv,
           mla_q_norm, mla_k_norm, sgu_v_norm, sgu_w_s, sgu_b_s, gla_w_a2, gla_b_a,
           gla_o_norm, w_branch, w_out):
    B, S, D = x.shape
    T = B * S
    depth = w_in.shape[0]
    tm_proj = _pick_tile(T, 2048)
    tm_row = _pick_tile(T, 512)
    tm_big = _pick_tile(T, 1024)
    tq = _pick_tile(S, 512)
    cpb = _pick_tile(S // GLA_CHUNK, 8)

    cosf, sins = _rope_tables(positions)
    x2 = x.reshape(T, D)
    w_lin, w_silu, w_gelu, w_gate, w_small = _split_w_in(w_in, _pick_tile(D, 256))
    wb, wo = w_branch.astype(BF16), w_out.astype(BF16)
    for l in range(depth):
        wuq = _pad_heads(mla_w_uq[l], MLA_QK, 0, MLA_QK, MLA_QK_PAD).astype(BF16)
        wuk = _pad_heads(mla_w_ukv[l], MLA_NOPE + MLA_V, 0, MLA_NOPE, MLA_NOPE).astype(BF16)
        wuv = _pad_heads(mla_w_ukv[l], MLA_NOPE + MLA_V, MLA_NOPE, MLA_NOPE + MLA_V, MLA_V).T.astype(BF16)
        qg = jnp.pad(mla_q_norm[l], (0, MLA_QK_PAD - MLA_QK)).reshape(1, MLA_QK_PAD)
        kg = jnp.pad(mla_k_norm[l], (0, MLA_QK_PAD - MLA_QK)).reshape(1, MLA_QK_PAD)
        wa_pad = jnp.zeros((SMALL_W, GLA_HEADS * GLA_DK), F32).at[
            MLA_ROPE:MLA_ROPE + GLA_GATE_RANK].set(gla_w_a2[l])

        h, small = _norm_call(x2, norm_g[l].reshape(1, D), w_small, l, tm_big)
        p_lin = _inproj_call(h, w_lin, l, _identity, tm_proj)
        p_silu = _inproj_call(h, w_silu, l, _identity, tm_proj)
        p_gelu = _inproj_call(h, w_gelu, l, _gelu_tanh, tm_proj)
        p_gate = _inproj_call(h, w_gate, l, _identity, tm_proj)
        q, k, vt = _mla_prep_call(p_lin, small, cosf, sins,
                                  mla_cq_norm[l].reshape(1, -1), mla_ckv_norm[l].reshape(1, -1),
                                  wuq, wuk, wuv, qg, kg, tm_big, tq)
        ya = _attn_call(q, k, vt, p_silu, B, S, tq)
        yc = _gla_call(p_lin, p_silu, small, wa_pad, gla_b_a[l].reshape(1, -1),
                       gla_o_norm[l].reshape(1, -1), B, S, cpb)
        merged = _merge_call(ya, p_gelu, p_silu, yc, p_gate, sgu_v_norm[l].reshape(1, -1), sgu_w_s[l],
                             sgu_b_s[l].T, wb, l, tm_row, S, tq)
        x2 = _out_call(x2, merged, wo, l, tm_row)
    return x2.reshape(B, S, D)
```

```python
import functools
import math

import jax
import jax.numpy as jnp
from jax import lax
from jax.experimental import pallas as pl
from jax.experimental.pallas import tpu as pltpu

F32 = jnp.float32
BF16 = jnp.bfloat16

MLA_HEADS = 8
MLA_NOPE = 128
MLA_ROPE = 64
MLA_QK = MLA_NOPE + MLA_ROPE
MLA_V = 128
MLA_Q_RANK = 512
MLA_KV_RANK = 512
MLA_WIDTH = MLA_HEADS * MLA_V
MLA_QK_PAD = 256
MLA_VT_ROWS = MLA_V + 16
SUBLANE = 8
ORDER_LAG = 3
ROPE_THETA = 10000.0
SGU_GROUPS = 8
SGU_CHUNK = 128
SGU_WIDTH = 1024
GLA_HEADS = 4
GLA_DK = 128
GLA_DV = 256
GLA_GATE_RANK = 16
GLA_TAU = 16.0
GLA_CHUNK = 128
GLA_WIDTH = GLA_HEADS * GLA_DV
D_MODEL = 2048
N_BRANCH = 3
BRANCH_WIDTH = 1024
EPS = 1e-6
LANE = 128
LOG2_E = math.log2(math.e)
MASK_VALUE = -1e30
SMALL_W = LANE

PROJ_TN = 1024
LIN_CQKV, LIN_QK_C, LIN_V_C = range(3)
SILU_Z_A, SILU_Z_B, SILU_Z_C = range(3)
GELU_U_B, GELU_V_B = range(2)

VMEM_LIMIT = 56 * 1024 * 1024


def _params(*sem):
    return pltpu.CompilerParams(dimension_semantics=sem, vmem_limit_bytes=VMEM_LIMIT)


def _dot(a, b):
    return jnp.dot(a, b, preferred_element_type=F32)


def _dot_nt(a, b):
    return lax.dot_general(a, b, (((1,), (1,)), ((), ())), preferred_element_type=F32)


def _dot_tn(a, b):
    return lax.dot_general(a, b, (((0,), (0,)), ((), ())), preferred_element_type=F32)


def _sigmoid(x):
    return 0.5 * jnp.tanh(0.5 * x) + 0.5


def _gelu_tanh(x):
    c = math.sqrt(2.0 / math.pi)
    hx = 0.5 * x
    return hx * jnp.tanh(x * (c + (c * 0.044715) * (x * x))) + hx


def _silu(x):
    hx = 0.5 * x
    return hx * jnp.tanh(hx) + hx


def _identity(x):
    return x


def _resident(shape, layer=None):
    nd = len(shape)
    if layer is None:
        return pl.BlockSpec(shape, lambda *_: (0,) * nd, pipeline_mode=pl.Buffered(1))
    return pl.BlockSpec((None,) + tuple(shape), lambda *_: (layer,) + (0,) * nd,
                        pipeline_mode=pl.Buffered(1))


def _norm_kernel(x_ref, g_ref, ws_ref, h_ref, small_ref):
    x = x_ref[...]
    rstd = lax.rsqrt(jnp.mean(x * x, axis=-1, keepdims=True) + EPS)
    h = (x * rstd * g_ref[...]).astype(BF16)
    h_ref[...] = h
    small_ref[...] = _dot_nt(h, ws_ref[...])


def _norm_call(x2, g, w_small, layer, tm):
    T, D = x2.shape
    return pl.pallas_call(
        _norm_kernel,
        grid=(T // tm,),
        in_specs=[pl.BlockSpec((tm, D), lambda i: (i, 0)),
                  _resident((1, D)),
                  _resident((SMALL_W, D), layer)],
        out_specs=[pl.BlockSpec((tm, D), lambda i: (i, 0)),
                   pl.BlockSpec((tm, SMALL_W), lambda i: (i, 0))],
        out_shape=[jax.ShapeDtypeStruct((T, D), BF16),
                   jax.ShapeDtypeStruct((T, SMALL_W), F32)],
        compiler_params=_params("parallel"),
    )(x2, g, w_small)


def _inproj_kernel(h_ref, w_ref, o_ref, *, act):
    o_ref[...] = act(_dot_nt(h_ref[...], w_ref[...])).astype(o_ref.dtype)


def _inproj_call(h, w, layer, act, tm):
    T, D = h.shape
    N = w.shape[1]
    return pl.pallas_call(
        functools.partial(_inproj_kernel, act=act),
        grid=(T // tm, N // PROJ_TN),
        in_specs=[pl.BlockSpec((tm, D), lambda i, j: (i, 0)),
                  pl.BlockSpec((None, PROJ_TN, D), lambda i, j: (layer, j, 0))],
        out_specs=pl.BlockSpec((tm, PROJ_TN), lambda i, j: (i, j)),
        out_shape=jax.ShapeDtypeStruct((T, N), BF16),
        compiler_params=_params("parallel", "arbitrary"),
    )(h, w)


def _rope128(r, cosf, sins):
    lane = lax.broadcasted_iota(jnp.int32, r.shape, 1)
    partner = jnp.where(lane < MLA_ROPE // 2,
                        pltpu.roll(r, LANE - MLA_ROPE // 2, axis=1),
                        pltpu.roll(r, MLA_ROPE // 2, axis=1))
    return r * cosf + partner * sins


def _mla_prep_kernel(cq_ref, ckv_ref, small_ref, cos_ref, sin_ref,
                     cqg_ref, ckvg_ref, wuq_ref, wuk_ref, wuv_ref, qg_ref, kg_ref,
                     q_ref, k_ref, vt_ref, *, tq):
    cosf = cos_ref[...]
    sins = sin_ref[...]
    scale = MLA_QK ** -0.5 * LOG2_E

    cq = cq_ref[...].astype(F32)
    nq = cq * lax.rsqrt(jnp.mean(cq * cq, axis=-1, keepdims=True) + EPS) * cqg_ref[...]
    q_all = _dot(nq.astype(BF16), wuq_ref[...])

    ckv = ckv_ref[...].astype(F32)
    nkv = ckv * lax.rsqrt(jnp.mean(ckv * ckv, axis=-1, keepdims=True) + EPS) * ckvg_ref[...]
    nkv = nkv.astype(BF16)
    k_all = _dot(nkv, wuk_ref[...])
    vt = _dot_nt(wuv_ref[...], nkv)
    ones_row = (lax.broadcasted_iota(jnp.int32, (MLA_VT_ROWS - MLA_V, tq), 0) == 0)
    for sub in range(vt.shape[1] // tq):
        for hd in range(MLA_HEADS):
            vt_ref[sub, hd, :MLA_V, :] = vt[hd * MLA_V:(hd + 1) * MLA_V,
                                            sub * tq:(sub + 1) * tq].astype(vt_ref.dtype)
            vt_ref[sub, hd, MLA_V:, :] = ones_row.astype(vt_ref.dtype)

    qg = qg_ref[...]
    kg = kg_ref[...]
    lane = lax.broadcasted_iota(jnp.int32, (1, LANE), 1)
    kr = jnp.where(lane < MLA_ROPE, small_ref[...], 0.0)
    kr_ss = jnp.sum(kr * kr, axis=-1, keepdims=True)
    kr_rot = _rope128(kr * kg[:, MLA_NOPE:], cosf, sins)

    for hd in range(MLA_HEADS):
        qb = q_all[:, hd * MLA_QK_PAD:(hd + 1) * MLA_QK_PAD]
        rq = lax.rsqrt(jnp.sum(qb * qb, axis=-1, keepdims=True) * (1.0 / MLA_QK) + EPS) * scale
        qn = qb * rq * qg
        q_ref[:, hd * MLA_QK_PAD:hd * MLA_QK_PAD + MLA_NOPE] = qn[:, :MLA_NOPE].astype(q_ref.dtype)
        q_ref[:, hd * MLA_QK_PAD + MLA_NOPE:(hd + 1) * MLA_QK_PAD] = _rope128(
            qn[:, MLA_NOPE:], cosf, sins).astype(q_ref.dtype)

        kb = k_all[:, hd * MLA_NOPE:(hd + 1) * MLA_NOPE]
        rk = lax.rsqrt((jnp.sum(kb * kb, axis=-1, keepdims=True) + kr_ss) * (1.0 / MLA_QK) + EPS)
        k_ref[:, hd * MLA_QK_PAD:hd * MLA_QK_PAD + MLA_NOPE] = (
            kb * rk * kg[:, :MLA_NOPE]).astype(k_ref.dtype)
        k_ref[:, hd * MLA_QK_PAD + MLA_NOPE:(hd + 1) * MLA_QK_PAD] = (kr_rot * rk).astype(k_ref.dtype)


def _mla_prep_call(proj, small, cosf, sins, cqg, ckvg, wuq, wuk, wuv, qg, kg, tm, tq):
    T = proj.shape[0]
    assert tm % tq == 0
    HQ = MLA_HEADS * MLA_QK_PAD
    return pl.pallas_call(
        functools.partial(_mla_prep_kernel, tq=tq),
        grid=(T // tm,),
        in_specs=[pl.BlockSpec((tm, MLA_Q_RANK), lambda i: (i, 0)),
                  pl.BlockSpec((tm, MLA_KV_RANK), lambda i: (i, 1)),
                  pl.BlockSpec((tm, SMALL_W), lambda i: (i, 0)),
                  pl.BlockSpec((tm, LANE), lambda i: (i, 0)),
                  pl.BlockSpec((tm, LANE), lambda i: (i, 0)),
                  _resident((1, MLA_Q_RANK)),
                  _resident((1, MLA_KV_RANK)),
                  _resident((MLA_Q_RANK, HQ)),
                  _resident((MLA_KV_RANK, MLA_HEADS * MLA_NOPE)),
                  _resident((MLA_WIDTH, MLA_KV_RANK)),
                  _resident((1, MLA_QK_PAD)),
                  _resident((1, MLA_QK_PAD))],
        out_specs=[pl.BlockSpec((tm, HQ), lambda i: (i, 0)),
                   pl.BlockSpec((tm, HQ), lambda i: (i, 0)),
                   pl.BlockSpec((tm // tq, MLA_HEADS, MLA_VT_ROWS, tq), lambda i: (i, 0, 0, 0))],
        out_shape=[jax.ShapeDtypeStruct((T, HQ), BF16),
                   jax.ShapeDtypeStruct((T, HQ), BF16),
                   jax.ShapeDtypeStruct((T // tq, MLA_HEADS, MLA_VT_ROWS, tq), BF16)],
        compiler_params=_params("parallel"),
    )(proj, proj, small, cosf, sins, cqg, ckvg, wuq, wuk, wuv, qg, kg)


def _attn_kernel(qlo_ref, qhi_ref, k_ref, vt_ref, szlo_ref, szhi_ref, o_ref,
                 q2_ref, s_ref, mf_ref, pv_ref, *, tq, nq, n_steps):
    g = pl.program_id(0)
    half = nq // 2
    nt = nq + 1
    i1 = jnp.minimum(g, n_steps - 1) % half
    i2 = jnp.maximum(g - 1, 0) % half

    @pl.when(g == 0)
    def _():
        for t in range(nt):
            s_ref[t] = jnp.zeros((tq, tq), F32)
            mf_ref[t] = jnp.zeros((SUBLANE, tq), F32)

    q2_ref[0] = qlo_ref[...]
    q2_ref[1] = qhi_ref[...]

    def is_lo(p, i):
        return True if p == 0 else (False if p >= half else p <= i)

    def pick(p, i, lo, hi):
        c = is_lo(p, i)
        if c is True:
            return lo
        if c is False:
            return hi
        return jnp.where(c, lo, hi)

    def kv_tile(p, i):
        if p == 0:
            return i
        if p == nq:
            return nq - 1 - i
        return pick(p, i, p - 1, p - i - 1)

    def combine(ref, shape, i, op, init):
        lo = jnp.full(shape, init, F32)
        hi = jnp.full(shape, init, F32)
        for p in range(nt):
            c = is_lo(p, i)
            x = ref[p]
            if c is True:
                lo = op(lo, x)
            elif c is False:
                hi = op(hi, x)
            else:
                lo = op(lo, jnp.where(c, x, init))
                hi = op(hi, jnp.where(c, init, x))
        return lo, hi

    m_lo, m_hi = combine(mf_ref, (SUBLANE, tq), i2, jnp.maximum, MASK_VALUE)
    m_lo = jnp.max(m_lo, axis=0, keepdims=True)
    m_hi = jnp.max(m_hi, axis=0, keepdims=True)

    causal = (lax.broadcasted_iota(jnp.int32, (tq, tq), 0)
              <= lax.broadcasted_iota(jnp.int32, (tq, tq), 1))
    order_zeros = [jnp.zeros((1, tq), F32)] * ORDER_LAG
    for p in range(nt):
        e = jnp.exp2(s_ref[p] - (pick(p, i2, m_lo, m_hi) + order_zeros[0]))
        pv_ref[p] = _dot(vt_ref[kv_tile(p, i2)], e.astype(BF16))
        j = kv_tile(p, i1)
        s = _dot_nt(k_ref[pl.ds(pl.multiple_of(j * tq, tq), tq), :], q2_ref[pick(p, i1, 0, 1)])
        if p == 0 or p == nq:
            s = jnp.where(causal, s, MASK_VALUE)
        s_ref[p] = s
        mf = jnp.max(s.reshape(tq // SUBLANE, SUBLANE, tq), axis=0)
        mf_ref[p] = mf
        bits = lax.bitcast_convert_type(mf[:1, :], jnp.uint32)
        order_zeros = order_zeros[1:] + [lax.bitcast_convert_type((bits >> 16) >> 16, F32)]

    a_lo, a_hi = combine(pv_ref, (MLA_VT_ROWS, tq), i2, jnp.add, 0.0)
    for a, sz_ref, rows in ((a_lo, szlo_ref, slice(0, tq)), (a_hi, szhi_ref, slice(tq, 2 * tq))):
        ot = a[:MLA_V, :] / a[MLA_V:MLA_V + 1, :]
        o_ref[rows, :] = (ot.T * _silu(sz_ref[...].astype(F32))).astype(o_ref.dtype)


def _attn_call(q, k, vt, p_silu, B, S, tq):
    T = q.shape[0]
    nq = S // tq
    assert nq % 2 == 0 and MLA_V == LANE
    half = nq // 2
    n_steps = B * MLA_HEADS * half
    sz_col0 = SILU_Z_A * PROJ_TN // MLA_V

    def bhi(g):
        bh, i = g // half, g % half
        return bh // MLA_HEADS, bh % MLA_HEADS, i

    def first(g):
        return bhi(jnp.minimum(g, n_steps - 1))

    def second(g):
        return bhi(jnp.maximum(g - 1, 0))

    def lo_blk(f, col0=0):
        def index(g):
            b, h, i = f(g)
            return b * nq + i, col0 + h
        return index

    def hi_blk(f, col0=0):
        def index(g):
            b, h, i = f(g)
            return b * nq + nq - 1 - i, col0 + h
        return index

    def seq_blk(f):
        def index(g):
            b, h, _ = f(g)
            return b, h
        return index

    def seq_blk4(f):
        def index(g):
            b, h, _ = f(g)
            return b, h, 0, 0
        return index

    def out_blk(g):
        b, h, i = second(g)
        return b * half + i, h

    nt = nq + 1
    return pl.pallas_call(
        functools.partial(_attn_kernel, tq=tq, nq=nq, n_steps=n_steps),
        grid=(n_steps + 1,),
        in_specs=[pl.BlockSpec((tq, MLA_QK_PAD), lo_blk(first)),
                  pl.BlockSpec((tq, MLA_QK_PAD), hi_blk(first)),
                  pl.BlockSpec((S, MLA_QK_PAD), seq_blk(first)),
                  pl.BlockSpec((nq, None, MLA_VT_ROWS, tq), seq_blk4(second)),
                  pl.BlockSpec((tq, MLA_V), lo_blk(second, sz_col0)),
                  pl.BlockSpec((tq, MLA_V), hi_blk(second, sz_col0))],
        out_specs=pl.BlockSpec((2 * tq, MLA_V), out_blk),
        out_shape=jax.ShapeDtypeStruct((T, MLA_WIDTH), BF16),
        scratch_shapes=[pltpu.VMEM((2, tq, MLA_QK_PAD), BF16),
                        pltpu.VMEM((nt, tq, tq), F32),
                        pltpu.VMEM((nt, SUBLANE, tq), F32),
                        pltpu.VMEM((nt, MLA_VT_ROWS, tq), F32)],
        compiler_params=_params("arbitrary"),
    )(q, q, k, vt, p_silu, p_silu)


def _paired_block(r, S, tq, tm):
    per_seq, per_tile, nq = S // tm, tq // tm, S // tq
    b, w = r // per_seq, r % per_seq
    u, sub = w // per_tile, w % per_tile
    pos = jnp.where(u < nq // 2, 2 * u, 2 * (nq - 1 - u) + 1)
    return b * per_seq + pos * per_tile + sub


def _sgu_tile(u_ref, v_ref, sz_ref, vg_ref, ws_ref, bs_ref, o_ref, tm):
    v = v_ref[...].astype(F32)
    vn = (v * lax.rsqrt(jnp.mean(v * v, axis=-1, keepdims=True) + EPS) * vg_ref[...]).astype(BF16)
    row = lax.broadcasted_iota(jnp.int32, (SGU_CHUNK, SGU_CHUNK), 0)
    col = lax.broadcasted_iota(jnp.int32, (SGU_CHUNK, SGU_CHUNK), 1)
    gd = SGU_WIDTH // SGU_GROUPS
    for g in range(SGU_GROUPS):
        w = jnp.where(row >= col, ws_ref[g], 0.0).astype(BF16)
        bias = bs_ref[:, g:g + 1]
        for c in range(tm // SGU_CHUNK):
            rs = slice(c * SGU_CHUNK, (c + 1) * SGU_CHUNK)
            cs = slice(g * gd, (g + 1) * gd)
            mix = _dot(w, vn[rs, cs]) + bias
            o_ref[rs, cs] = (u_ref[rs, cs].astype(F32) * mix
                             * _silu(sz_ref[rs, cs].astype(F32))).astype(o_ref.dtype)


def _split_bf16(x):
    hi = x.astype(BF16)
    lo = (x - hi.astype(F32)).astype(BF16)
    return hi, lo


def _gla_kernel(qk_ref, v_ref, sz_ref, small_ref, wa_ref, ba_ref, og_ref, o_ref, st_ref, *, cpb):
    L = GLA_CHUNK

    @pl.when(pl.program_id(1) == 0)
    def _():
        st_ref[...] = jnp.zeros(st_ref.shape, F32)

    a_hi, a_lo = _split_bf16(small_ref[...])
    w_hi, w_lo = _split_bf16(wa_ref[...])
    xg = _dot(a_hi, w_hi) + _dot(a_lo, w_hi) + _dot(a_hi, w_lo) + ba_ref[...]
    log_a = (jnp.minimum(xg, 0.0) - jnp.log(1.0 + jnp.exp(-jnp.abs(xg)))) * (1.0 / GLA_TAU)
    row = lax.broadcasted_iota(jnp.int32, (L, L), 0)
    col = lax.broadcasted_iota(jnp.int32, (L, L), 1)
    causal = row >= col
    ones_tril = causal.astype(BF16)
    og = og_ref[...]
    states = [st_ref[hd] for hd in range(GLA_HEADS)]

    for c in range(cpb):
        rs = slice(c * L, (c + 1) * L)
        la_hi, la_lo = _split_bf16(log_a[rs, :])
        bcum = _dot(ones_tril, la_hi) + _dot(ones_tril, la_lo)
        for hd in range(GLA_HEADS):
            ks = slice(hd * GLA_DK, (hd + 1) * GLA_DK)
            k2 = slice(GLA_HEADS * GLA_DK + hd * GLA_DK, GLA_HEADS * GLA_DK + (hd + 1) * GLA_DK)
            vs = slice(hd * GLA_DV, (hd + 1) * GLA_DV)
            b = bcum[:, ks]
            b_mid = b[L // 2:L // 2 + 1, :]
            b_last = b[L - 1:L, :]
            q = qk_ref[rs, ks].astype(F32) * (GLA_DK ** -0.5)
            k = qk_ref[rs, k2].astype(F32)
            v = v_ref[rs, vs]
            q_t = (q * jnp.exp(b - b_mid)).astype(BF16)
            k_t = (k * jnp.exp(b_mid - b)).astype(BF16)
            attn = jnp.where(causal, _dot_nt(q_t, k_t), 0.0).astype(BF16)
            st = states[hd]
            o = _dot_nt((q * jnp.exp(b)).astype(BF16), st.astype(BF16)) + _dot(attn, v)
            k_s = (k * jnp.exp(b_last - b)).astype(BF16)
            states[hd] = jnp.exp(b_last) * st + _dot_tn(v, k_s)
            on = o * lax.rsqrt(jnp.mean(o * o, axis=-1, keepdims=True) + EPS) * og
            o_ref[rs, vs] = (on * _silu(sz_ref[rs, vs].astype(F32))).astype(o_ref.dtype)

    for hd in range(GLA_HEADS):
        st_ref[hd] = states[hd]


def _gla_call(p_lin, p_silu, small, wa_pad, ba, og, B, S, cpb):
    T = p_lin.shape[0]
    R = cpb * GLA_CHUNK
    ns = S // R
    return pl.pallas_call(
        functools.partial(_gla_kernel, cpb=cpb),
        grid=(B, ns),
        in_specs=[pl.BlockSpec((R, PROJ_TN), lambda b, c: (b * ns + c, LIN_QK_C)),
                  pl.BlockSpec((R, GLA_WIDTH), lambda b, c: (b * ns + c, LIN_V_C)),
                  pl.BlockSpec((R, GLA_WIDTH), lambda b, c: (b * ns + c, SILU_Z_C)),
                  pl.BlockSpec((R, SMALL_W), lambda b, c: (b * ns + c, 0)),
                  _resident((SMALL_W, GLA_HEADS * GLA_DK)),
                  _resident((1, GLA_HEADS * GLA_DK)),
                  _resident((1, GLA_DV))],
        out_specs=pl.BlockSpec((R, GLA_WIDTH), lambda b, c: (b * ns + c, 0)),
        out_shape=jax.ShapeDtypeStruct((T, GLA_WIDTH), BF16),
        scratch_shapes=[pltpu.VMEM((GLA_HEADS, GLA_DV, GLA_DK), F32)],
        compiler_params=_params("parallel", "arbitrary"),
    )(p_lin, p_lin, p_silu, small, wa_pad, ba, og)


def _merge_kernel(ya_ref, u_ref, v_ref, szb_ref, yc_ref, ga_ref, gb_ref, gc_ref,
                  vg_ref, ws_ref, bs_ref, wb_ref, o_ref, yb_ref, *, tm):
    _sgu_tile(u_ref, v_ref, szb_ref, vg_ref, ws_ref, bs_ref, yb_ref, tm)
    merged = _sigmoid(ga_ref[...].astype(F32)) * _dot(ya_ref[...], wb_ref[0])
    merged += _sigmoid(gb_ref[...].astype(F32)) * _dot(yb_ref[...], wb_ref[1])
    merged += _sigmoid(gc_ref[...].astype(F32)) * _dot(yc_ref[...], wb_ref[2])
    o_ref[...] = merged.astype(o_ref.dtype)


def _merge_call(ya, p_gelu, p_silu, yc, p_gate, vg, ws, bs_t, wb, layer, tm, S, tq):
    T = ya.shape[0]
    D = wb.shape[-1]
    assert SGU_WIDTH == BRANCH_WIDTH
    yspec = pl.BlockSpec((tm, BRANCH_WIDTH), lambda i: (i, 0))
    ya_spec = pl.BlockSpec((tm, BRANCH_WIDTH), lambda i: (_paired_block(i, S, tq, tm), 0))
    return pl.pallas_call(
        functools.partial(_merge_kernel, tm=tm),
        grid=(T // tm,),
        in_specs=[ya_spec,
                  pl.BlockSpec((tm, SGU_WIDTH), lambda i: (i, GELU_U_B)),
                  pl.BlockSpec((tm, SGU_WIDTH), lambda i: (i, GELU_V_B)),
                  pl.BlockSpec((tm, SGU_WIDTH), lambda i: (i, SILU_Z_B)),
                  yspec,
                  pl.BlockSpec((tm, D), lambda i: (i, 0)),
                  pl.BlockSpec((tm, D), lambda i: (i, 1)),
                  pl.BlockSpec((tm, D), lambda i: (i, 2)),
                  _resident((1, SGU_WIDTH)),
                  _resident((SGU_GROUPS, SGU_CHUNK, SGU_CHUNK)),
                  _resident((SGU_CHUNK, SGU_GROUPS)),
                  _resident((N_BRANCH, BRANCH_WIDTH, D), layer)],
        out_specs=pl.BlockSpec((tm, D), lambda i: (i, 0)),
        out_shape=jax.ShapeDtypeStruct((T, D), BF16),
        scratch_shapes=[pltpu.VMEM((tm, SGU_WIDTH), BF16)],
        compiler_params=_params("parallel"),
    )(ya, p_gelu, p_gelu, p_silu, yc, p_gate, p_gate, p_gate, vg, ws, bs_t, wb)


def _out_kernel(x_ref, m_ref, wo_ref, o_ref):
    o_ref[...] = x_ref[...] + _dot(m_ref[...], wo_ref[...])


def _out_call(x2, merged, wo, layer, tm):
    T, D = x2.shape
    return pl.pallas_call(
        _out_kernel,
        grid=(T // tm,),
        in_specs=[pl.BlockSpec((tm, D), lambda i: (i, 0)),
                  pl.BlockSpec((tm, D), lambda i: (i, 0)),
                  _resident((D, D), layer)],
        out_specs=pl.BlockSpec((tm, D), lambda i: (i, 0)),
        out_shape=jax.ShapeDtypeStruct((T, D), F32),
        compiler_params=_params("parallel"),
    )(x2, merged, wo)


_IN_SIZES = (("c_q", MLA_Q_RANK), ("c_kv", MLA_KV_RANK), ("k_rope", MLA_ROPE), ("z_a", MLA_WIDTH),
             ("u_b", SGU_WIDTH), ("v_b", SGU_WIDTH), ("z_b", SGU_WIDTH),
             ("q_c", GLA_HEADS * GLA_DK), ("k_c", GLA_HEADS * GLA_DK), ("v_c", GLA_WIDTH),
             ("a_r", GLA_GATE_RANK), ("z_c", GLA_WIDTH), ("gates", N_BRANCH * D_MODEL))
_IN_SEG = {}
_off = 0
for _name, _size in _IN_SIZES:
    _IN_SEG[_name] = (_off, _size)
    _off += _size
IN_COLS = _off
_W_GROUPS = (("c_q", "c_kv", "q_c", "k_c", "v_c"), ("z_a", "z_b", "z_c"), ("u_b", "v_b"),
             ("gates",), ("k_rope", "a_r"))


def _wprep_kernel(w_ref, lin_ref, silu_ref, gelu_ref, gate_ref, small_ref):
    small_ref[...] = jnp.zeros(small_ref.shape, small_ref.dtype)
    for o_ref, names in zip((lin_ref, silu_ref, gelu_ref, gate_ref, small_ref), _W_GROUPS):
        dst = 0
        for name in names:
            src, size = _IN_SEG[name]
            o_ref[dst:dst + size, :] = w_ref[src:src + size, :].astype(o_ref.dtype)
            dst += size


def _split_w_in(w_in, tk):
    depth, D, cols = w_in.shape
    assert cols == IN_COLS
    widths = [sum(_IN_SEG[n][1] for n in names) for names in _W_GROUPS[:-1]] + [SMALL_W]
    return pl.pallas_call(
        _wprep_kernel,
        grid=(depth, D // tk),
        in_specs=[pl.BlockSpec((None, cols, tk), lambda l, r: (l, 0, r))],
        out_specs=[pl.BlockSpec((None, n, tk), lambda l, r: (l, 0, r)) for n in widths],
        out_shape=[jax.ShapeDtypeStruct((depth, n, D), BF16) for n in widths],
        compiler_params=_params("parallel", "parallel"),
    )(jnp.swapaxes(w_in, 1, 2))


def _pad_heads(w, per_head, lo, hi, width):
    r = w.shape[0]
    w = w.reshape(r, MLA_HEADS, per_head)[:, :, lo:hi]
    w = jnp.pad(w, ((0, 0), (0, 0), (0, width - (hi - lo))))
    return w.reshape(r, MLA_HEADS * width)


def _rope_tables(positions):
    half = MLA_ROPE // 2
    inv_freq = 1.0 / (ROPE_THETA ** (jnp.arange(0, MLA_ROPE, 2, dtype=F32) / MLA_ROPE))
    inv_freq = jnp.concatenate([inv_freq, inv_freq, jnp.zeros((LANE - 2 * half,), F32)])
    ang = positions.astype(F32).reshape(-1, 1) * inv_freq
    lane = jnp.arange(LANE)
    cos, sin = jnp.cos(ang), jnp.sin(ang)
    return (jnp.where(lane < 2 * half, cos, 0.0),
            jnp.where(lane < half, -sin, jnp.where(lane < 2 * half, sin, 0.0)))


def _pick_tile(n, want):
    t = min(n, want)
    while n % t:
        t //= 2
    return t


def kernel(x, positions, norm_g, w_in, mla_cq_norm, mla_ckv_norm, mla_w_uq, mla_w_ukv,
           mla_q_norm, mla_k_norm, sgu_v_norm, sgu_w_s, sgu_b_s, gla_w_a2, gla_b_a,
           gla_o_norm, w_branch, w_out):
    B, S, D = x.shape
    T = B * S
    depth = w_in.shape[0]
    tm_proj = _pick_tile(T, 2048)
    tm_row = _pick_tile(T, 512)
    tm_big = _pick_tile(T, 1024)
    tq = _pick_tile(S, 512)
    cpb = _pick_tile(S // GLA_CHUNK, 16)

    cosf, sins = _rope_tables(positions)
    x2 = x.reshape(T, D)
    w_lin, w_silu, w_gelu, w_gate, w_small = _split_w_in(w_in, _pick_tile(D, 256))
    wb, wo = w_branch.astype(BF16), w_out.astype(BF16)
    for l in range(depth):
        wuq = _pad_heads(mla_w_uq[l], MLA_QK, 0, MLA_QK, MLA_QK_PAD).astype(BF16)
        wuk = _pad_heads(mla_w_ukv[l], MLA_NOPE + MLA_V, 0, MLA_NOPE, MLA_NOPE).astype(BF16)
        wuv = _pad_heads(mla_w_ukv[l], MLA_NOPE + MLA_V, MLA_NOPE, MLA_NOPE + MLA_V, MLA_V).T.astype(BF16)
        qg = jnp.pad(mla_q_norm[l], (0, MLA_QK_PAD - MLA_QK)).reshape(1, MLA_QK_PAD)
        kg = jnp.pad(mla_k_norm[l], (0, MLA_QK_PAD - MLA_QK)).reshape(1, MLA_QK_PAD)
        wa_pad = jnp.zeros((SMALL_W, GLA_HEADS * GLA_DK), F32).at[
            MLA_ROPE:MLA_ROPE + GLA_GATE_RANK].set(gla_w_a2[l])

        h, small = _norm_call(x2, norm_g[l].reshape(1, D), w_small, l, tm_big)
        p_lin = _inproj_call(h, w_lin, l, _identity, tm_proj)
        p_silu = _inproj_call(h, w_silu, l, _identity, tm_proj)
        p_gelu = _inproj_call(h, w_gelu, l, _gelu_tanh, tm_proj)
        p_gate = _inproj_call(h, w_gate, l, _identity, tm_proj)
        q, k, vt = _mla_prep_call(p_lin, small, cosf, sins,
                                  mla_cq_norm[l].reshape(1, -1), mla_ckv_norm[l].reshape(1, -1),
                                  wuq, wuk, wuv, qg, kg, tm_big, tq)
        ya = _attn_call(q, k, vt, p_silu, B, S, tq)
        yc = _gla_call(p_lin, p_silu, small, wa_pad, gla_b_a[l].reshape(1, -1),
                       gla_o_norm[l].reshape(1, -1), B, S, cpb)
        merged = _merge_call(ya, p_gelu, p_silu, yc, p_gate, sgu_v_norm[l].reshape(1, -1), sgu_w_s[l],
                             sgu_b_s[l].T, wb, l, tm_row, S, tq)
        x2 = _out_call(x2, merged, wo, l, tm_row)
    return x2.reshape(B, S, D)
```

```python
import functools
import math

import jax
import jax.numpy as jnp
from jax import lax
from jax.experimental import pallas as pl
from jax.experimental.pallas import tpu as pltpu

F32 = jnp.float32
BF16 = jnp.bfloat16

MLA_HEADS = 8
MLA_NOPE = 128
MLA_ROPE = 64
MLA_QK = MLA_NOPE + MLA_ROPE
MLA_V = 128
MLA_Q_RANK = 512
MLA_KV_RANK = 512
MLA_WIDTH = MLA_HEADS * MLA_V
MLA_QK_PAD = 256
MLA_VT_ROWS = MLA_V + 16
SUBLANE = 8
ORDER_LAG = 4
ROPE_THETA = 10000.0
SGU_GROUPS = 8
SGU_CHUNK = 128
SGU_WIDTH = 1024
GLA_HEADS = 4
GLA_DK = 128
GLA_DV = 256
GLA_GATE_RANK = 16
GLA_TAU = 16.0
GLA_CHUNK = 128
GLA_WIDTH = GLA_HEADS * GLA_DV
D_MODEL = 2048
N_BRANCH = 3
BRANCH_WIDTH = 1024
EPS = 1e-6
LANE = 128
LOG2_E = math.log2(math.e)
MASK_VALUE = -1e30
SMALL_W = LANE

PROJ_TN = 1024
LIN_CQKV, LIN_QK_C, LIN_V_C = range(3)
SILU_Z_A, SILU_Z_B, SILU_Z_C = range(3)
GELU_U_B, GELU_V_B = range(2)

VMEM_LIMIT = 56 * 1024 * 1024


def _params(*sem):
    return pltpu.CompilerParams(dimension_semantics=sem, vmem_limit_bytes=VMEM_LIMIT)


def _dot(a, b):
    return jnp.dot(a, b, preferred_element_type=F32)


def _dot_nt(a, b):
    return lax.dot_general(a, b, (((1,), (1,)), ((), ())), preferred_element_type=F32)


def _dot_tn(a, b):
    return lax.dot_general(a, b, (((0,), (0,)), ((), ())), preferred_element_type=F32)


def _sigmoid(x):
    return 0.5 * jnp.tanh(0.5 * x) + 0.5


def _gelu_tanh(x):
    c = math.sqrt(2.0 / math.pi)
    hx = 0.5 * x
    return hx * jnp.tanh(x * (c + (c * 0.044715) * (x * x))) + hx


def _silu(x):
    hx = 0.5 * x
    return hx * jnp.tanh(hx) + hx


def _identity(x):
    return x


def _resident(shape, layer=None):
    nd = len(shape)
    if layer is None:
        return pl.BlockSpec(shape, lambda *_: (0,) * nd, pipeline_mode=pl.Buffered(1))
    return pl.BlockSpec((None,) + tuple(shape), lambda *_: (layer,) + (0,) * nd,
                        pipeline_mode=pl.Buffered(1))


def _norm_kernel(x_ref, g_ref, ws_ref, h_ref, small_ref):
    x = x_ref[...]
    rstd = lax.rsqrt(jnp.mean(x * x, axis=-1, keepdims=True) + EPS)
    h = (x * rstd * g_ref[...]).astype(BF16)
    h_ref[...] = h
    small_ref[...] = _dot_nt(h, ws_ref[...])


def _norm_call(x2, g, w_small, layer, tm):
    T, D = x2.shape
    return pl.pallas_call(
        _norm_kernel,
        grid=(T // tm,),
        in_specs=[pl.BlockSpec((tm, D), lambda i: (i, 0)),
                  _resident((1, D)),
                  _resident((SMALL_W, D), layer)],
        out_specs=[pl.BlockSpec((tm, D), lambda i: (i, 0)),
                   pl.BlockSpec((tm, SMALL_W), lambda i: (i, 0))],
        out_shape=[jax.ShapeDtypeStruct((T, D), BF16),
                   jax.ShapeDtypeStruct((T, SMALL_W), F32)],
        compiler_params=_params("parallel"),
    )(x2, g, w_small)


def _inproj_kernel(h_ref, w_ref, o_ref, *, act):
    o_ref[...] = act(_dot_nt(h_ref[...], w_ref[...])).astype(o_ref.dtype)


def _inproj_call(h, w, layer, act, tm):
    T, D = h.shape
    N = w.shape[1]
    return pl.pallas_call(
        functools.partial(_inproj_kernel, act=act),
        grid=(T // tm, N // PROJ_TN),
        in_specs=[pl.BlockSpec((tm, D), lambda i, j: (i, 0)),
                  pl.BlockSpec((None, PROJ_TN, D), lambda i, j: (layer, j, 0))],
        out_specs=pl.BlockSpec((tm, PROJ_TN), lambda i, j: (i, j)),
        out_shape=jax.ShapeDtypeStruct((T, N), BF16),
        compiler_params=_params("parallel", "arbitrary"),
    )(h, w)


def _rope128(r, cosf, sins):
    lane = lax.broadcasted_iota(jnp.int32, r.shape, 1)
    partner = jnp.where(lane < MLA_ROPE // 2,
                        pltpu.roll(r, LANE - MLA_ROPE // 2, axis=1),
                        pltpu.roll(r, MLA_ROPE // 2, axis=1))
    return r * cosf + partner * sins


def _mla_prep_kernel(cq_ref, ckv_ref, small_ref, cos_ref, sin_ref,
                     cqg_ref, ckvg_ref, wuq_ref, wuk_ref, wuv_ref, qg_ref, kg_ref,
                     q_ref, k_ref, vt_ref, *, tq):
    cosf = cos_ref[...]
    sins = sin_ref[...]
    scale = MLA_QK ** -0.5 * LOG2_E

    cq = cq_ref[...].astype(F32)
    nq = cq * lax.rsqrt(jnp.mean(cq * cq, axis=-1, keepdims=True) + EPS) * cqg_ref[...]
    q_all = _dot(nq.astype(BF16), wuq_ref[...])

    ckv = ckv_ref[...].astype(F32)
    nkv = ckv * lax.rsqrt(jnp.mean(ckv * ckv, axis=-1, keepdims=True) + EPS) * ckvg_ref[...]
    nkv = nkv.astype(BF16)
    k_all = _dot(nkv, wuk_ref[...])
    vt = _dot_nt(wuv_ref[...], nkv)
    ones_row = (lax.broadcasted_iota(jnp.int32, (MLA_VT_ROWS - MLA_V, tq), 0) == 0)
    for sub in range(vt.shape[1] // tq):
        for hd in range(MLA_HEADS):
            vt_ref[sub, hd, :MLA_V, :] = vt[hd * MLA_V:(hd + 1) * MLA_V,
                                            sub * tq:(sub + 1) * tq].astype(vt_ref.dtype)
            vt_ref[sub, hd, MLA_V:, :] = ones_row.astype(vt_ref.dtype)

    qg = qg_ref[...]
    kg = kg_ref[...]
    lane = lax.broadcasted_iota(jnp.int32, (1, LANE), 1)
    kr = jnp.where(lane < MLA_ROPE, small_ref[...], 0.0)
    kr_ss = jnp.sum(kr * kr, axis=-1, keepdims=True)
    kr_rot = _rope128(kr * kg[:, MLA_NOPE:], cosf, sins)

    for hd in range(MLA_HEADS):
        qb = q_all[:, hd * MLA_QK_PAD:(hd + 1) * MLA_QK_PAD]
        rq = lax.rsqrt(jnp.sum(qb * qb, axis=-1, keepdims=True) * (1.0 / MLA_QK) + EPS) * scale
        qn = qb * rq * qg
        q_ref[:, hd * MLA_QK_PAD:hd * MLA_QK_PAD + MLA_NOPE] = qn[:, :MLA_NOPE].astype(q_ref.dtype)
        q_ref[:, hd * MLA_QK_PAD + MLA_NOPE:(hd + 1) * MLA_QK_PAD] = _rope128(
            qn[:, MLA_NOPE:], cosf, sins).astype(q_ref.dtype)

        kb = k_all[:, hd * MLA_NOPE:(hd + 1) * MLA_NOPE]
        rk = lax.rsqrt((jnp.sum(kb * kb, axis=-1, keepdims=True) + kr_ss) * (1.0 / MLA_QK) + EPS)
        k_ref[:, hd * MLA_QK_PAD:hd * MLA_QK_PAD + MLA_NOPE] = (
            kb * rk * kg[:, :MLA_NOPE]).astype(k_ref.dtype)
        k_ref[:, hd * MLA_QK_PAD + MLA_NOPE:(hd + 1) * MLA_QK_PAD] = (kr_rot * rk).astype(k_ref.dtype)


def _mla_prep_call(proj, small, cosf, sins, cqg, ckvg, wuq, wuk, wuv, qg, kg, tm, tq):
    T = proj.shape[0]
    assert tm % tq == 0
    HQ = MLA_HEADS * MLA_QK_PAD
    return pl.pallas_call(
        functools.partial(_mla_prep_kernel, tq=tq),
        grid=(T // tm,),
        in_specs=[pl.BlockSpec((tm, MLA_Q_RANK), lambda i: (i, 0)),
                  pl.BlockSpec((tm, MLA_KV_RANK), lambda i: (i, 1)),
                  pl.BlockSpec((tm, SMALL_W), lambda i: (i, 0)),
                  pl.BlockSpec((tm, LANE), lambda i: (i, 0)),
                  pl.BlockSpec((tm, LANE), lambda i: (i, 0)),
                  _resident((1, MLA_Q_RANK)),
                  _resident((1, MLA_KV_RANK)),
                  _resident((MLA_Q_RANK, HQ)),
                  _resident((MLA_KV_RANK, MLA_HEADS * MLA_NOPE)),
                  _resident((MLA_WIDTH, MLA_KV_RANK)),
                  _resident((1, MLA_QK_PAD)),
                  _resident((1, MLA_QK_PAD))],
        out_specs=[pl.BlockSpec((tm, HQ), lambda i: (i, 0)),
                   pl.BlockSpec((tm, HQ), lambda i: (i, 0)),
                   pl.BlockSpec((tm // tq, MLA_HEADS, MLA_VT_ROWS, tq), lambda i: (i, 0, 0, 0))],
        out_shape=[jax.ShapeDtypeStruct((T, HQ), BF16),
                   jax.ShapeDtypeStruct((T, HQ), BF16),
                   jax.ShapeDtypeStruct((T // tq, MLA_HEADS, MLA_VT_ROWS, tq), BF16)],
        compiler_params=_params("parallel"),
    )(proj, proj, small, cosf, sins, cqg, ckvg, wuq, wuk, wuv, qg, kg)


def _attn_kernel(qlo_ref, qhi_ref, k_ref, vt_ref, szlo_ref, szhi_ref, o_ref,
                 q2_ref, s_ref, mf_ref, pv_ref, *, tq, nq, n_steps):
    g = pl.program_id(0)
    half = nq // 2
    nt = nq + 1
    i1 = jnp.minimum(g, n_steps - 1) % half
    i2 = jnp.maximum(g - 1, 0) % half

    @pl.when(g == 0)
    def _():
        for t in range(nt):
            s_ref[t] = jnp.zeros((tq, tq), F32)
            mf_ref[t] = jnp.zeros((SUBLANE, tq), F32)

    q2_ref[0] = qlo_ref[...]
    q2_ref[1] = qhi_ref[...]

    def is_lo(p, i):
        return True if p == 0 else (False if p >= half else p <= i)

    def pick(p, i, lo, hi):
        c = is_lo(p, i)
        if c is True:
            return lo
        if c is False:
            return hi
        return jnp.where(c, lo, hi)

    def kv_tile(p, i):
        if p == 0:
            return i
        if p == nq:
            return nq - 1 - i
        return pick(p, i, p - 1, p - i - 1)

    def combine(ref, shape, i, op, init):
        lo = jnp.full(shape, init, F32)
        hi = jnp.full(shape, init, F32)
        for p in range(nt):
            c = is_lo(p, i)
            x = ref[p]
            if c is True:
                lo = op(lo, x)
            elif c is False:
                hi = op(hi, x)
            else:
                lo = op(lo, jnp.where(c, x, init))
                hi = op(hi, jnp.where(c, init, x))
        return lo, hi

    m_lo, m_hi = combine(mf_ref, (SUBLANE, tq), i2, jnp.maximum, MASK_VALUE)
    m_lo = jnp.max(m_lo, axis=0, keepdims=True)
    m_hi = jnp.max(m_hi, axis=0, keepdims=True)

    causal = (lax.broadcasted_iota(jnp.int32, (tq, tq), 0)
              <= lax.broadcasted_iota(jnp.int32, (tq, tq), 1))
    order_zeros = [jnp.zeros((1, tq), F32)] * ORDER_LAG
    for p in range(nt):
        e = jnp.exp2(s_ref[p] - (pick(p, i2, m_lo, m_hi) + order_zeros[0]))
        pv_ref[p] = _dot(vt_ref[kv_tile(p, i2)], e.astype(BF16))
        j = kv_tile(p, i1)
        s = _dot_nt(k_ref[pl.ds(pl.multiple_of(j * tq, tq), tq), :], q2_ref[pick(p, i1, 0, 1)])
        if p == 0 or p == nq:
            s = jnp.where(causal, s, MASK_VALUE)
        s_ref[p] = s
        mf = jnp.max(s.reshape(tq // SUBLANE, SUBLANE, tq), axis=0)
        mf_ref[p] = mf
        bits = lax.bitcast_convert_type(mf[:1, :], jnp.uint32)
        order_zeros = order_zeros[1:] + [lax.bitcast_convert_type((bits >> 16) >> 16, F32)]

    a_lo, a_hi = combine(pv_ref, (MLA_VT_ROWS, tq), i2, jnp.add, 0.0)
    for a, sz_ref, rows in ((a_lo, szlo_ref, slice(0, tq)), (a_hi, szhi_ref, slice(tq, 2 * tq))):
        ot = a[:MLA_V, :] / a[MLA_V:MLA_V + 1, :]
        o_ref[rows, :] = (ot.T * _silu(sz_ref[...].astype(F32))).astype(o_ref.dtype)


def _attn_call(q, k, vt, p_silu, B, S, tq):
    T = q.shape[0]
    nq = S // tq
    assert nq % 2 == 0 and MLA_V == LANE
    half = nq // 2
    n_steps = B * MLA_HEADS * half
    sz_col0 = SILU_Z_A * PROJ_TN // MLA_V

    def bhi(g):
        bh, i = g // half, g % half
        return bh // MLA_HEADS, bh % MLA_HEADS, i

    def first(g):
        return bhi(jnp.minimum(g, n_steps - 1))

    def second(g):
        return bhi(jnp.maximum(g - 1, 0))

    def lo_blk(f, col0=0):
        def index(g):
            b, h, i = f(g)
            return b * nq + i, col0 + h
        return index

    def hi_blk(f, col0=0):
        def index(g):
            b, h, i = f(g)
            return b * nq + nq - 1 - i, col0 + h
        return index

    def seq_blk(f):
        def index(g):
            b, h, _ = f(g)
            return b, h
        return index

    def seq_blk4(f):
        def index(g):
            b, h, _ = f(g)
            return b, h, 0, 0
        return index

    def out_blk(g):
        b, h, i = second(g)
        return b * half + i, h

    nt = nq + 1
    return pl.pallas_call(
        functools.partial(_attn_kernel, tq=tq, nq=nq, n_steps=n_steps),
        grid=(n_steps + 1,),
        in_specs=[pl.BlockSpec((tq, MLA_QK_PAD), lo_blk(first)),
                  pl.BlockSpec((tq, MLA_QK_PAD), hi_blk(first)),
                  pl.BlockSpec((S, MLA_QK_PAD), seq_blk(first)),
                  pl.BlockSpec((nq, None, MLA_VT_ROWS, tq), seq_blk4(second)),
                  pl.BlockSpec((tq, MLA_V), lo_blk(second, sz_col0)),
                  pl.BlockSpec((tq, MLA_V), hi_blk(second, sz_col0))],
        out_specs=pl.BlockSpec((2 * tq, MLA_V), out_blk),
        out_shape=jax.ShapeDtypeStruct((T, MLA_WIDTH), BF16),
        scratch_shapes=[pltpu.VMEM((2, tq, MLA_QK_PAD), BF16),
                        pltpu.VMEM((nt, tq, tq), F32),
                        pltpu.VMEM((nt, SUBLANE, tq), F32),
                        pltpu.VMEM((nt, MLA_VT_ROWS, tq), F32)],
        compiler_params=_params("arbitrary"),
    )(q, q, k, vt, p_silu, p_silu)


def _paired_block(r, S, tq, tm):
    per_seq, per_tile, nq = S // tm, tq // tm, S // tq
    b, w = r // per_seq, r % per_seq
    u, sub = w // per_tile, w % per_tile
    pos = jnp.where(u < nq // 2, 2 * u, 2 * (nq - 1 - u) + 1)
    return b * per_seq + pos * per_tile + sub


def _sgu_tile(u_ref, v_ref, sz_ref, vg_ref, ws_ref, bs_ref, o_ref, tm):
    v = v_ref[...].astype(F32)
    vn = (v * lax.rsqrt(jnp.mean(v * v, axis=-1, keepdims=True) + EPS) * vg_ref[...]).astype(BF16)
    row = lax.broadcasted_iota(jnp.int32, (SGU_CHUNK, SGU_CHUNK), 0)
    col = lax.broadcasted_iota(jnp.int32, (SGU_CHUNK, SGU_CHUNK), 1)
    gd = SGU_WIDTH // SGU_GROUPS
    for g in range(SGU_GROUPS):
        w = jnp.where(row >= col, ws_ref[g], 0.0).astype(BF16)
        bias = bs_ref[:, g:g + 1]
        for c in range(tm // SGU_CHUNK):
            rs = slice(c * SGU_CHUNK, (c + 1) * SGU_CHUNK)
            cs = slice(g * gd, (g + 1) * gd)
            mix = _dot(w, vn[rs, cs]) + bias
            o_ref[rs, cs] = (u_ref[rs, cs].astype(F32) * mix
                             * _silu(sz_ref[rs, cs].astype(F32))).astype(o_ref.dtype)


def _split_bf16(x):
    hi = x.astype(BF16)
    lo = (x - hi.astype(F32)).astype(BF16)
    return hi, lo


def _gla_kernel(qk_ref, v_ref, sz_ref, small_ref, wa_ref, ba_ref, og_ref, o_ref, st_ref, *, cpb):
    L = GLA_CHUNK

    @pl.when(pl.program_id(1) == 0)
    def _():
        st_ref[...] = jnp.zeros(st_ref.shape, F32)

    a_hi, a_lo = _split_bf16(small_ref[...])
    w_hi, w_lo = _split_bf16(wa_ref[...])
    xg = _dot(a_hi, w_hi) + _dot(a_lo, w_hi) + _dot(a_hi, w_lo) + ba_ref[...]
    log_a = (jnp.minimum(xg, 0.0) - jnp.log(1.0 + jnp.exp(-jnp.abs(xg)))) * (1.0 / GLA_TAU)
    row = lax.broadcasted_iota(jnp.int32, (L, L), 0)
    col = lax.broadcasted_iota(jnp.int32, (L, L), 1)
    causal = row >= col
    ones_tril = causal.astype(BF16)
    og = og_ref[...]
    states = [st_ref[hd] for hd in range(GLA_HEADS)]

    for c in range(cpb):
        rs = slice(c * L, (c + 1) * L)
        la_hi, la_lo = _split_bf16(log_a[rs, :])
        bcum = _dot(ones_tril, la_hi) + _dot(ones_tril, la_lo)
        for hd in range(GLA_HEADS):
            ks = slice(hd * GLA_DK, (hd + 1) * GLA_DK)
            k2 = slice(GLA_HEADS * GLA_DK + hd * GLA_DK, GLA_HEADS * GLA_DK + (hd + 1) * GLA_DK)
            vs = slice(hd * GLA_DV, (hd + 1) * GLA_DV)
            b = bcum[:, ks]
            b_mid = b[L // 2:L // 2 + 1, :]
            b_last = b[L - 1:L, :]
            q = qk_ref[rs, ks].astype(F32) * (GLA_DK ** -0.5)
            k = qk_ref[rs, k2].astype(F32)
            v = v_ref[rs, vs]
            q_t = (q * jnp.exp(b - b_mid)).astype(BF16)
            k_t = (k * jnp.exp(b_mid - b)).astype(BF16)
            attn = jnp.where(causal, _dot_nt(q_t, k_t), 0.0).astype(BF16)
            st = states[hd]
            o = _dot_nt((q * jnp.exp(b)).astype(BF16), st.astype(BF16)) + _dot(attn, v)
            k_s = (k * jnp.exp(b_last - b)).astype(BF16)
            states[hd] = jnp.exp(b_last) * st + _dot_tn(v, k_s)
            on = o * lax.rsqrt(jnp.mean(o * o, axis=-1, keepdims=True) + EPS) * og
            o_ref[rs, vs] = (on * _silu(sz_ref[rs, vs].astype(F32))).astype(o_ref.dtype)

    for hd in range(GLA_HEADS):
        st_ref[hd] = states[hd]


def _gla_call(p_lin, p_silu, small, wa_pad, ba, og, B, S, cpb):
    T = p_lin.shape[0]
    R = cpb * GLA_CHUNK
    ns = S // R
    return pl.pallas_call(
        functools.partial(_gla_kernel, cpb=cpb),
        grid=(B, ns),
        in_specs=[pl.BlockSpec((R, PROJ_TN), lambda b, c: (b * ns + c, LIN_QK_C)),
                  pl.BlockSpec((R, GLA_WIDTH), lambda b, c: (b * ns + c, LIN_V_C)),
                  pl.BlockSpec((R, GLA_WIDTH), lambda b, c: (b * ns + c, SILU_Z_C)),
                  pl.BlockSpec((R, SMALL_W), lambda b, c: (b * ns + c, 0)),
                  _resident((SMALL_W, GLA_HEADS * GLA_DK)),
                  _resident((1, GLA_HEADS * GLA_DK)),
                  _resident((1, GLA_DV))],
        out_specs=pl.BlockSpec((R, GLA_WIDTH), lambda b, c: (b * ns + c, 0)),
        out_shape=jax.ShapeDtypeStruct((T, GLA_WIDTH), BF16),
        scratch_shapes=[pltpu.VMEM((GLA_HEADS, GLA_DV, GLA_DK), F32)],
        compiler_params=_params("parallel", "arbitrary"),
    )(p_lin, p_lin, p_silu, small, wa_pad, ba, og)


def _merge_kernel(ya_ref, u_ref, v_ref, szb_ref, yc_ref, ga_ref, gb_ref, gc_ref,
                  vg_ref, ws_ref, bs_ref, wb_ref, o_ref, yb_ref, *, tm):
    _sgu_tile(u_ref, v_ref, szb_ref, vg_ref, ws_ref, bs_ref, yb_ref, tm)
    merged = _sigmoid(ga_ref[...].astype(F32)) * _dot(ya_ref[...], wb_ref[0])
    merged += _sigmoid(gb_ref[...].astype(F32)) * _dot(yb_ref[...], wb_ref[1])
    merged += _sigmoid(gc_ref[...].astype(F32)) * _dot(yc_ref[...], wb_ref[2])
    o_ref[...] = merged.astype(o_ref.dtype)


def _merge_call(ya, p_gelu, p_silu, yc, p_gate, vg, ws, bs_t, wb, layer, tm, S, tq):
    T = ya.shape[0]
    D = wb.shape[-1]
    assert SGU_WIDTH == BRANCH_WIDTH
    yspec = pl.BlockSpec((tm, BRANCH_WIDTH), lambda i: (i, 0))
    ya_spec = pl.BlockSpec((tm, BRANCH_WIDTH), lambda i: (_paired_block(i, S, tq, tm), 0))
    return pl.pallas_call(
        functools.partial(_merge_kernel, tm=tm),
        grid=(T // tm,),
        in_specs=[ya_spec,
                  pl.BlockSpec((tm, SGU_WIDTH), lambda i: (i, GELU_U_B)),
                  pl.BlockSpec((tm, SGU_WIDTH), lambda i: (i, GELU_V_B)),
                  pl.BlockSpec((tm, SGU_WIDTH), lambda i: (i, SILU_Z_B)),
                  yspec,
                  pl.BlockSpec((tm, D), lambda i: (i, 0)),
                  pl.BlockSpec((tm, D), lambda i: (i, 1)),
                  pl.BlockSpec((tm, D), lambda i: (i, 2)),
                  _resident((1, SGU_WIDTH)),
                  _resident((SGU_GROUPS, SGU_CHUNK, SGU_CHUNK)),
                  _resident((SGU_CHUNK, SGU_GROUPS)),
                  _resident((N_BRANCH, BRANCH_WIDTH, D), layer)],
        out_specs=pl.BlockSpec((tm, D), lambda i: (i, 0)),
        out_shape=jax.ShapeDtypeStruct((T, D), BF16),
        scratch_shapes=[pltpu.VMEM((tm, SGU_WIDTH), BF16)],
        compiler_params=_params("parallel"),
    )(ya, p_gelu, p_gelu, p_silu, yc, p_gate, p_gate, p_gate, vg, ws, bs_t, wb)


def _out_kernel(x_ref, m_ref, wo_ref, o_ref):
    o_ref[...] = x_ref[...] + _dot(m_ref[...], wo_ref[...])


def _out_call(x2, merged, wo, layer, tm):
    T, D = x2.shape
    return pl.pallas_call(
        _out_kernel,
        grid=(T // tm,),
        in_specs=[pl.BlockSpec((tm, D), lambda i: (i, 0)),
                  pl.BlockSpec((tm, D), lambda i: (i, 0)),
                  _resident((D, D), layer)],
        out_specs=pl.BlockSpec((tm, D), lambda i: (i, 0)),
        out_shape=jax.ShapeDtypeStruct((T, D), F32),
        compiler_params=_params("parallel"),
    )(x2, merged, wo)


_IN_SIZES = (("c_q", MLA_Q_RANK), ("c_kv", MLA_KV_RANK), ("k_rope", MLA_ROPE), ("z_a", MLA_WIDTH),
             ("u_b", SGU_WIDTH), ("v_b", SGU_WIDTH), ("z_b", SGU_WIDTH),
             ("q_c", GLA_HEADS * GLA_DK), ("k_c", GLA_HEADS * GLA_DK), ("v_c", GLA_WIDTH),
             ("a_r", GLA_GATE_RANK), ("z_c", GLA_WIDTH), ("gates", N_BRANCH * D_MODEL))
_IN_SEG = {}
_off = 0
for _name, _size in _IN_SIZES:
    _IN_SEG[_name] = (_off, _size)
    _off += _size
IN_COLS = _off
_W_GROUPS = (("c_q", "c_kv", "q_c", "k_c", "v_c"), ("z_a", "z_b", "z_c"), ("u_b", "v_b"),
             ("gates",), ("k_rope", "a_r"))


def _wprep_kernel(w_ref, lin_ref, silu_ref, gelu_ref, gate_ref, small_ref):
    small_ref[...] = jnp.zeros(small_ref.shape, small_ref.dtype)
    for o_ref, names in zip((lin_ref, silu_ref, gelu_ref, gate_ref, small_ref), _W_GROUPS):
        dst = 0
        for name in names:
            src, size = _IN_SEG[name]
            o_ref[dst:dst + size, :] = w_ref[src:src + size, :].astype(o_ref.dtype)
            dst += size


def _split_w_in(w_in, tk):
    depth, D, cols = w_in.shape
    assert cols == IN_COLS
    widths = [sum(_IN_SEG[n][1] for n in names) for names in _W_GROUPS[:-1]] + [SMALL_W]
    return pl.pallas_call(
        _wprep_kernel,
        grid=(depth, D // tk),
        in_specs=[pl.BlockSpec((None, cols, tk), lambda l, r: (l, 0, r))],
        out_specs=[pl.BlockSpec((None, n, tk), lambda l, r: (l, 0, r)) for n in widths],
        out_shape=[jax.ShapeDtypeStruct((depth, n, D), BF16) for n in widths],
        compiler_params=_params("parallel", "parallel"),
    )(jnp.swapaxes(w_in, 1, 2))


def _pad_heads(w, per_head, lo, hi, width):
    r = w.shape[0]
    w = w.reshape(r, MLA_HEADS, per_head)[:, :, lo:hi]
    w = jnp.pad(w, ((0, 0), (0, 0), (0, width - (hi - lo))))
    return w.reshape(r, MLA_HEADS * width)


def _rope_tables(positions):
    half = MLA_ROPE // 2
    inv_freq = 1.0 / (ROPE_THETA ** (jnp.arange(0, MLA_ROPE, 2, dtype=F32) / MLA_ROPE))
    inv_freq = jnp.concatenate([inv_freq, inv_freq, jnp.zeros((LANE - 2 * half,), F32)])
    ang = positions.astype(F32).reshape(-1, 1) * inv_freq
    lane = jnp.arange(LANE)
    cos, sin = jnp.cos(ang), jnp.sin(ang)
    return (jnp.where(lane < 2 * half, cos, 0.0),
            jnp.where(lane < half, -sin, jnp.where(lane < 2 * half, sin, 0.0)))


def _pick_tile(n, want):
    t = min(n, want)
    while n % t:
        t //= 2
    return t


def kernel(x, positions, norm_g, w_in, mla_cq_norm, mla_ckv_norm, mla_w_uq, mla_w_ukv,
           mla_q_norm, mla_k_norm, sgu_v_norm, sgu_w_s, sgu_b_s, gla_w_a2, gla_b_a,
           gla_o_norm, w_branch, w_out):
    B, S, D = x.shape
    T = B * S
    depth = w_in.shape[0]
    tm_proj = _pick_tile(T, 2048)
    tm_row = _pick_tile(T, 512)
    tm_big = _pick_tile(T, 1024)
    tq = _pick_tile(S, 512)
    cpb = _pick_tile(S // GLA_CHUNK, 8)

    cosf, sins = _rope_tables(positions)
    x2 = x.reshape(T, D)
    w_lin, w_silu, w_gelu, w_gate, w_small = _split_w_in(w_in, _pick_tile(D, 256))
    wb, wo = w_branch.astype(BF16), w_out.astype(BF16)
    for l in range(depth):
        wuq = _pad_heads(mla_w_uq[l], MLA_QK, 0, MLA_QK, MLA_QK_PAD).astype(BF16)
        wuk = _pad_heads(mla_w_ukv[l], MLA_NOPE + MLA_V, 0, MLA_NOPE, MLA_NOPE).astype(BF16)
        wuv = _pad_heads(mla_w_ukv[l], MLA_NOPE + MLA_V, MLA_NOPE, MLA_NOPE + MLA_V, MLA_V).T.astype(BF16)
        qg = jnp.pad(mla_q_norm[l], (0, MLA_QK_PAD - MLA_QK)).reshape(1, MLA_QK_PAD)
        kg = jnp.pad(mla_k_norm[l], (0, MLA_QK_PAD - MLA_QK)).reshape(1, MLA_QK_PAD)
        wa_pad = jnp.zeros((SMALL_W, GLA_HEADS * GLA_DK), F32).at[
            MLA_ROPE:MLA_ROPE + GLA_GATE_RANK].set(gla_w_a2[l])

        h, small = _norm_call(x2, norm_g[l].reshape(1, D), w_small, l, tm_big)
        p_lin = _inproj_call(h, w_lin, l, _identity, tm_proj)
        p_silu = _inproj_call(h, w_silu, l, _identity, tm_proj)
        p_gelu = _inproj_call(h, w_gelu, l, _gelu_tanh, tm_proj)
        p_gate = _inproj_call(h, w_gate, l, _identity, tm_proj)
        q, k, vt = _mla_prep_call(p_lin, small, cosf, sins,
                                  mla_cq_norm[l].reshape(1, -1), mla_ckv_norm[l].reshape(1, -1),
                                  wuq, wuk, wuv, qg, kg, tm_big, tq)
        ya = _attn_call(q, k, vt, p_silu, B, S, tq)
        yc = _gla_call(p_lin, p_silu, small, wa_pad, gla_b_a[l].reshape(1, -1),
                       gla_o_norm[l].reshape(1, -1), B, S, cpb)
        merged = _merge_call(ya, p_gelu, p_silu, yc, p_gate, sgu_v_norm[l].reshape(1, -1), sgu_w_s[l],
                             sgu_b_s[l].T, wb, l, tm_row, S, tq)
        x2 = _out_call(x2, merged, wo, l, tm_row)
    return x2.reshape(B, S, D)
```
